```python
import math
import jax, jax.numpy as jnp
from jax import lax
import numpy as np

D_MODEL = 1024
BATCH = 4
SEQ = 8192
DEPTH = 4

N_MEM = 256
HEAD_DIM = 64
ROT_DIM = HEAD_DIM // 4
ROPE_THETA = 500000.0
EPS = 1e-6
NEG_INF = -1e30
N_MIXERS = 3

A_HEADS = D_MODEL // HEAD_DIM
A_WIDTH = A_HEADS * HEAD_DIM
A_PATTERNS = ((128, 1), (512, 4), (2048, 16))
A_BLOCK = 64

B_HEADS = D_MODEL // (2 * HEAD_DIM)
B_WIDTH = B_HEADS * 2 * HEAD_DIM
B_QBLOCK = 128

C_HEADS = D_MODEL // HEAD_DIM
C_KV_HEADS = max(1, C_HEADS // 8)
C_GROUP = C_HEADS // C_KV_HEADS
C_Q_WIDTH = C_HEADS * HEAD_DIM
C_KV_WIDTH = C_KV_HEADS * HEAD_DIM
C_RADIUS = 128
C_BLOCK = 128

X_HEADS = 4
X_HEAD_DIM = D_MODEL // X_HEADS

_FF_RAW = -(-8 * D_MODEL // 3)
D_FF = -(-_FF_RAW // 256) * 256

N_A = len(range(0, DEPTH, N_MIXERS))
N_B = len(range(1, DEPTH, N_MIXERS))
N_C = len(range(2, DEPTH, N_MIXERS))

kernel_name = "hybrid_dilated_diff_swa_encoder"


def rms_norm(x, g):
    xf = x.astype(jnp.float32)
    y = xf * lax.rsqrt(jnp.mean(xf * xf, axis=-1, keepdims=True) + EPS)
    return (y * g.astype(jnp.float32)).astype(x.dtype)


def split_heads(t, n):
    b, s, _ = t.shape
    return t.reshape(b, s, n, -1).transpose(0, 2, 1, 3)


def merge_heads(t):
    b, n, s, d = t.shape
    return t.transpose(0, 2, 1, 3).reshape(b, s, n * d)


def rope_tables(positions):
    inv_freq = ROPE_THETA ** (-jnp.arange(0, ROT_DIM, 2, dtype=jnp.float32) / ROT_DIM)
    ang = positions.astype(jnp.float32)[..., None] * inv_freq
    return jnp.cos(ang), jnp.sin(ang)


def apply_partial_rope(t, cos, sin):
    shape = (cos.shape[0],) + (1,) * (t.ndim - 3) + cos.shape[1:]
    c = cos.reshape(shape).astype(t.dtype)
    s = sin.reshape(shape).astype(t.dtype)
    half = ROT_DIM // 2
    t1, t2, rest = t[..., :half], t[..., half:ROT_DIM], t[..., ROT_DIM:]
    return jnp.concatenate([t1 * c - t2 * s, t2 * c + t1 * s, rest], axis=-1)


def banded_attention(q, k, v, radius, block, sink=None):
    L = q.shape[-2]
    nb = -(-L // block)
    pad = nb * block - L

    def pad_seq(t, lo, hi):
        return jnp.pad(t, [(0, 0)] * (t.ndim - 2) + [(lo, hi), (0, 0)])

    qb = pad_seq(q, 0, pad).reshape(q.shape[:-2] + (nb, block, q.shape[-1]))

    def band(t):
        tp = pad_seq(t, block, block + pad).reshape(t.shape[:-2] + (nb + 2, block, t.shape[-1]))
        return jnp.concatenate([tp[..., :-2, :, :], tp[..., 1:-1, :, :], tp[..., 2:, :, :]], axis=-2)

    kb, vb = band(k), band(v)
    s = jnp.einsum('...gnqd,...nkd->...gnqk', qb, kb,
                   preferred_element_type=jnp.float32) * (q.shape[-1] ** -0.5)
    qpos = jnp.arange(nb * block).reshape(nb, block, 1)
    kpos = (jnp.arange(nb)[:, None, None] - 1) * block + jnp.arange(3 * block)[None, None, :]
    valid = (jnp.abs(kpos - qpos) <= radius) & (kpos >= 0) & (kpos < L)
    s = jnp.where(valid, s, NEG_INF)
    m = jnp.max(s, axis=-1, keepdims=True)
    if sink is not None:
        sink = sink.astype(jnp.float32)
        m = jnp.maximum(m, sink)
    p = jnp.exp(s - m)
    denom = jnp.sum(p, axis=-1, keepdims=True)
    if sink is not None:
        denom = denom + jnp.exp(sink - m)
    o = jnp.einsum('...gnqk,...nkd->...gnqd', p.astype(v.dtype), vb,
                   preferred_element_type=jnp.float32) / denom
    lse = (m + jnp.log(denom))[..., 0]
    o = o.reshape(o.shape[:-3] + (nb * block, o.shape[-1]))[..., :L, :]
    lse = lse.reshape(lse.shape[:-2] + (nb * block,))[..., :L]
    return o, lse


def dilated_attention_mixer(u, w_in, w_out, cos, sin):
    b, s_len, _ = u.shape
    q, k, v = jnp.split(u @ w_in, 3, axis=-1)
    q = apply_partial_rope(split_heads(q, A_HEADS), cos, sin)
    k = apply_partial_rope(split_heads(k, A_HEADS), cos, sin)
    v = split_heads(v, A_HEADS)
    outs, lses = [], []
    for window, dil in A_PATTERNS:
        radius = window // (2 * dil)

        def by_residue(t):
            return t.reshape(b, A_HEADS, s_len // dil, dil, HEAD_DIM).swapaxes(2, 3)

        o, lse = banded_attention(by_residue(q)[..., None, :, :], by_residue(k), by_residue(v),
                                  radius, A_BLOCK)
        outs.append(o[..., 0, :, :].swapaxes(2, 3).reshape(b, A_HEADS, s_len, HEAD_DIM))
        lses.append(lse[..., 0, :].swapaxes(2, 3).reshape(b, A_HEADS, s_len))
    w = jax.nn.softmax(jnp.stack(lses), axis=0)
    o = jnp.einsum('pbhs,pbhsd->bhsd', w, jnp.stack(outs))
    return merge_heads(o).astype(u.dtype) @ w_out


def differential_attention_mixer(u, w_in, w_out, lam_q1, lam_k1, lam_q2, lam_k2, sub_g,
                                 cos, sin, layer_idx):
    b, s_len, _ = u.shape
    q, k, v = jnp.split(u @ w_in, 3, axis=-1)

    def qk_heads(t):
        return t.reshape(b, s_len, B_HEADS, 2, HEAD_DIM).transpose(0, 2, 3, 1, 4)

    q = apply_partial_rope(qk_heads(q), cos, sin)
    k = apply_partial_rope(qk_heads(k), cos, sin)
    v = split_heads(v, B_HEADS)
    lam_init = 0.8 - 0.6 * math.exp(-0.3 * layer_idx)
    f32 = jnp.float32
    lam = (jnp.exp(jnp.sum(lam_q1.astype(f32) * lam_k1.astype(f32)))
           - jnp.exp(jnp.sum(lam_q2.astype(f32) * lam_k2.astype(f32))) + lam_init)
    nqb = s_len // B_QBLOCK
    qb = q.reshape(b, B_HEADS, 2, nqb, B_QBLOCK, HEAD_DIM).transpose(3, 0, 1, 2, 4, 5)
    scale = HEAD_DIM ** -0.5

    def one_block(qblk):
        sc = jnp.einsum('bhcqd,bhckd->bhcqk', qblk, k, preferred_element_type=f32) * scale
        p = jax.nn.softmax(sc, axis=-1)
        a = p[:, :, 0] - lam * p[:, :, 1]
        return jnp.einsum('bhqk,bhkd->bhqd', a.astype(v.dtype), v, preferred_element_type=f32)

    o = lax.map(one_block, qb)
    o = o.transpose(1, 2, 0, 3, 4).reshape(b, B_HEADS, s_len, 2 * HEAD_DIM)
    o = rms_norm(o, sub_g) * (1.0 - lam_init)
    return merge_heads(o).astype(u.dtype) @ w_out


def windowed_gqa_sink_mixer(u, w_in, w_out, sink, cos, sin):
    b, s_len, _ = u.shape
    qkv = u @ w_in
    q = qkv[..., :C_Q_WIDTH]
    k = qkv[..., C_Q_WIDTH:C_Q_WIDTH + C_KV_WIDTH]
    v = qkv[..., C_Q_WIDTH + C_KV_WIDTH:]
    q = apply_partial_rope(split_heads(q, C_HEADS), cos, sin)
    k = apply_partial_rope(split_heads(k, C_KV_HEADS), cos, sin)
    v = split_heads(v, C_KV_HEADS)
    q = q.reshape(b, C_KV_HEADS, C_GROUP, s_len, HEAD_DIM)
    o, _ = banded_attention(q, k, v, C_RADIUS, C_BLOCK,
                            sink=sink.reshape(1, C_KV_HEADS, C_GROUP, 1, 1, 1))
    o = o.reshape(b, C_HEADS, s_len, HEAD_DIM)
    return merge_heads(o).astype(u.dtype) @ w_out


def memory_cross_attention(u, mem_n, wq, wkv, wo):
    q = split_heads(u @ wq, X_HEADS)
    k, v = jnp.split(mem_n @ wkv, 2, axis=-1)
    k = split_heads(k, X_HEADS)
    v = split_heads(v, X_HEADS)
    s = jnp.einsum('bhqd,bhkd->bhqk', q, k,
                   preferred_element_type=jnp.float32) * (X_HEAD_DIM ** -0.5)
    p = jax.nn.softmax(s, axis=-1)
    o = jnp.einsum('bhqk,bhkd->bhqd', p.astype(v.dtype), v)
    return merge_heads(o) @ wo


def swiglu_ffn(u, w_gate_up, w_down):
    g, up = jnp.split(u @ w_gate_up, 2, axis=-1)
    return (jax.nn.silu(g) * up) @ w_down


def setup_inputs(seed: int = 0) -> dict:
    key = jax.random.key(seed)
    ks = iter(jax.random.split(key, 32))

    def nrm(shape, scale):
        return jax.random.normal(next(ks), shape, jnp.float32) * scale

    def dense(shape):
        return nrm(shape, shape[-2] ** -0.5)

    def gain(shape):
        return 1.0 + nrm(shape, 0.05)

    x = nrm((BATCH, SEQ, D_MODEL), 1.0)
    mem = nrm((BATCH, N_MEM, D_MODEL), 1.0)
    offset = jax.random.randint(next(ks), (BATCH, 1), 0, 4096, dtype=jnp.int32)
    positions = jnp.arange(SEQ, dtype=jnp.int32)[None, :] + offset
    return {
        "x": x, "mem": mem, "positions": positions,
        "mix_pre_g": gain((DEPTH, D_MODEL)), "mix_post_g": gain((DEPTH, D_MODEL)),
        "mem_pre_g": gain((DEPTH, D_MODEL)), "mem_kv_g": gain((DEPTH, D_MODEL)),
        "mem_post_g": gain((DEPTH, D_MODEL)),
        "ffn_pre_g": gain((DEPTH, D_MODEL)), "ffn_post_g": gain((DEPTH, D_MODEL)),
        "a_w_in": dense((N_A, D_MODEL, 3 * A_WIDTH)), "a_w_out": dense((N_A, A_WIDTH, D_MODEL)),
        "b_w_in": dense((N_B, D_MODEL, 3 * B_WIDTH)), "b_w_out": dense((N_B, B_WIDTH, D_MODEL)),
        "b_lam_q1": nrm((N_B, HEAD_DIM), 0.1), "b_lam_k1": nrm((N_B, HEAD_DIM), 0.1),
        "b_lam_q2": nrm((N_B, HEAD_DIM), 0.1), "b_lam_k2": nrm((N_B, HEAD_DIM), 0.1),
        "b_sub_g": gain((N_B, 2 * HEAD_DIM)),
        "c_w_in": dense((N_C, D_MODEL, C_Q_WIDTH + 2 * C_KV_WIDTH)),
        "c_w_out": dense((N_C, C_Q_WIDTH, D_MODEL)),
        "c_sink": nrm((N_C, C_HEADS), 0.5),
        "x_wq": dense((DEPTH, D_MODEL, D_MODEL)), "x_wkv": dense((DEPTH, D_MODEL, 2 * D_MODEL)),
        "x_wo": dense((DEPTH, D_MODEL, D_MODEL)),
        "w_gate_up": dense((DEPTH, D_MODEL, 2 * D_FF)), "w_down": dense((DEPTH, D_FF, D_MODEL)),
    }


def reference(x, mem, positions, mix_pre_g, mix_post_g, mem_pre_g, mem_kv_g, mem_post_g,
              ffn_pre_g, ffn_post_g, a_w_in, a_w_out, b_w_in, b_w_out, b_lam_q1, b_lam_k1,
              b_lam_q2, b_lam_k2, b_sub_g, c_w_in, c_w_out, c_sink, x_wq, x_wkv, x_wo,
              w_gate_up, w_down):
    cos, sin = rope_tables(positions)
    h = x
    for i in range(DEPTH):
        kind, j = i % N_MIXERS, i // N_MIXERS
        u = rms_norm(h, mix_pre_g[i])
        if kind == 0:
            y = dilated_attention_mixer(u, a_w_in[j], a_w_out[j], cos, sin)
        elif kind == 1:
            y = differential_attention_mixer(u, b_w_in[j], b_w_out[j], b_lam_q1[j], b_lam_k1[j],
                                             b_lam_q2[j], b_lam_k2[j], b_sub_g[j], cos, sin, i)
        else:
            y = windowed_gqa_sink_mixer(u, c_w_in[j], c_w_out[j], c_sink[j], cos, sin)
        h = h + rms_norm(y, mix_post_g[i])
        u = rms_norm(h, mem_pre_g[i])
        y = memory_cross_attention(u, rms_norm(mem, mem_kv_g[i]), x_wq[i], x_wkv[i], x_wo[i])
        h = h + rms_norm(y, mem_post_g[i])
        u = rms_norm(h, ffn_pre_g[i])
        y = swiglu_ffn(u, w_gate_up[i], w_down[i])
        h = h + rms_norm(y, ffn_post_g[i])
    return h
```

```python
import functools
import math

import jax
import jax.numpy as jnp
import numpy as np
from jax import lax
from jax.experimental import pallas as pl
from jax.experimental.pallas import tpu as pltpu

F32 = jnp.float32
BF16 = jnp.bfloat16

HEAD_DIM = 64
ROT_HALF = HEAD_DIM // 8
ROPE_THETA = 500000.0
EPS = 1e-6
NEG_INF = -1e30
LOG2E = 1.4426950408889634
N_MIXERS = 3

A_PATTERNS = ((128, 1), (512, 4), (2048, 16))
C_RADIUS = 128
X_HEADS = 4

LANES = 128
ROW_TILE = 512
BAND_BLOCK = 256
DENSE_BLOCK = 512
FF_CHUNKS = 2
VMEM_LIMIT = 56 * 1024 * 1024

_NT = (((1,), (1,)), ((), ()))
_TN = (((0,), (0,)), ((), ()))


def _params(*sem):
    return pltpu.CompilerParams(dimension_semantics=sem, vmem_limit_bytes=VMEM_LIMIT)


def _rms(x, g):
    ms = jnp.mean(x * x, axis=-1, keepdims=True)
    return x * lax.rsqrt(ms + EPS) * g


def _in_proj_kernel(h_ref, g_ref, wqT_ref, wk_ref, wvT_ref, cosT_ref, sinT_ref, kc_ref, ka_ref, kb_ref,
                    qT_ref, k_ref, vT_ref, *, q_scale, qblk, vblk):
    tm = h_ref.shape[0]
    u = _rms(h_ref[...], g_ref[...]).astype(BF16)

    kf = jnp.dot(u, wk_ref[...], preferred_element_type=F32)
    kc, ka, kb = kc_ref[...], ka_ref[...], kb_ref[...]
    for j in range(kf.shape[1] // LANES):
        x = kf[:, j * LANES:(j + 1) * LANES]
        y = x * kc + pltpu.roll(x, LANES - ROT_HALF, 1) * ka + pltpu.roll(x, ROT_HALF, 1) * kb
        k_ref[:, j * LANES:(j + 1) * LANES] = y.astype(BF16)

    qf = lax.dot_general(wqT_ref[...], u, _NT, preferred_element_type=F32)
    c = cosT_ref[...] * q_scale
    s = sinT_ref[...] * q_scale
    for unit in range(qf.shape[0] // HEAD_DIM):
        r0 = unit * HEAD_DIM
        t1 = qf[r0:r0 + ROT_HALF]
        t2 = qf[r0 + ROT_HALF:r0 + 2 * ROT_HALF]
        rest = qf[r0 + 2 * ROT_HALF:r0 + HEAD_DIM] * q_scale
        blk = jnp.concatenate([t1 * c - t2 * s, t2 * c + t1 * s, rest], axis=0).astype(BF16)
        for jb in range(tm // qblk):
            qT_ref[jb, r0:r0 + HEAD_DIM, :] = blk[:, jb * qblk:(jb + 1) * qblk]

    vf = lax.dot_general(wvT_ref[...], u, _NT, preferred_element_type=F32).astype(BF16)
    for jb in range(tm // vblk):
        vT_ref[jb] = vf[:, jb * vblk:(jb + 1) * vblk]


def _in_proj(h, g, wqT, wk, wvT, cosT, sinT, kc, ka, kb, *, q_scale, qblk, vblk):
    b, s, d = h.shape
    nq, nk, nv = wqT.shape[0], wk.shape[1], wvT.shape[0]
    tm = ROW_TILE
    kern = functools.partial(_in_proj_kernel, q_scale=q_scale, qblk=qblk, vblk=vblk)
    const = lambda bi, i: (0, 0)
    return pl.pallas_call(
        kern,
        grid=(b, s // tm),
        in_specs=[
            pl.BlockSpec((None, tm, d), lambda bi, i: (bi, i, 0)),
            pl.BlockSpec((1, d), const),
            pl.BlockSpec((nq, d), const),
            pl.BlockSpec((d, nk), const),
            pl.BlockSpec((nv, d), const),
            pl.BlockSpec((None, ROT_HALF, tm), lambda bi, i: (bi, 0, i)),
            pl.BlockSpec((None, ROT_HALF, tm), lambda bi, i: (bi, 0, i)),
            pl.BlockSpec((None, tm, LANES), lambda bi, i: (bi, i, 0)),
            pl.BlockSpec((None, tm, LANES), lambda bi, i: (bi, i, 0)),
            pl.BlockSpec((None, tm, LANES), lambda bi, i: (bi, i, 0)),
        ],
        out_specs=[
            pl.BlockSpec((None, tm // qblk, nq, qblk), lambda bi, i: (bi, i, 0, 0)),
            pl.BlockSpec((None, tm, nk), lambda bi, i: (bi, i, 0)),
            pl.BlockSpec((None, tm // vblk, nv, vblk), lambda bi, i: (bi, i, 0, 0)),
        ],
        out_shape=[
            jax.ShapeDtypeStruct((b, s // qblk, nq, qblk), BF16),
            jax.ShapeDtypeStruct((b, s, nk), BF16),
            jax.ShapeDtypeStruct((b, s // vblk, nv, vblk), BF16),
        ],
        compiler_params=_params("parallel", "parallel"),
        name="mixer_in_proj",
    )(h, g, wqT, wk, wvT, cosT, sinT, kc, ka, kb)


def _stage_queries(qT_ref, qz_ref, t):
    q = qT_ref[...]
    row = lax.broadcasted_iota(jnp.int32, q.shape, 0)
    zero = jnp.zeros_like(q)
    qz_ref[:, 0:t] = jnp.where(row < HEAD_DIM, q, zero)
    qz_ref[:, t:2 * t] = jnp.where(row >= HEAD_DIM, q, zero)


def _online_softmax_step(s, m_ref, l_ref):
    m_old = m_ref[...]
    m_new = jnp.maximum(m_old, jnp.max(s, axis=0, keepdims=True))
    alpha = jnp.exp2(m_old - m_new)
    p = jnp.exp2(s - m_new)
    l_ref[...] = alpha * l_ref[...] + jnp.sum(p, axis=0, keepdims=True)
    m_ref[...] = m_new
    return alpha, p.astype(BF16)


def _band_attn_kernel(*refs, t, width, nkb, has_sink):
    if has_sink:
        sink_ref, bias_ref, qT_ref, k_ref, vT_ref, oT_ref, qz_ref, m_ref, l_ref, acc_ref = refs
    else:
        bias_ref, qT_ref, k_ref, vT_ref, oT_ref, qz_ref, m_ref, l_ref, acc_ref = refs
    pair = pl.program_id(1)
    qb = pl.program_id(2)
    _stage_queries(qT_ref, qz_ref, t)
    m_ref[...] = jnp.full(m_ref.shape, NEG_INF, F32)
    l_ref[...] = jnp.zeros(l_ref.shape, F32)
    acc_ref[...] = jnp.zeros(acc_ref.shape, F32)

    def body(kb, carry):
        kblk = k_ref[pl.ds(pl.multiple_of(kb * t, t), t), :]
        s = jnp.dot(kblk, qz_ref[...], preferred_element_type=F32)
        bias = bias_ref[kb - qb + width]
        s = s + jnp.concatenate([bias, bias], axis=1)
        alpha, p = _online_softmax_step(s, m_ref, l_ref)
        v = vT_ref[kb]
        for u in range(2):
            pv = jnp.dot(v[u * HEAD_DIM:(u + 1) * HEAD_DIM], p[:, u * t:(u + 1) * t],
                         preferred_element_type=F32)
            acc_ref[u] = alpha[:, u * t:(u + 1) * t] * acc_ref[u] + pv
        return carry

    lax.fori_loop(jnp.maximum(qb - width, 0), jnp.minimum(qb + width, nkb - 1) + 1, body, 0)

    for u in range(2):
        m = m_ref[:, u * t:(u + 1) * t]
        l = l_ref[:, u * t:(u + 1) * t]
        acc = acc_ref[u]
        if has_sink:
            sk = sink_ref[pair * 2 + u]
            m2 = jnp.maximum(m, sk)
            w = jnp.exp2(m - m2)
            l = l * w + jnp.exp2(sk - m2)
            acc = acc * w
        oT_ref[u * HEAD_DIM:(u + 1) * HEAD_DIM, :] = (acc / l).astype(BF16)


def _band_attention(qT, k, vT, bias, sink, *, width, shared_kv):
    b, nqb, nq, t = qT.shape
    s = k.shape[1]
    nkb = s // t
    npairs = nq // (2 * HEAD_DIM)
    kv_idx = (lambda p: 0) if shared_kv else (lambda p: p)
    has_sink = sink is not None
    kern = functools.partial(_band_attn_kernel, t=t, width=width, nkb=nkb, has_sink=has_sink)
    in_specs = [
        pl.BlockSpec(bias.shape, lambda bi, p, i: (0, 0, 0)),
        pl.BlockSpec((None, None, 2 * HEAD_DIM, t), lambda bi, p, i: (bi, i, p, 0)),
        pl.BlockSpec((None, s, 2 * HEAD_DIM), lambda bi, p, i: (bi, 0, kv_idx(p))),
        pl.BlockSpec((None, nkb, 2 * HEAD_DIM, t), lambda bi, p, i: (bi, 0, kv_idx(p), 0)),
    ]
    args = [bias, qT, k, vT]
    if has_sink:
        in_specs = [pl.BlockSpec(memory_space=pltpu.SMEM)] + in_specs
        args = [sink] + args
    return pl.pallas_call(
        kern,
        grid=(b, npairs, nqb),
        in_specs=in_specs,
        out_specs=pl.BlockSpec((None, None, 2 * HEAD_DIM, t), lambda bi, p, i: (bi, i, p, 0)),
        out_shape=jax.ShapeDtypeStruct((b, nqb, nq, t), BF16),
        scratch_shapes=[
            pltpu.VMEM((2 * HEAD_DIM, 2 * t), BF16),
            pltpu.VMEM((1, 2 * t), F32),
            pltpu.VMEM((1, 2 * t), F32),
            pltpu.VMEM((2, HEAD_DIM, t), F32),
        ],
        compiler_params=_params("parallel", "parallel", "arbitrary"),
        name="band_attention",
    )(*args)


def _diff_attn_kernel(lam_ref, subg_ref, qT_ref, k_ref, vT_ref, oT_ref, qz_ref, m_ref, l_ref, acc_ref,
                      *, t, nkb, lam_init):
    _stage_queries(qT_ref, qz_ref, t)
    m_ref[...] = jnp.full(m_ref.shape, NEG_INF, F32)
    l_ref[...] = jnp.zeros(l_ref.shape, F32)
    acc_ref[...] = jnp.zeros(acc_ref.shape, F32)

    def body(kb, carry):
        kblk = k_ref[pl.ds(pl.multiple_of(kb * t, t), t), :]
        s = jnp.dot(kblk, qz_ref[...], preferred_element_type=F32)
        alpha, p = _online_softmax_step(s, m_ref, l_ref)
        v = vT_ref[kb]
        for u in range(2):
            pv = jnp.dot(v, p[:, u * t:(u + 1) * t], preferred_element_type=F32)
            acc_ref[u] = alpha[:, u * t:(u + 1) * t] * acc_ref[u] + pv
        return carry

    lax.fori_loop(0, nkb, body, 0)

    lv = lam_ref[...]
    e1 = jnp.exp(jnp.sum(lv[0:1] * lv[1:2], axis=-1, keepdims=True))
    e2 = jnp.exp(jnp.sum(lv[2:3] * lv[3:4], axis=-1, keepdims=True))
    lam = e1 - e2 + lam_init
    o = acc_ref[0] / l_ref[:, 0:t] - lam * (acc_ref[1] / l_ref[:, t:2 * t])
    ms = jnp.mean(o * o, axis=0, keepdims=True)
    o = o * lax.rsqrt(ms + EPS) * subg_ref[...] * (1.0 - lam_init)
    oT_ref[...] = o.astype(BF16)


def _diff_attention(qT, k, vT, lamv, subg, *, lam_init):
    b, nqb, nq, t = qT.shape
    s = k.shape[1]
    nkb = s // t
    heads = nq // (2 * HEAD_DIM)
    kern = functools.partial(_diff_attn_kernel, t=t, nkb=nkb, lam_init=lam_init)
    return pl.pallas_call(
        kern,
        grid=(b, heads, nqb),
        in_specs=[
            pl.BlockSpec(lamv.shape, lambda bi, h, i: (0, 0)),
            pl.BlockSpec(subg.shape, lambda bi, h, i: (0, 0)),
            pl.BlockSpec((None, None, 2 * HEAD_DIM, t), lambda bi, h, i: (bi, i, h, 0)),
            pl.BlockSpec((None, s, 2 * HEAD_DIM), lambda bi, h, i: (bi, 0, h)),
            pl.BlockSpec((None, nkb, 2 * HEAD_DIM, t), lambda bi, h, i: (bi, 0, h, 0)),
        ],
        out_specs=pl.BlockSpec((None, None, 2 * HEAD_DIM, t), lambda bi, h, i: (bi, i, h, 0)),
        out_shape=jax.ShapeDtypeStruct((b, nqb, nq, t), BF16),
        scratch_shapes=[
            pltpu.VMEM((2 * HEAD_DIM, 2 * t), BF16),
            pltpu.VMEM((1, 2 * t), F32),
            pltpu.VMEM((1, 2 * t), F32),
            pltpu.VMEM((2, 2 * HEAD_DIM, t), F32),
        ],
        compiler_params=_params("parallel", "parallel", "arbitrary"),
        name="diff_attention",
    )(lamv, subg, qT, k, vT)


def _mid_kernel(h_ref, oT_ref, wout_ref, gmix_ref, gpre_ref, wq_ref, kT_ref, v_ref, wo_ref, gpost_ref,
                out_ref, *, x_scale):
    ys = [lax.dot_general(oT_ref[j], wout_ref[...], _TN, preferred_element_type=F32)
          for j in range(oT_ref.shape[0])]
    y = jnp.concatenate(ys, axis=0) if len(ys) > 1 else ys[0]
    h1 = h_ref[...] + _rms(y, gmix_ref[...])

    u = _rms(h1, gpre_ref[...]).astype(BF16)
    q = (jnp.dot(u, wq_ref[...], preferred_element_type=F32) * x_scale).astype(BF16)
    xd = q.shape[1] // X_HEADS
    outs = []
    for hd in range(X_HEADS):
        s = jnp.dot(q[:, hd * xd:(hd + 1) * xd], kT_ref[hd * xd:(hd + 1) * xd, :],
                    preferred_element_type=F32)
        p = jnp.exp2(s - jnp.max(s, axis=-1, keepdims=True))
        l = jnp.sum(p, axis=-1, keepdims=True)
        o = jnp.dot(p.astype(BF16), v_ref[:, hd * xd:(hd + 1) * xd], preferred_element_type=F32)
        outs.append((o / l).astype(BF16))
    y2 = jnp.dot(jnp.concatenate(outs, axis=1), wo_ref[...], preferred_element_type=F32)
    out_ref[...] = h1 + _rms(y2, gpost_ref[...])


def _mid(h, oT, wout, gmix, gpre, wq, kT, v, wo, gpost, *, x_scale):
    b, s, d = h.shape
    tm = ROW_TILE
    oblk = oT.shape[3]
    n_mem = v.shape[1]
    const = lambda bi, i: (0, 0)
    kern = functools.partial(_mid_kernel, x_scale=x_scale)
    return pl.pallas_call(
        kern,
        grid=(b, s // tm),
        in_specs=[
            pl.BlockSpec((None, tm, d), lambda bi, i: (bi, i, 0)),
            pl.BlockSpec((None, tm // oblk, d, oblk), lambda bi, i: (bi, i, 0, 0)),
            pl.BlockSpec((d, d), const),
            pl.BlockSpec((1, d), const),
            pl.BlockSpec((1, d), const),
            pl.BlockSpec((d, d), const),
            pl.BlockSpec((None, d, n_mem), lambda bi, i: (bi, 0, 0)),
            pl.BlockSpec((None, n_mem, d), lambda bi, i: (bi, 0, 0)),
            pl.BlockSpec((d, d), const),
            pl.BlockSpec((1, d), const),
        ],
        out_specs=pl.BlockSpec((None, tm, d), lambda bi, i: (bi, i, 0)),
        out_shape=jax.ShapeDtypeStruct((b, s, d), F32),
        compiler_params=_params("parallel", "parallel"),
        name="out_proj_cross_attention",
    )(h, oT, wout, gmix, gpre, wq, kT, v, wo, gpost)


def _mem_kv_kernel(mem_ref, g_ref, wkT_ref, wv_ref, kT_ref, v_ref):
    mn = _rms(mem_ref[...], g_ref[...]).astype(BF16)
    kT_ref[...] = lax.dot_general(wkT_ref[...], mn, _NT, preferred_element_type=F32).astype(BF16)
    v_ref[...] = jnp.dot(mn, wv_ref[...], preferred_element_type=F32).astype(BF16)


def _mem_kv(mem, g, wkT, wv):
    depth, d = g.shape[0], g.shape[2]
    b, n_mem, _ = mem.shape
    return pl.pallas_call(
        _mem_kv_kernel,
        grid=(depth, b),
        in_specs=[
            pl.BlockSpec((None, n_mem, d), lambda li, bi: (bi, 0, 0)),
            pl.BlockSpec((None, 1, d), lambda li, bi: (li, 0, 0)),
            pl.BlockSpec((None, d, d), lambda li, bi: (li, 0, 0)),
            pl.BlockSpec((None, d, d), lambda li, bi: (li, 0, 0)),
        ],
        out_specs=[
            pl.BlockSpec((None, None, d, n_mem), lambda li, bi: (li, bi, 0, 0)),
            pl.BlockSpec((None, None, n_mem, d), lambda li, bi: (li, bi, 0, 0)),
        ],
        out_shape=[
            jax.ShapeDtypeStruct((depth, b, d, n_mem), BF16),
            jax.ShapeDtypeStruct((depth, b, n_mem, d), BF16),
        ],
        compiler_params=_params("parallel", "parallel"),
        name="memory_kv",
    )(mem, g, wkT, wv)


def _ffn_kernel(h_ref, gpre_ref, wg_ref, wu_ref, wd_ref, gpost_ref, out_ref, u_ref, acc_ref):
    j = pl.program_id(1)

    @pl.when(j == 0)
    def _():
        u_ref[...] = _rms(h_ref[...], gpre_ref[...]).astype(BF16)
        acc_ref[...] = jnp.zeros(acc_ref.shape, F32)

    u = u_ref[...]
    g = jnp.dot(u, wg_ref[...], preferred_element_type=F32)
    up = jnp.dot(u, wu_ref[...], preferred_element_type=F32)
    a = (g / (1.0 + jnp.exp(-g)) * up).astype(BF16)
    acc_ref[...] += jnp.dot(a, wd_ref[...], preferred_element_type=F32)

    @pl.when(j == pl.num_programs(1) - 1)
    def _():
        out_ref[...] = h_ref[...] + _rms(acc_ref[...], gpost_ref[...])


def _ffn(h, gpre, wgu, wd, gpost):
    b, s, d = h.shape
    tm = ROW_TILE
    dff = wd.shape[0]
    fc = dff // FF_CHUNKS
    rows = b * s
    h2 = h.reshape(rows, d)
    out = pl.pallas_call(
        _ffn_kernel,
        grid=(rows // tm, FF_CHUNKS),
        in_specs=[
            pl.BlockSpec((tm, d), lambda i, j: (i, 0)),
            pl.BlockSpec((1, d), lambda i, j: (0, 0)),
            pl.BlockSpec((d, fc), lambda i, j: (0, j)),
            pl.BlockSpec((d, fc), lambda i, j: (0, FF_CHUNKS + j)),
            pl.BlockSpec((fc, d), lambda i, j: (j, 0)),
            pl.BlockSpec((1, d), lambda i, j: (0, 0)),
        ],
        out_specs=pl.BlockSpec((tm, d), lambda i, j: (i, 0)),
        out_shape=jax.ShapeDtypeStruct((rows, d), F32),
        scratch_shapes=[pltpu.VMEM((tm, d), BF16), pltpu.VMEM((tm, d), F32)],
        compiler_params=_params("parallel", "arbitrary"),
        name="swiglu_ffn",
    )(h2, gpre, wgu, wgu, wd, gpost)
    return out.reshape(b, s, d)


def _rope_tables(positions):
    inv_freq = ROPE_THETA ** (-jnp.arange(0, 2 * ROT_HALF, 2, dtype=F32) / (2 * ROT_HALF))
    ang = positions.astype(F32)[..., None] * inv_freq
    cos, sin = jnp.cos(ang), jnp.sin(ang)
    cosT, sinT = cos.transpose(0, 2, 1), sin.transpose(0, 2, 1)
    zeros = jnp.zeros_like(cos)
    pad = HEAD_DIM - 2 * ROT_HALF
    ones_tail = jnp.ones(cos.shape[:-1] + (pad,), F32)
    zero_tail = jnp.zeros(cos.shape[:-1] + (pad,), F32)
    reps = LANES // HEAD_DIM
    kc = jnp.tile(jnp.concatenate([cos, cos, ones_tail], axis=-1), reps)
    ka = jnp.tile(jnp.concatenate([-sin, zeros, zero_tail], axis=-1), reps)
    kb = jnp.tile(jnp.concatenate([zeros, sin, zero_tail], axis=-1), reps)
    return cosT, sinT, kc, ka, kb


def _band_bias(t, width, multiplicity):
    i = np.arange(t)[:, None]
    j = np.arange(t)[None, :]
    tiles = []
    for d in range(-width, width + 1):
        c = multiplicity(d * t + i - j)
        tiles.append(np.where(c > 0, np.log2(np.maximum(c, 1)), NEG_INF))
    return jnp.asarray(np.stack(tiles), F32)


def _dilated_multiplicity(delta):
    c = np.zeros(delta.shape, np.int64)
    for window, dil in A_PATTERNS:
        c += (delta % dil == 0) & (np.abs(delta) <= (window // (2 * dil)) * dil)
    return c


def _window_multiplicity(delta):
    return (np.abs(delta) <= C_RADIUS).astype(np.int64)


def _row(g):
    return g.reshape(1, -1)


def kernel(x, mem, positions, mix_pre_g, mix_post_g, mem_pre_g, mem_kv_g, mem_post_g, ffn_pre_g, ffn_post_g,
           a_w_in, a_w_out, b_w_in, b_w_out, b_lam_q1, b_lam_k1, b_lam_q2, b_lam_k2, b_sub_g, c_w_in, c_w_out,
           c_sink, x_wq, x_wkv, x_wo, w_gate_up, w_down):
    depth, d = mix_pre_g.shape
    assert d % (2 * HEAD_DIM) == 0 and x.shape[1] % DENSE_BLOCK == 0 and x.shape[1] % ROW_TILE == 0
    cosT, sinT, kc, ka, kb = _rope_tables(positions)
    q_scale = HEAD_DIM ** -0.5 * LOG2E
    x_scale = (d // X_HEADS) ** -0.5 * LOG2E

    mem_kT, mem_v = _mem_kv(mem, mem_kv_g.reshape(depth, 1, d),
                            x_wkv[:, :, :d].transpose(0, 2, 1).astype(BF16), x_wkv[:, :, d:].astype(BF16))

    a_width = max((w // (2 * dl)) * dl for w, dl in A_PATTERNS) // BAND_BLOCK
    a_bias = _band_bias(BAND_BLOCK, a_width, _dilated_multiplicity)
    c_width = -(-C_RADIUS // BAND_BLOCK)
    c_bias = _band_bias(BAND_BLOCK, c_width, _window_multiplicity)

    h = x
    for i in range(depth):
        kind, j = i % N_MIXERS, i // N_MIXERS
        if kind == 0:
            w_in, w_out = a_w_in[j], a_w_out[j]
            wq, wk, wv = w_in[:, :d], w_in[:, d:2 * d], w_in[:, 2 * d:]
            blk = BAND_BLOCK
        elif kind == 1:
            w_in, w_out = b_w_in[j], b_w_out[j]
            wq, wk, wv = w_in[:, :d], w_in[:, d:2 * d], w_in[:, 2 * d:]
            blk = DENSE_BLOCK
        else:
            w_in, w_out = c_w_in[j], c_w_out[j]
            kvw = w_in.shape[1] - d
            wq, wk, wv = w_in[:, :d], w_in[:, d:d + kvw // 2], w_in[:, d + kvw // 2:]
            n_kv = (kvw // 2) // HEAD_DIM
            grp = (d // HEAD_DIM) // n_kv
            perm = np.arange(d).reshape(n_kv, grp, HEAD_DIM).transpose(1, 0, 2).reshape(-1)
            wq, w_out = wq[:, perm], w_out[perm, :]
            blk = BAND_BLOCK
        qT, k, vT = _in_proj(h, _row(mix_pre_g[i]), wq.T.astype(BF16), wk.astype(BF16), wv.T.astype(BF16),
                             cosT, sinT, kc, ka, kb, q_scale=q_scale, qblk=blk, vblk=blk)
        if kind == 0:
            oT = _band_attention(qT, k, vT, a_bias, None, width=a_width, shared_kv=False)
        elif kind == 1:
            lam_init = 0.8 - 0.6 * math.exp(-0.3 * i)
            lamv = jnp.stack([b_lam_q1[j], b_lam_k1[j], b_lam_q2[j], b_lam_k2[j]]).astype(F32)
            subg = jnp.broadcast_to(b_sub_g[j].astype(F32)[:, None], (2 * HEAD_DIM, DENSE_BLOCK))
            oT = _diff_attention(qT, k, vT, lamv, subg, lam_init=lam_init)
        else:
            sink = (c_sink[j].astype(F32) * LOG2E)[perm[::HEAD_DIM] // HEAD_DIM]
            oT = _band_attention(qT, k, vT, c_bias, sink, width=c_width, shared_kv=True)
        h = _mid(h, oT, w_out.astype(BF16), _row(mix_post_g[i]), _row(mem_pre_g[i]), x_wq[i].astype(BF16),
                 mem_kT[i], mem_v[i], x_wo[i].astype(BF16), _row(mem_post_g[i]), x_scale=x_scale)
        h = _ffn(h, _row(ffn_pre_g[i]), w_gate_up[i].astype(BF16), w_down[i].astype(BF16), _row(ffn_post_g[i]))
    return h
```

```python
import functools
import math

import jax
import jax.numpy as jnp
import numpy as np
from jax import lax
from jax.experimental import pallas as pl
from jax.experimental.pallas import tpu as pltpu

F32 = jnp.float32
BF16 = jnp.bfloat16

HEAD_DIM = 64
ROT_HALF = HEAD_DIM // 8
ROPE_THETA = 500000.0
EPS = 1e-6
NEG_INF = -1e30
LOG2E = 1.4426950408889634
N_MIXERS = 3

A_PATTERNS = ((128, 1), (512, 4), (2048, 16))
C_RADIUS = 128
X_HEADS = 4

LANES = 128
ROW_TILE = 512
BAND_BLOCK = 256
DENSE_BLOCK = 512
DENSE_CHUNK = 256
FF_CHUNKS = 2
VMEM_LIMIT = 56 * 1024 * 1024

_NT = (((1,), (1,)), ((), ()))
_TN = (((0,), (0,)), ((), ()))


def _params(*sem):
    return pltpu.CompilerParams(dimension_semantics=sem, vmem_limit_bytes=VMEM_LIMIT)


def _rms(x, g):
    ms = jnp.mean(x * x, axis=-1, keepdims=True)
    return x * lax.rsqrt(ms + EPS) * g


def _in_proj_kernel(h_ref, g_ref, wqT_ref, wk_ref, wvT_ref, cosT_ref, sinT_ref, kc_ref, ka_ref, kb_ref,
                    qT_ref, k_ref, vT_ref, *, q_scale, qblk, vblk):
    tm = h_ref.shape[0]
    u = _rms(h_ref[...], g_ref[...]).astype(BF16)

    kf = jnp.dot(u, wk_ref[...], preferred_element_type=F32)
    kc, ka, kb = kc_ref[...], ka_ref[...], kb_ref[...]
    for j in range(kf.shape[1] // LANES):
        x = kf[:, j * LANES:(j + 1) * LANES]
        y = x * kc + pltpu.roll(x, LANES - ROT_HALF, 1) * ka + pltpu.roll(x, ROT_HALF, 1) * kb
        k_ref[:, j * LANES:(j + 1) * LANES] = y.astype(BF16)

    qf = lax.dot_general(wqT_ref[...], u, _NT, preferred_element_type=F32)
    c = cosT_ref[...] * q_scale
    s = sinT_ref[...] * q_scale
    for unit in range(qf.shape[0] // HEAD_DIM):
        r0 = unit * HEAD_DIM
        t1 = qf[r0:r0 + ROT_HALF]
        t2 = qf[r0 + ROT_HALF:r0 + 2 * ROT_HALF]
        rest = qf[r0 + 2 * ROT_HALF:r0 + HEAD_DIM] * q_scale
        blk = jnp.concatenate([t1 * c - t2 * s, t2 * c + t1 * s, rest], axis=0).astype(BF16)
        for jb in range(tm // qblk):
            qT_ref[jb, r0:r0 + HEAD_DIM, :] = blk[:, jb * qblk:(jb + 1) * qblk]

    vf = lax.dot_general(wvT_ref[...], u, _NT, preferred_element_type=F32).astype(BF16)
    for jb in range(tm // vblk):
        vT_ref[jb] = vf[:, jb * vblk:(jb + 1) * vblk]


def _in_proj(h, g, wqT, wk, wvT, cosT, sinT, kc, ka, kb, *, q_scale, qblk, vblk):
    b, s, d = h.shape
    nq, nk, nv = wqT.shape[0], wk.shape[1], wvT.shape[0]
    tm = ROW_TILE
    kern = functools.partial(_in_proj_kernel, q_scale=q_scale, qblk=qblk, vblk=vblk)
    const = lambda bi, i: (0, 0)
    return pl.pallas_call(
        kern,
        grid=(b, s // tm),
        in_specs=[
            pl.BlockSpec((None, tm, d), lambda bi, i: (bi, i, 0)),
            pl.BlockSpec((1, d), const),
            pl.BlockSpec((nq, d), const),
            pl.BlockSpec((d, nk), const),
            pl.BlockSpec((nv, d), const),
            pl.BlockSpec((None, ROT_HALF, tm), lambda bi, i: (bi, 0, i)),
            pl.BlockSpec((None, ROT_HALF, tm), lambda bi, i: (bi, 0, i)),
            pl.BlockSpec((None, tm, LANES), lambda bi, i: (bi, i, 0)),
            pl.BlockSpec((None, tm, LANES), lambda bi, i: (bi, i, 0)),
            pl.BlockSpec((None, tm, LANES), lambda bi, i: (bi, i, 0)),
        ],
        out_specs=[
            pl.BlockSpec((None, tm // qblk, nq, qblk), lambda bi, i: (bi, i, 0, 0)),
            pl.BlockSpec((None, tm, nk), lambda bi, i: (bi, i, 0)),
            pl.BlockSpec((None, tm // vblk, nv, vblk), lambda bi, i: (bi, i, 0, 0)),
        ],
        out_shape=[
            jax.ShapeDtypeStruct((b, s // qblk, nq, qblk), BF16),
            jax.ShapeDtypeStruct((b, s, nk), BF16),
            jax.ShapeDtypeStruct((b, s // vblk, nv, vblk), BF16),
        ],
        compiler_params=_params("parallel", "parallel"),
        name="mixer_in_proj",
    )(h, g, wqT, wk, wvT, cosT, sinT, kc, ka, kb)


def _stage_queries(qT_ref, qz_ref, t):
    q = qT_ref[...]
    row = lax.broadcasted_iota(jnp.int32, q.shape, 0)
    zero = jnp.zeros_like(q)
    qz_ref[:, 0:t] = jnp.where(row < HEAD_DIM, q, zero)
    qz_ref[:, t:2 * t] = jnp.where(row >= HEAD_DIM, q, zero)


def _init_stats(m_ref, l_ref, acc_ref):
    m_ref[...] = jnp.full(m_ref.shape, NEG_INF, F32)
    l_ref[...] = jnp.zeros(l_ref.shape, F32)
    acc_ref[...] = jnp.zeros(acc_ref.shape, F32)


def _scores(k_ref, kb, qz_ref, dst_ref, t):
    kblk = k_ref[pl.ds(pl.multiple_of(kb * t, t), t), :]
    dst_ref[...] = jnp.dot(kblk, qz_ref[...], preferred_element_type=F32)


def _softmax_pv(s, v, cols, acc_at, m_ref, l_ref):
    m_old = m_ref[:, cols]
    m_new = jnp.maximum(m_old, jnp.max(s, axis=0, keepdims=True))
    alpha = jnp.exp2(m_old - m_new)
    p = jnp.exp2(s - m_new)
    l_ref[:, cols] = alpha * l_ref[:, cols] + jnp.sum(p, axis=0, keepdims=True)
    m_ref[:, cols] = m_new
    acc_at[...] = alpha * acc_at[...] + jnp.dot(v, p.astype(BF16), preferred_element_type=F32)


def _band_attn_kernel(*refs, t, width, nkb, has_sink):
    if has_sink:
        sink_ref, bias_ref, qT_ref, k_ref, vT_ref, oT_ref, qz_ref, m_ref, l_ref, acc_ref, sa_ref, sb_ref = refs
    else:
        bias_ref, qT_ref, k_ref, vT_ref, oT_ref, qz_ref, m_ref, l_ref, acc_ref, sa_ref, sb_ref = refs
    pair = pl.program_id(1)
    qb = pl.program_id(2)
    _stage_queries(qT_ref, qz_ref, t)
    _init_stats(m_ref, l_ref, acc_ref)
    nblk = 2 * width + 1
    bufs = (sa_ref, sb_ref)

    def block(d):
        kb = qb + d
        valid = jnp.logical_and(kb >= 0, kb < nkb)
        return jnp.clip(kb, 0, nkb - 1), jnp.where(valid, d + width, nblk)

    _scores(k_ref, block(-width)[0], qz_ref, bufs[0], t)
    for n, d in enumerate(range(-width, width + 1)):
        if d < width:
            _scores(k_ref, block(d + 1)[0], qz_ref, bufs[(n + 1) % 2], t)
        kb, bidx = block(d)
        bias = bias_ref[bidx]
        v = vT_ref[kb]
        for u in range(2):
            cols = slice(u * t, (u + 1) * t)
            _softmax_pv(bufs[n % 2][:, cols] + bias, v[u * HEAD_DIM:(u + 1) * HEAD_DIM], cols,
                        acc_ref.at[u], m_ref, l_ref)

    for u in range(2):
        m = m_ref[:, u * t:(u + 1) * t]
        l = l_ref[:, u * t:(u + 1) * t]
        acc = acc_ref[u]
        if has_sink:
            sk = sink_ref[pair * 2 + u]
            m2 = jnp.maximum(m, sk)
            w = jnp.exp2(m - m2)
            l = l * w + jnp.exp2(sk - m2)
            acc = acc * w
        oT_ref[u * HEAD_DIM:(u + 1) * HEAD_DIM, :] = (acc / l).astype(BF16)


def _band_attention(qT, k, vT, bias, sink, *, width, shared_kv):
    b, nqb, nq, t = qT.shape
    s = k.shape[1]
    nkb = s // t
    npairs = nq // (2 * HEAD_DIM)
    kv_idx = (lambda p: 0) if shared_kv else (lambda p: p)
    has_sink = sink is not None
    kern = functools.partial(_band_attn_kernel, t=t, width=width, nkb=nkb, has_sink=has_sink)
    in_specs = [
        pl.BlockSpec(bias.shape, lambda bi, p, i: (0, 0, 0)),
        pl.BlockSpec((None, None, 2 * HEAD_DIM, t), lambda bi, p, i: (bi, i, p, 0)),
        pl.BlockSpec((None, s, 2 * HEAD_DIM), lambda bi, p, i: (bi, 0, kv_idx(p))),
        pl.BlockSpec((None, nkb, 2 * HEAD_DIM, t), lambda bi, p, i: (bi, 0, kv_idx(p), 0)),
    ]
    args = [bias, qT, k, vT]
    if has_sink:
        in_specs = [pl.BlockSpec(memory_space=pltpu.SMEM)] + in_specs
        args = [sink] + args
    return pl.pallas_call(
        kern,
        grid=(b, npairs, nqb),
        in_specs=in_specs,
        out_specs=pl.BlockSpec((None, None, 2 * HEAD_DIM, t), lambda bi, p, i: (bi, i, p, 0)),
        out_shape=jax.ShapeDtypeStruct((b, nqb, nq, t), BF16),
        scratch_shapes=[
            pltpu.VMEM((2 * HEAD_DIM, 2 * t), BF16),
            pltpu.VMEM((1, 2 * t), F32),
            pltpu.VMEM((1, 2 * t), F32),
            pltpu.VMEM((2, HEAD_DIM, t), F32),
            pltpu.VMEM((t, 2 * t), F32),
            pltpu.VMEM((t, 2 * t), F32),
        ],
        compiler_params=_params("parallel", "parallel", "arbitrary"),
        name="band_attention",
    )(*args)


def _diff_attn_kernel(lam_ref, subg_ref, qT_ref, k_ref, vT_ref, oT_ref, qz_ref, m_ref, l_ref, acc_ref,
                      sa_ref, sb_ref, *, t, nkb, cw, lam_init):
    _stage_queries(qT_ref, qz_ref, t)
    _init_stats(m_ref, l_ref, acc_ref)

    def consume(kb, src_ref):
        v = vT_ref[kb]
        for c in range(2 * t // cw):
            u, cc = divmod(c, t // cw)
            _softmax_pv(src_ref[:, c * cw:(c + 1) * cw], v, slice(c * cw, (c + 1) * cw),
                        acc_ref.at[u, :, cc * cw:(cc + 1) * cw], m_ref, l_ref)

    _scores(k_ref, 0, qz_ref, sa_ref, t)

    def two_blocks(j, carry):
        _scores(k_ref, 2 * j + 1, qz_ref, sb_ref, t)
        consume(2 * j, sa_ref)
        _scores(k_ref, 2 * j + 2, qz_ref, sa_ref, t)
        consume(2 * j + 1, sb_ref)
        return carry

    lax.fori_loop(0, nkb // 2 - 1, two_blocks, 0)
    _scores(k_ref, nkb - 1, qz_ref, sb_ref, t)
    consume(nkb - 2, sa_ref)
    consume(nkb - 1, sb_ref)

    lv = lam_ref[...]
    e1 = jnp.exp(jnp.sum(lv[0:1] * lv[1:2], axis=-1, keepdims=True))
    e2 = jnp.exp(jnp.sum(lv[2:3] * lv[3:4], axis=-1, keepdims=True))
    lam = e1 - e2 + lam_init
    o = acc_ref[0] / l_ref[:, 0:t] - lam * (acc_ref[1] / l_ref[:, t:2 * t])
    ms = jnp.mean(o * o, axis=0, keepdims=True)
    o = o * lax.rsqrt(ms + EPS) * subg_ref[...] * (1.0 - lam_init)
    oT_ref[...] = o.astype(BF16)


def _diff_attention(qT, k, vT, lamv, subg, *, lam_init):
    b, nqb, nq, t = qT.shape
    s = k.shape[1]
    nkb = s // t
    heads = nq // (2 * HEAD_DIM)
    assert nkb % 2 == 0
    kern = functools.partial(_diff_attn_kernel, t=t, nkb=nkb, cw=DENSE_CHUNK, lam_init=lam_init)
    return pl.pallas_call(
        kern,
        grid=(b, heads, nqb),
        in_specs=[
            pl.BlockSpec(lamv.shape, lambda bi, h, i: (0, 0)),
            pl.BlockSpec(subg.shape, lambda bi, h, i: (0, 0)),
            pl.BlockSpec((None, None, 2 * HEAD_DIM, t), lambda bi, h, i: (bi, i, h, 0)),
            pl.BlockSpec((None, s, 2 * HEAD_DIM), lambda bi, h, i: (bi, 0, h)),
            pl.BlockSpec((None, nkb, 2 * HEAD_DIM, t), lambda bi, h, i: (bi, 0, h, 0)),
        ],
        out_specs=pl.BlockSpec((None, None, 2 * HEAD_DIM, t), lambda bi, h, i: (bi, i, h, 0)),
        out_shape=jax.ShapeDtypeStruct((b, nqb, nq, t), BF16),
        scratch_shapes=[
            pltpu.VMEM((2 * HEAD_DIM, 2 * t), BF16),
            pltpu.VMEM((1, 2 * t), F32),
            pltpu.VMEM((1, 2 * t), F32),
            pltpu.VMEM((2, 2 * HEAD_DIM, t), F32),
            pltpu.VMEM((t, 2 * t), F32),
            pltpu.VMEM((t, 2 * t), F32),
        ],
        compiler_params=_params("parallel", "parallel", "arbitrary"),
        name="diff_attention",
    )(lamv, subg, qT, k, vT)


def _mid_kernel(h_ref, oT_ref, wout_ref, gmix_ref, gpre_ref, wq_ref, kT_ref, v_ref, wo_ref, gpost_ref,
                out_ref, *, x_scale):
    ys = [lax.dot_general(oT_ref[j], wout_ref[...], _TN, preferred_element_type=F32)
          for j in range(oT_ref.shape[0])]
    y = jnp.concatenate(ys, axis=0) if len(ys) > 1 else ys[0]
    h1 = h_ref[...] + _rms(y, gmix_ref[...])

    u = _rms(h1, gpre_ref[...]).astype(BF16)
    q = (jnp.dot(u, wq_ref[...], preferred_element_type=F32) * x_scale).astype(BF16)
    xd = q.shape[1] // X_HEADS
    outs = []
    for hd in range(X_HEADS):
        s = jnp.dot(q[:, hd * xd:(hd + 1) * xd], kT_ref[hd * xd:(hd + 1) * xd, :],
                    preferred_element_type=F32)
        p = jnp.exp2(s - jnp.max(s, axis=-1, keepdims=True))
        l = jnp.sum(p, axis=-1, keepdims=True)
        o = jnp.dot(p.astype(BF16), v_ref[:, hd * xd:(hd + 1) * xd], preferred_element_type=F32)
        outs.append((o / l).astype(BF16))
    y2 = jnp.dot(jnp.concatenate(outs, axis=1), wo_ref[...], preferred_element_type=F32)
    out_ref[...] = h1 + _rms(y2, gpost_ref[...])


def _mid(h, oT, wout, gmix, gpre, wq, kT, v, wo, gpost, *, x_scale):
    b, s, d = h.shape
    tm = ROW_TILE
    oblk = oT.shape[3]
    n_mem = v.shape[1]
    const = lambda bi, i: (0, 0)
    kern = functools.partial(_mid_kernel, x_scale=x_scale)
    return pl.pallas_call(
        kern,
        grid=(b, s // tm),
        in_specs=[
            pl.BlockSpec((None, tm, d), lambda bi, i: (bi, i, 0)),
            pl.BlockSpec((None, tm // oblk, d, oblk), lambda bi, i: (bi, i, 0, 0)),
            pl.BlockSpec((d, d), const),
            pl.BlockSpec((1, d), const),
            pl.BlockSpec((1, d), const),
            pl.BlockSpec((d, d), const),
            pl.BlockSpec((None, d, n_mem), lambda bi, i: (bi, 0, 0)),
            pl.BlockSpec((None, n_mem, d), lambda bi, i: (bi, 0, 0)),
            pl.BlockSpec((d, d), const),
            pl.BlockSpec((1, d), const),
        ],
        out_specs=pl.BlockSpec((None, tm, d), lambda bi, i: (bi, i, 0)),
        out_shape=jax.ShapeDtypeStruct((b, s, d), F32),
        compiler_params=_params("parallel", "parallel"),
        name="out_proj_cross_attention",
    )(h, oT, wout, gmix, gpre, wq, kT, v, wo, gpost)


def _mem_kv_kernel(mem_ref, g_ref, wkT_ref, wv_ref, kT_ref, v_ref):
    mn = _rms(mem_ref[...], g_ref[...]).astype(BF16)
    kT_ref[...] = lax.dot_general(wkT_ref[...], mn, _NT, preferred_element_type=F32).astype(BF16)
    v_ref[...] = jnp.dot(mn, wv_ref[...], preferred_element_type=F32).astype(BF16)


def _mem_kv(mem, g, wkT, wv):
    depth, d = g.shape[0], g.shape[2]
    b, n_mem, _ = mem.shape
    return pl.pallas_call(
        _mem_kv_kernel,
        grid=(depth, b),
        in_specs=[
            pl.BlockSpec((None, n_mem, d), lambda li, bi: (bi, 0, 0)),
            pl.BlockSpec((None, 1, d), lambda li, bi: (li, 0, 0)),
            pl.BlockSpec((None, d, d), lambda li, bi: (li, 0, 0)),
            pl.BlockSpec((None, d, d), lambda li, bi: (li, 0, 0)),
        ],
        out_specs=[
            pl.BlockSpec((None, None, d, n_mem), lambda li, bi: (li, bi, 0, 0)),
            pl.BlockSpec((None, None, n_mem, d), lambda li, bi: (li, bi, 0, 0)),
        ],
        out_shape=[
            jax.ShapeDtypeStruct((depth, b, d, n_mem), BF16),
            jax.ShapeDtypeStruct((depth, b, n_mem, d), BF16),
        ],
        compiler_params=_params("parallel", "parallel"),
        name="memory_kv",
    )(mem, g, wkT, wv)


def _ffn_kernel(h_ref, gpre_ref, wg_ref, wu_ref, wd_ref, gpost_ref, out_ref, u_ref, acc_ref):
    j = pl.program_id(1)

    @pl.when(j == 0)
    def _():
        u_ref[...] = _rms(h_ref[...], gpre_ref[...]).astype(BF16)
        acc_ref[...] = jnp.zeros(acc_ref.shape, F32)

    u = u_ref[...]
    g = jnp.dot(u, wg_ref[...], preferred_element_type=F32)
    up = jnp.dot(u, wu_ref[...], preferred_element_type=F32)
    a = (g / (1.0 + jnp.exp(-g)) * up).astype(BF16)
    acc_ref[...] += jnp.dot(a, wd_ref[...], preferred_element_type=F32)

    @pl.when(j == pl.num_programs(1) - 1)
    def _():
        out_ref[...] = h_ref[...] + _rms(acc_ref[...], gpost_ref[...])


def _ffn(h, gpre, wgu, wd, gpost):
    b, s, d = h.shape
    tm = ROW_TILE
    dff = wd.shape[0]
    fc = dff // FF_CHUNKS
    rows = b * s
    h2 = h.reshape(rows, d)
    out = pl.pallas_call(
        _ffn_kernel,
        grid=(rows // tm, FF_CHUNKS),
        in_specs=[
            pl.BlockSpec((tm, d), lambda i, j: (i, 0)),
            pl.BlockSpec((1, d), lambda i, j: (0, 0)),
            pl.BlockSpec((d, fc), lambda i, j: (0, j)),
            pl.BlockSpec((d, fc), lambda i, j: (0, FF_CHUNKS + j)),
            pl.BlockSpec((fc, d), lambda i, j: (j, 0)),
            pl.BlockSpec((1, d), lambda i, j: (0, 0)),
        ],
        out_specs=pl.BlockSpec((tm, d), lambda i, j: (i, 0)),
        out_shape=jax.ShapeDtypeStruct((rows, d), F32),
        scratch_shapes=[pltpu.VMEM((tm, d), BF16), pltpu.VMEM((tm, d), F32)],
        compiler_params=_params("parallel", "arbitrary"),
        name="swiglu_ffn",
    )(h2, gpre, wgu, wgu, wd, gpost)
    return out.reshape(b, s, d)


def _rope_tables(positions):
    inv_freq = ROPE_THETA ** (-jnp.arange(0, 2 * ROT_HALF, 2, dtype=F32) / (2 * ROT_HALF))
    ang = positions.astype(F32)[..., None] * inv_freq
    cos, sin = jnp.cos(ang), jnp.sin(ang)
    cosT, sinT = cos.transpose(0, 2, 1), sin.transpose(0, 2, 1)
    zeros = jnp.zeros_like(cos)
    pad = HEAD_DIM - 2 * ROT_HALF
    ones_tail = jnp.ones(cos.shape[:-1] + (pad,), F32)
    zero_tail = jnp.zeros(cos.shape[:-1] + (pad,), F32)
    reps = LANES // HEAD_DIM
    kc = jnp.tile(jnp.concatenate([cos, cos, ones_tail], axis=-1), reps)
    ka = jnp.tile(jnp.concatenate([-sin, zeros, zero_tail], axis=-1), reps)
    kb = jnp.tile(jnp.concatenate([zeros, sin, zero_tail], axis=-1), reps)
    return cosT, sinT, kc, ka, kb


def _band_bias(t, width, multiplicity):
    i = np.arange(t)[:, None]
    j = np.arange(t)[None, :]
    tiles = []
    for d in range(-width, width + 1):
        c = multiplicity(d * t + i - j)
        tiles.append(np.where(c > 0, np.log2(np.maximum(c, 1)), NEG_INF))
    tiles.append(np.full((t, t), NEG_INF))
    return jnp.asarray(np.stack(tiles), F32)


def _dilated_multiplicity(delta):
    c = np.zeros(delta.shape, np.int64)
    for window, dil in A_PATTERNS:
        c += (delta % dil == 0) & (np.abs(delta) <= (window // (2 * dil)) * dil)
    return c


def _window_multiplicity(delta):
    return (np.abs(delta) <= C_RADIUS).astype(np.int64)


def _row(g):
    return g.reshape(1, -1)


def kernel(x, mem, positions, mix_pre_g, mix_post_g, mem_pre_g, mem_kv_g, mem_post_g, ffn_pre_g, ffn_post_g,
           a_w_in, a_w_out, b_w_in, b_w_out, b_lam_q1, b_lam_k1, b_lam_q2, b_lam_k2, b_sub_g, c_w_in, c_w_out,
           c_sink, x_wq, x_wkv, x_wo, w_gate_up, w_down):
    depth, d = mix_pre_g.shape
    assert d % (2 * HEAD_DIM) == 0 and x.shape[1] % DENSE_BLOCK == 0 and x.shape[1] % ROW_TILE == 0
    cosT, sinT, kc, ka, kb = _rope_tables(positions)
    q_scale = HEAD_DIM ** -0.5 * LOG2E
    x_scale = (d // X_HEADS) ** -0.5 * LOG2E

    mem_kT, mem_v = _mem_kv(mem, mem_kv_g.reshape(depth, 1, d),
                            x_wkv[:, :, :d].transpose(0, 2, 1).astype(BF16), x_wkv[:, :, d:].astype(BF16))

    a_width = max((w // (2 * dl)) * dl for w, dl in A_PATTERNS) // BAND_BLOCK
    a_bias = _band_bias(BAND_BLOCK, a_width, _dilated_multiplicity)
    c_width = -(-C_RADIUS // BAND_BLOCK)
    c_bias = _band_bias(BAND_BLOCK, c_width, _window_multiplicity)

    h = x
    for i in range(depth):
        kind, j = i % N_MIXERS, i // N_MIXERS
        if kind == 0:
            w_in, w_out = a_w_in[j], a_w_out[j]
            wq, wk, wv = w_in[:, :d], w_in[:, d:2 * d], w_in[:, 2 * d:]
            blk = BAND_BLOCK
        elif kind == 1:
            w_in, w_out = b_w_in[j], b_w_out[j]
            wq, wk, wv = w_in[:, :d], w_in[:, d:2 * d], w_in[:, 2 * d:]
            blk = DENSE_BLOCK
        else:
            w_in, w_out = c_w_in[j], c_w_out[j]
            kvw = w_in.shape[1] - d
            wq, wk, wv = w_in[:, :d], w_in[:, d:d + kvw // 2], w_in[:, d + kvw // 2:]
            n_kv = (kvw // 2) // HEAD_DIM
            grp = (d // HEAD_DIM) // n_kv
            perm = np.arange(d).reshape(n_kv, grp, HEAD_DIM).transpose(1, 0, 2).reshape(-1)
            wq, w_out = wq[:, perm], w_out[perm, :]
            blk = BAND_BLOCK
        qT, k, vT = _in_proj(h, _row(mix_pre_g[i]), wq.T.astype(BF16), wk.astype(BF16), wv.T.astype(BF16),
                             cosT, sinT, kc, ka, kb, q_scale=q_scale, qblk=blk, vblk=blk)
        if kind == 0:
            oT = _band_attention(qT, k, vT, a_bias, None, width=a_width, shared_kv=False)
        elif kind == 1:
            lam_init = 0.8 - 0.6 * math.exp(-0.3 * i)
            lamv = jnp.stack([b_lam_q1[j], b_lam_k1[j], b_lam_q2[j], b_lam_k2[j]]).astype(F32)
            subg = jnp.broadcast_to(b_sub_g[j].astype(F32)[:, None], (2 * HEAD_DIM, DENSE_BLOCK))
            oT = _diff_attention(qT, k, vT, lamv, subg, lam_init=lam_init)
        else:
            sink = (c_sink[j].astype(F32) * LOG2E)[perm[::HEAD_DIM] // HEAD_DIM]
            oT = _band_attention(qT, k, vT, c_bias, sink, width=c_width, shared_kv=True)
        h = _mid(h, oT, w_out.astype(BF16), _row(mix_post_g[i]), _row(mem_pre_g[i]), x_wq[i].astype(BF16),
                 mem_kT[i], mem_v[i], x_wo[i].astype(BF16), _row(mem_post_g[i]), x_scale=x_scale)
        h = _ffn(h, _row(ffn_pre_g[i]), w_gate_up[i].astype(BF16), w_down[i].astype(BF16), _row(ffn_post_g[i]))
    return h
```

```python
import functools
import math

import jax
import jax.numpy as jnp
import numpy as np
from jax import lax
from jax.experimental import pallas as pl
from jax.experimental.pallas import tpu as pltpu

F32 = jnp.float32
BF16 = jnp.bfloat16

HEAD_DIM = 64
ROT_HALF = HEAD_DIM // 8
ROPE_THETA = 500000.0
EPS = 1e-6
NEG_INF = -1e30
LOG2E = 1.4426950408889634
N_MIXERS = 3

A_PATTERNS = ((128, 1), (512, 4), (2048, 16))
C_RADIUS = 128
X_HEADS = 4

LANES = 128
ROW_TILE = 512
WINDOW_BLOCK = 256
DILATED_BLOCK = 512
DENSE_BLOCK = 512
ATTN_CHUNK = 256
ATTN_LEAD = 1
DENSE_PER_TRIP = 4
BAND_PER_TRIP = 8
FF_CHUNKS = 2
VMEM_LIMIT = 56 * 1024 * 1024

_NT = (((1,), (1,)), ((), ()))
_TN = (((0,), (0,)), ((), ()))


def _params(*sem):
    return pltpu.CompilerParams(dimension_semantics=sem, vmem_limit_bytes=VMEM_LIMIT)


def _rms(x, g):
    ms = jnp.mean(x * x, axis=-1, keepdims=True)
    return x * lax.rsqrt(ms + EPS) * g


def _in_proj_kernel(h_ref, g_ref, wqT_ref, wk_ref, wvT_ref, cosT_ref, sinT_ref, kc_ref, ka_ref, kb_ref,
                    qT_ref, k_ref, vT_ref, *, q_scale, qblk, vblk):
    tm = h_ref.shape[0]
    u = _rms(h_ref[...], g_ref[...]).astype(BF16)

    kf = jnp.dot(u, wk_ref[...], preferred_element_type=F32)
    kc, ka, kb = kc_ref[...], ka_ref[...], kb_ref[...]
    for j in range(kf.shape[1] // LANES):
        x = kf[:, j * LANES:(j + 1) * LANES]
        y = x * kc + pltpu.roll(x, LANES - ROT_HALF, 1) * ka + pltpu.roll(x, ROT_HALF, 1) * kb
        k_ref[:, j * LANES:(j + 1) * LANES] = y.astype(BF16)

    qf = lax.dot_general(wqT_ref[...], u, _NT, preferred_element_type=F32)
    c = cosT_ref[...] * q_scale
    s = sinT_ref[...] * q_scale
    for unit in range(qf.shape[0] // HEAD_DIM):
        r0 = unit * HEAD_DIM
        t1 = qf[r0:r0 + ROT_HALF]
        t2 = qf[r0 + ROT_HALF:r0 + 2 * ROT_HALF]
        rest = qf[r0 + 2 * ROT_HALF:r0 + HEAD_DIM] * q_scale
        blk = jnp.concatenate([t1 * c - t2 * s, t2 * c + t1 * s, rest], axis=0).astype(BF16)
        for jb in range(tm // qblk):
            qT_ref[jb, r0:r0 + HEAD_DIM, :] = blk[:, jb * qblk:(jb + 1) * qblk]

    vf = lax.dot_general(wvT_ref[...], u, _NT, preferred_element_type=F32).astype(BF16)
    for jb in range(tm // vblk):
        vT_ref[jb] = vf[:, jb * vblk:(jb + 1) * vblk]


def _in_proj(h, g, wqT, wk, wvT, cosT, sinT, kc, ka, kb, *, q_scale, qblk, vblk):
    b, s, d = h.shape
    nq, nk, nv = wqT.shape[0], wk.shape[1], wvT.shape[0]
    tm = ROW_TILE
    kern = functools.partial(_in_proj_kernel, q_scale=q_scale, qblk=qblk, vblk=vblk)
    const = lambda bi, i: (0, 0)
    return pl.pallas_call(
        kern,
        grid=(b, s // tm),
        in_specs=[
            pl.BlockSpec((None, tm, d), lambda bi, i: (bi, i, 0)),
            pl.BlockSpec((1, d), const),
            pl.BlockSpec((nq, d), const),
            pl.BlockSpec((d, nk), const),
            pl.BlockSpec((nv, d), const),
            pl.BlockSpec((None, ROT_HALF, tm), lambda bi, i: (bi, 0, i)),
            pl.BlockSpec((None, ROT_HALF, tm), lambda bi, i: (bi, 0, i)),
            pl.BlockSpec((None, tm, LANES), lambda bi, i: (bi, i, 0)),
            pl.BlockSpec((None, tm, LANES), lambda bi, i: (bi, i, 0)),
            pl.BlockSpec((None, tm, LANES), lambda bi, i: (bi, i, 0)),
        ],
        out_specs=[
            pl.BlockSpec((None, tm // qblk, nq, qblk), lambda bi, i: (bi, i, 0, 0)),
            pl.BlockSpec((None, tm, nk), lambda bi, i: (bi, i, 0)),
            pl.BlockSpec((None, tm // vblk, nv, vblk), lambda bi, i: (bi, i, 0, 0)),
        ],
        out_shape=[
            jax.ShapeDtypeStruct((b, s // qblk, nq, qblk), BF16),
            jax.ShapeDtypeStruct((b, s, nk), BF16),
            jax.ShapeDtypeStruct((b, s // vblk, nv, vblk), BF16),
        ],
        compiler_params=_params("parallel", "parallel"),
        name="mixer_in_proj",
    )(h, g, wqT, wk, wvT, cosT, sinT, kc, ka, kb)


def _stage_queries(qT_ref, qz_ref, t):
    q = qT_ref[...]
    row = lax.broadcasted_iota(jnp.int32, q.shape, 0)
    zero = jnp.zeros_like(q)
    qz_ref[:, 0:t] = jnp.where(row < HEAD_DIM, q, zero)
    qz_ref[:, t:2 * t] = jnp.where(row >= HEAD_DIM, q, zero)


def _init_stats(m_ref, l_ref, acc_ref):
    m_ref[...] = jnp.full(m_ref.shape, NEG_INF, F32)
    l_ref[...] = jnp.zeros(l_ref.shape, F32)
    acc_ref[...] = jnp.zeros(acc_ref.shape, F32)


def _scores_chunk(k_ref, kb, qz_ref, dst_ref, c, t, cw):
    kblk = k_ref[pl.ds(pl.multiple_of(kb * t, t), t), :]
    cols = slice(c * cw, (c + 1) * cw)
    dst_ref[:, cols] = jnp.dot(kblk, qz_ref[:, cols], preferred_element_type=F32)


def _softmax_pv(s, v, cols, acc_at, m_ref, l_ref):
    m_old = m_ref[:, cols]
    m_new = jnp.maximum(m_old, jnp.max(s, axis=0, keepdims=True))
    alpha = jnp.exp2(m_old - m_new)
    p = jnp.exp2(s - m_new)
    l_ref[:, cols] = alpha * l_ref[:, cols] + jnp.sum(p, axis=0, keepdims=True)
    m_ref[:, cols] = m_new
    acc_at[...] = alpha * acc_at[...] + jnp.dot(v, p.astype(BF16), preferred_element_type=F32)


def _block_step(kb, src_ref, kb_next, dst_ref, refs, *, t, cw, lead, shared_v, bias=None):
    k_ref, vT_ref, qz_ref, m_ref, l_ref, acc_ref = refs
    nchunk = 2 * t // cw
    v = vT_ref[kb]
    if kb_next is not None:
        for c in range(lead):
            _scores_chunk(k_ref, kb_next, qz_ref, dst_ref, c, t, cw)
    for c in range(nchunk):
        if kb_next is not None and c + lead < nchunk:
            _scores_chunk(k_ref, kb_next, qz_ref, dst_ref, c + lead, t, cw)
        u, cc = divmod(c, t // cw)
        s = src_ref[:, c * cw:(c + 1) * cw]
        if bias is not None:
            s = s + bias[:, cc * cw:(cc + 1) * cw]
        vu = v if shared_v else v[u * HEAD_DIM:(u + 1) * HEAD_DIM]
        _softmax_pv(s, vu, slice(c * cw, (c + 1) * cw), acc_ref.at[u, :, cc * cw:(cc + 1) * cw], m_ref, l_ref)


def _band_attn_kernel(*refs, t, cw, lead, per_trip, width, nkb, has_sink):
    if has_sink:
        sink_ref, bias_ref, qT_ref, k_ref, vT_ref, oT_ref, qz_ref, m_ref, l_ref, acc_ref, sa_ref, sb_ref = refs
    else:
        bias_ref, qT_ref, k_ref, vT_ref, oT_ref, qz_ref, m_ref, l_ref, acc_ref, sa_ref, sb_ref = refs
    pair = pl.program_id(1)
    qb = pl.program_id(2)
    _stage_queries(qT_ref, qz_ref, t)
    _init_stats(m_ref, l_ref, acc_ref)
    nblk = 2 * width + 1
    bufs = (sa_ref, sb_ref)
    step_refs = (k_ref, vT_ref, qz_ref, m_ref, l_ref, acc_ref)

    def block(d):
        kb = qb + d
        valid = jnp.logical_and(kb >= 0, kb < nkb)
        return jnp.clip(kb, 0, nkb - 1), jnp.where(valid, d + width, nblk)

    def step(d, parity, last):
        kb, bidx = block(d)
        _block_step(kb, bufs[parity], None if last else block(d + 1)[0], bufs[1 - parity], step_refs,
                    t=t, cw=cw, lead=lead, shared_v=False, bias=bias_ref[bidx])

    for c in range(2 * t // cw):
        _scores_chunk(k_ref, block(-width)[0], qz_ref, bufs[0], c, t, cw)

    def trip(j, carry):
        for i in range(per_trip):
            step(per_trip * j + i - width, i % 2, False)
        return carry

    ntrip = (nblk - 1) // per_trip
    lax.fori_loop(0, ntrip, trip, 0)
    for n in range(ntrip * per_trip, nblk):
        step(n - width, n % 2, n == nblk - 1)

    for u in range(2):
        m = m_ref[:, u * t:(u + 1) * t]
        l = l_ref[:, u * t:(u + 1) * t]
        acc = acc_ref[u]
        if has_sink:
            sk = sink_ref[pair * 2 + u]
            m2 = jnp.maximum(m, sk)
            w = jnp.exp2(m - m2)
            l = l * w + jnp.exp2(sk - m2)
            acc = acc * w
        oT_ref[u * HEAD_DIM:(u + 1) * HEAD_DIM, :] = (acc / l).astype(BF16)


def _band_attention(qT, k, vT, bias, sink, *, width, shared_kv):
    b, nqb, nq, t = qT.shape
    s = k.shape[1]
    nkb = s // t
    npairs = nq // (2 * HEAD_DIM)
    kv_idx = (lambda p: 0) if shared_kv else (lambda p: p)
    has_sink = sink is not None
    kern = functools.partial(_band_attn_kernel, t=t, cw=min(t, ATTN_CHUNK), lead=ATTN_LEAD,
                             per_trip=BAND_PER_TRIP, width=width, nkb=nkb, has_sink=has_sink)
    in_specs = [
        pl.BlockSpec(bias.shape, lambda bi, p, i: (0, 0, 0)),
        pl.BlockSpec((None, None, 2 * HEAD_DIM, t), lambda bi, p, i: (bi, i, p, 0)),
        pl.BlockSpec((None, s, 2 * HEAD_DIM), lambda bi, p, i: (bi, 0, kv_idx(p))),
        pl.BlockSpec((None, nkb, 2 * HEAD_DIM, t), lambda bi, p, i: (bi, 0, kv_idx(p), 0)),
    ]
    args = [bias, qT, k, vT]
    if has_sink:
        in_specs = [pl.BlockSpec(memory_space=pltpu.SMEM)] + in_specs
        args = [sink] + args
    return pl.pallas_call(
        kern,
        grid=(b, npairs, nqb),
        in_specs=in_specs,
        out_specs=pl.BlockSpec((None, None, 2 * HEAD_DIM, t), lambda bi, p, i: (bi, i, p, 0)),
        out_shape=jax.ShapeDtypeStruct((b, nqb, nq, t), BF16),
        scratch_shapes=[
            pltpu.VMEM((2 * HEAD_DIM, 2 * t), BF16),
            pltpu.VMEM((1, 2 * t), F32),
            pltpu.VMEM((1, 2 * t), F32),
            pltpu.VMEM((2, HEAD_DIM, t), F32),
            pltpu.VMEM((t, 2 * t), F32),
            pltpu.VMEM((t, 2 * t), F32),
        ],
        compiler_params=_params("parallel", "parallel", "arbitrary"),
        name="band_attention",
    )(*args)


def _diff_attn_kernel(lam_ref, subg_ref, qT_ref, k_ref, vT_ref, oT_ref, qz_ref, m_ref, l_ref, acc_ref,
                      sa_ref, sb_ref, *, t, nkb, cw, lead, per_trip, lam_init):
    _stage_queries(qT_ref, qz_ref, t)
    _init_stats(m_ref, l_ref, acc_ref)

    bufs = (sa_ref, sb_ref)
    step = functools.partial(_block_step, refs=(k_ref, vT_ref, qz_ref, m_ref, l_ref, acc_ref),
                             t=t, cw=cw, lead=lead, shared_v=True)

    for c in range(2 * t // cw):
        _scores_chunk(k_ref, 0, qz_ref, sa_ref, c, t, cw)

    def trip(j, carry):
        for i in range(per_trip):
            step(per_trip * j + i, bufs[i % 2], per_trip * j + i + 1, bufs[(i + 1) % 2])
        return carry

    ntrip = nkb // per_trip - 1
    lax.fori_loop(0, ntrip, trip, 0)
    for i in range(per_trip):
        kb = ntrip * per_trip + i
        step(kb, bufs[i % 2], kb + 1 if i + 1 < per_trip else None, bufs[(i + 1) % 2])

    lv = lam_ref[...]
    e1 = jnp.exp(jnp.sum(lv[0:1] * lv[1:2], axis=-1, keepdims=True))
    e2 = jnp.exp(jnp.sum(lv[2:3] * lv[3:4], axis=-1, keepdims=True))
    lam = e1 - e2 + lam_init
    o = acc_ref[0] / l_ref[:, 0:t] - lam * (acc_ref[1] / l_ref[:, t:2 * t])
    ms = jnp.mean(o * o, axis=0, keepdims=True)
    o = o * lax.rsqrt(ms + EPS) * subg_ref[...] * (1.0 - lam_init)
    oT_ref[...] = o.astype(BF16)


def _diff_attention(qT, k, vT, lamv, subg, *, lam_init):
    b, nqb, nq, t = qT.shape
    s = k.shape[1]
    nkb = s // t
    heads = nq // (2 * HEAD_DIM)
    assert nkb % DENSE_PER_TRIP == 0 and DENSE_PER_TRIP % 2 == 0
    kern = functools.partial(_diff_attn_kernel, t=t, nkb=nkb, cw=ATTN_CHUNK, lead=ATTN_LEAD,
                             per_trip=DENSE_PER_TRIP, lam_init=lam_init)
    return pl.pallas_call(
        kern,
        grid=(b, heads, nqb),
        in_specs=[
            pl.BlockSpec(lamv.shape, lambda bi, h, i: (0, 0)),
            pl.BlockSpec(subg.shape, lambda bi, h, i: (0, 0)),
            pl.BlockSpec((None, None, 2 * HEAD_DIM, t), lambda bi, h, i: (bi, i, h, 0)),
            pl.BlockSpec((None, s, 2 * HEAD_DIM), lambda bi, h, i: (bi, 0, h)),
            pl.BlockSpec((None, nkb, 2 * HEAD_DIM, t), lambda bi, h, i: (bi, 0, h, 0)),
        ],
        out_specs=pl.BlockSpec((None, None, 2 * HEAD_DIM, t), lambda bi, h, i: (bi, i, h, 0)),
        out_shape=jax.ShapeDtypeStruct((b, nqb, nq, t), BF16),
        scratch_shapes=[
            pltpu.VMEM((2 * HEAD_DIM, 2 * t), BF16),
            pltpu.VMEM((1, 2 * t), F32),
            pltpu.VMEM((1, 2 * t), F32),
            pltpu.VMEM((2, 2 * HEAD_DIM, t), F32),
            pltpu.VMEM((t, 2 * t), F32),
            pltpu.VMEM((t, 2 * t), F32),
        ],
        compiler_params=_params("parallel", "parallel", "arbitrary"),
        name="diff_attention",
    )(lamv, subg, qT, k, vT)


def _mid_kernel(h_ref, oT_ref, wout_ref, gmix_ref, gpre_ref, wq_ref, kT_ref, v_ref, wo_ref, gpost_ref,
                out_ref, *, x_scale):
    ys = [lax.dot_general(oT_ref[j], wout_ref[...], _TN, preferred_element_type=F32)
          for j in range(oT_ref.shape[0])]
    y = jnp.concatenate(ys, axis=0) if len(ys) > 1 else ys[0]
    h1 = h_ref[...] + _rms(y, gmix_ref[...])

    u = _rms(h1, gpre_ref[...]).astype(BF16)
    q = (jnp.dot(u, wq_ref[...], preferred_element_type=F32) * x_scale).astype(BF16)
    xd = q.shape[1] // X_HEADS
    outs = []
    for hd in range(X_HEADS):
        s = jnp.dot(q[:, hd * xd:(hd + 1) * xd], kT_ref[hd * xd:(hd + 1) * xd, :],
                    preferred_element_type=F32)
        p = jnp.exp2(s - jnp.max(s, axis=-1, keepdims=True))
        l = jnp.sum(p, axis=-1, keepdims=True)
        o = jnp.dot(p.astype(BF16), v_ref[:, hd * xd:(hd + 1) * xd], preferred_element_type=F32)
        outs.append((o / l).astype(BF16))
    y2 = jnp.dot(jnp.concatenate(outs, axis=1), wo_ref[...], preferred_element_type=F32)
    out_ref[...] = h1 + _rms(y2, gpost_ref[...])


def _mid(h, oT, wout, gmix, gpre, wq, kT, v, wo, gpost, *, x_scale):
    b, s, d = h.shape
    tm = ROW_TILE
    oblk = oT.shape[3]
    n_mem = v.shape[1]
    const = lambda bi, i: (0, 0)
    kern = functools.partial(_mid_kernel, x_scale=x_scale)
    return pl.pallas_call(
        kern,
        grid=(b, s // tm),
        in_specs=[
            pl.BlockSpec((None, tm, d), lambda bi, i: (bi, i, 0)),
            pl.BlockSpec((None, tm // oblk, d, oblk), lambda bi, i: (bi, i, 0, 0)),
            pl.BlockSpec((d, d), const),
            pl.BlockSpec((1, d), const),
            pl.BlockSpec((1, d), const),
            pl.BlockSpec((d, d), const),
            pl.BlockSpec((None, d, n_mem), lambda bi, i: (bi, 0, 0)),
            pl.BlockSpec((None, n_mem, d), lambda bi, i: (bi, 0, 0)),
            pl.BlockSpec((d, d), const),
            pl.BlockSpec((1, d), const),
        ],
        out_specs=pl.BlockSpec((None, tm, d), lambda bi, i: (bi, i, 0)),
        out_shape=jax.ShapeDtypeStruct((b, s, d), F32),
        compiler_params=_params("parallel", "parallel"),
        name="out_proj_cross_attention",
    )(h, oT, wout, gmix, gpre, wq, kT, v, wo, gpost)


def _mem_kv_kernel(mem_ref, g_ref, wkT_ref, wv_ref, kT_ref, v_ref):
    mn = _rms(mem_ref[...], g_ref[...]).astype(BF16)
    kT_ref[...] = lax.dot_general(wkT_ref[...], mn, _NT, preferred_element_type=F32).astype(BF16)
    v_ref[...] = jnp.dot(mn, wv_ref[...], preferred_element_type=F32).astype(BF16)


def _mem_kv(mem, g, wkT, wv):
    depth, d = g.shape[0], g.shape[2]
    b, n_mem, _ = mem.shape
    return pl.pallas_call(
        _mem_kv_kernel,
        grid=(depth, b),
        in_specs=[
            pl.BlockSpec((None, n_mem, d), lambda li, bi: (bi, 0, 0)),
            pl.BlockSpec((None, 1, d), lambda li, bi: (li, 0, 0)),
            pl.BlockSpec((None, d, d), lambda li, bi: (li, 0, 0)),
            pl.BlockSpec((None, d, d), lambda li, bi: (li, 0, 0)),
        ],
        out_specs=[
            pl.BlockSpec((None, None, d, n_mem), lambda li, bi: (li, bi, 0, 0)),
            pl.BlockSpec((None, None, n_mem, d), lambda li, bi: (li, bi, 0, 0)),
        ],
        out_shape=[
            jax.ShapeDtypeStruct((depth, b, d, n_mem), BF16),
            jax.ShapeDtypeStruct((depth, b, n_mem, d), BF16),
        ],
        compiler_params=_params("parallel", "parallel"),
        name="memory_kv",
    )(mem, g, wkT, wv)


def _ffn_kernel(h_ref, gpre_ref, wg_ref, wu_ref, wd_ref, gpost_ref, out_ref, u_ref, acc_ref):
    j = pl.program_id(1)

    @pl.when(j == 0)
    def _():
        u_ref[...] = _rms(h_ref[...], gpre_ref[...]).astype(BF16)
        acc_ref[...] = jnp.zeros(acc_ref.shape, F32)

    u = u_ref[...]
    g = jnp.dot(u, wg_ref[...], preferred_element_type=F32)
    up = jnp.dot(u, wu_ref[...], preferred_element_type=F32)
    a = (g / (1.0 + jnp.exp(-g)) * up).astype(BF16)
    acc_ref[...] += jnp.dot(a, wd_ref[...], preferred_element_type=F32)

    @pl.when(j == pl.num_programs(1) - 1)
    def _():
        out_ref[...] = h_ref[...] + _rms(acc_ref[...], gpost_ref[...])


def _ffn(h, gpre, wgu, wd, gpost):
    b, s, d = h.shape
    tm = ROW_TILE
    dff = wd.shape[0]
    fc = dff // FF_CHUNKS
    rows = b * s
    h2 = h.reshape(rows, d)
    out = pl.pallas_call(
        _ffn_kernel,
        grid=(rows // tm, FF_CHUNKS),
        in_specs=[
            pl.BlockSpec((tm, d), lambda i, j: (i, 0)),
            pl.BlockSpec((1, d), lambda i, j: (0, 0)),
            pl.BlockSpec((d, fc), lambda i, j: (0, j)),
            pl.BlockSpec((d, fc), lambda i, j: (0, FF_CHUNKS + j)),
            pl.BlockSpec((fc, d), lambda i, j: (j, 0)),
            pl.BlockSpec((1, d), lambda i, j: (0, 0)),
        ],
        out_specs=pl.BlockSpec((tm, d), lambda i, j: (i, 0)),
        out_shape=jax.ShapeDtypeStruct((rows, d), F32),
        scratch_shapes=[pltpu.VMEM((tm, d), BF16), pltpu.VMEM((tm, d), F32)],
        compiler_params=_params("parallel", "arbitrary"),
        name="swiglu_ffn",
    )(h2, gpre, wgu, wgu, wd, gpost)
    return out.reshape(b, s, d)


def _rope_tables(positions):
    inv_freq = ROPE_THETA ** (-jnp.arange(0, 2 * ROT_HALF, 2, dtype=F32) / (2 * ROT_HALF))
    ang = positions.astype(F32)[..., None] * inv_freq
    cos, sin = jnp.cos(ang), jnp.sin(ang)
    cosT, sinT = cos.transpose(0, 2, 1), sin.transpose(0, 2, 1)
    zeros = jnp.zeros_like(cos)
    pad = HEAD_DIM - 2 * ROT_HALF
    ones_tail = jnp.ones(cos.shape[:-1] + (pad,), F32)
    zero_tail = jnp.zeros(cos.shape[:-1] + (pad,), F32)
    reps = LANES // HEAD_DIM
    kc = jnp.tile(jnp.concatenate([cos, cos, ones_tail], axis=-1), reps)
    ka = jnp.tile(jnp.concatenate([-sin, zeros, zero_tail], axis=-1), reps)
    kb = jnp.tile(jnp.concatenate([zeros, sin, zero_tail], axis=-1), reps)
    return cosT, sinT, kc, ka, kb


def _band_bias(t, width, multiplicity):
    i = np.arange(t)[:, None]
    j = np.arange(t)[None, :]
    tiles = []
    for d in range(-width, width + 1):
        c = multiplicity(d * t + i - j)
        tiles.append(np.where(c > 0, np.log2(np.maximum(c, 1)), NEG_INF))
    tiles.append(np.full((t, t), NEG_INF))
    return jnp.asarray(np.stack(tiles), F32)


def _dilated_multiplicity(delta):
    c = np.zeros(delta.shape, np.int64)
    for window, dil in A_PATTERNS:
        c += (delta % dil == 0) & (np.abs(delta) <= (window // (2 * dil)) * dil)
    return c


def _window_multiplicity(delta):
    return (np.abs(delta) <= C_RADIUS).astype(np.int64)


def _row(g):
    return g.reshape(1, -1)


def kernel(x, mem, positions, mix_pre_g, mix_post_g, mem_pre_g, mem_kv_g, mem_post_g, ffn_pre_g, ffn_post_g,
           a_w_in, a_w_out, b_w_in, b_w_out, b_lam_q1, b_lam_k1, b_lam_q2, b_lam_k2, b_sub_g, c_w_in, c_w_out,
           c_sink, x_wq, x_wkv, x_wo, w_gate_up, w_down):
    depth, d = mix_pre_g.shape
    assert d % (2 * HEAD_DIM) == 0 and x.shape[1] % DENSE_BLOCK == 0 and x.shape[1] % ROW_TILE == 0
    cosT, sinT, kc, ka, kb = _rope_tables(positions)
    q_scale = HEAD_DIM ** -0.5 * LOG2E
    x_scale = (d // X_HEADS) ** -0.5 * LOG2E

    mem_kT, mem_v = _mem_kv(mem, mem_kv_g.reshape(depth, 1, d),
                            x_wkv[:, :, :d].transpose(0, 2, 1).astype(BF16), x_wkv[:, :, d:].astype(BF16))

    a_width = -(-max((w // (2 * dl)) * dl for w, dl in A_PATTERNS) // DILATED_BLOCK)
    a_bias = _band_bias(DILATED_BLOCK, a_width, _dilated_multiplicity)
    c_width = -(-C_RADIUS // WINDOW_BLOCK)
    c_bias = _band_bias(WINDOW_BLOCK, c_width, _window_multiplicity)

    h = x
    for i in range(depth):
        kind, j = i % N_MIXERS, i // N_MIXERS
        if kind == 0:
            w_in, w_out = a_w_in[j], a_w_out[j]
            wq, wk, wv = w_in[:, :d], w_in[:, d:2 * d], w_in[:, 2 * d:]
            blk = DILATED_BLOCK
        elif kind == 1:
            w_in, w_out = b_w_in[j], b_w_out[j]
            wq, wk, wv = w_in[:, :d], w_in[:, d:2 * d], w_in[:, 2 * d:]
            blk = DENSE_BLOCK
        else:
            w_in, w_out = c_w_in[j], c_w_out[j]
            kvw = w_in.shape[1] - d
            wq, wk, wv = w_in[:, :d], w_in[:, d:d + kvw // 2], w_in[:, d + kvw // 2:]
            n_kv = (kvw // 2) // HEAD_DIM
            grp = (d // HEAD_DIM) // n_kv
            perm = np.arange(d).reshape(n_kv, grp, HEAD_DIM).transpose(1, 0, 2).reshape(-1)
            wq, w_out = wq[:, perm], w_out[perm, :]
            blk = WINDOW_BLOCK
        qT, k, vT = _in_proj(h, _row(mix_pre_g[i]), wq.T.astype(BF16), wk.astype(BF16), wv.T.astype(BF16),
                             cosT, sinT, kc, ka, kb, q_scale=q_scale, qblk=blk, vblk=blk)
        if kind == 0:
            oT = _band_attention(qT, k, vT, a_bias, None, width=a_width, shared_kv=False)
        elif kind == 1:
            lam_init = 0.8 - 0.6 * math.exp(-0.3 * i)
            lamv = jnp.stack([b_lam_q1[j], b_lam_k1[j], b_lam_q2[j], b_lam_k2[j]]).astype(F32)
            subg = jnp.broadcast_to(b_sub_g[j].astype(F32)[:, None], (2 * HEAD_DIM, DENSE_BLOCK))
            oT = _diff_attention(qT, k, vT, lamv, subg, lam_init=lam_init)
        else:
            sink = (c_sink[j].astype(F32) * LOG2E)[perm[::HEAD_DIM] // HEAD_DIM]
            oT = _band_attention(qT, k, vT, c_bias, sink, width=c_width, shared_kv=True)
        h = _mid(h, oT, w_out.astype(BF16), _row(mix_post_g[i]), _row(mem_pre_g[i]), x_wq[i].astype(BF16),
                 mem_kT[i], mem_v[i], x_wo[i].astype(BF16), _row(mem_post_g[i]), x_scale=x_scale)
        h = _ffn(h, _row(ffn_pre_g[i]), w_gate_up[i].astype(BF16), w_down[i].astype(BF16), _row(ffn_post_g[i]))
    return h
```

```python
import functools
import math

import jax
import jax.numpy as jnp
import numpy as np
from jax import lax
from jax.experimental import pallas as pl
from jax.experimental.pallas import tpu as pltpu

F32 = jnp.float32
BF16 = jnp.bfloat16

HEAD_DIM = 64
ROT_HALF = HEAD_DIM // 8
ROPE_THETA = 500000.0
EPS = 1e-6
NEG_INF = -1e30
LOG2E = 1.4426950408889634
N_MIXERS = 3

A_PATTERNS = ((128, 1), (512, 4), (2048, 16))
C_RADIUS = 128
X_HEADS = 4

LANES = 128
ROW_TILE = 512
WINDOW_BLOCK = 256
DILATED_BLOCK = 512
DENSE_BLOCK = 512
ATTN_CHUNK = 256
ATTN_LEAD = 1
DENSE_PER_TRIP = 4
BAND_PER_TRIP = 8
WIN_Q = 128
WIN_PER_TRIP = 4
WIN_DEPTH = 4
FF_CHUNKS = 2
VMEM_LIMIT = 56 * 1024 * 1024

_NT = (((1,), (1,)), ((), ()))
_TN = (((0,), (0,)), ((), ()))


def _params(*sem):
    return pltpu.CompilerParams(dimension_semantics=sem, vmem_limit_bytes=VMEM_LIMIT)


def _rms(x, g):
    ms = jnp.mean(x * x, axis=-1, keepdims=True)
    return x * lax.rsqrt(ms + EPS) * g


def _in_proj_kernel(h_ref, g_ref, wqT_ref, wk_ref, wvT_ref, cosT_ref, sinT_ref, kc_ref, ka_ref, kb_ref,
                    qT_ref, k_ref, vT_ref, *, q_scale, qblk, vblk):
    tm = h_ref.shape[0]
    u = _rms(h_ref[...], g_ref[...]).astype(BF16)

    kf = jnp.dot(u, wk_ref[...], preferred_element_type=F32)
    kc, ka, kb = kc_ref[...], ka_ref[...], kb_ref[...]
    for j in range(kf.shape[1] // LANES):
        x = kf[:, j * LANES:(j + 1) * LANES]
        y = x * kc + pltpu.roll(x, LANES - ROT_HALF, 1) * ka + pltpu.roll(x, ROT_HALF, 1) * kb
        k_ref[:, j * LANES:(j + 1) * LANES] = y.astype(BF16)

    qf = lax.dot_general(wqT_ref[...], u, _NT, preferred_element_type=F32)
    c = cosT_ref[...] * q_scale
    s = sinT_ref[...] * q_scale
    for unit in range(qf.shape[0] // HEAD_DIM):
        r0 = unit * HEAD_DIM
        t1 = qf[r0:r0 + ROT_HALF]
        t2 = qf[r0 + ROT_HALF:r0 + 2 * ROT_HALF]
        rest = qf[r0 + 2 * ROT_HALF:r0 + HEAD_DIM] * q_scale
        blk = jnp.concatenate([t1 * c - t2 * s, t2 * c + t1 * s, rest], axis=0).astype(BF16)
        for jb in range(tm // qblk):
            qT_ref[jb, r0:r0 + HEAD_DIM, :] = blk[:, jb * qblk:(jb + 1) * qblk]

    vf = lax.dot_general(wvT_ref[...], u, _NT, preferred_element_type=F32).astype(BF16)
    for jb in range(tm // vblk):
        vT_ref[jb] = vf[:, jb * vblk:(jb + 1) * vblk]


def _in_proj(h, g, wqT, wk, wvT, cosT, sinT, kc, ka, kb, *, q_scale, qblk, vblk):
    b, s, d = h.shape
    nq, nk, nv = wqT.shape[0], wk.shape[1], wvT.shape[0]
    tm = ROW_TILE
    kern = functools.partial(_in_proj_kernel, q_scale=q_scale, qblk=qblk, vblk=vblk)
    const = lambda bi, i: (0, 0)
    return pl.pallas_call(
        kern,
        grid=(b, s // tm),
        in_specs=[
            pl.BlockSpec((None, tm, d), lambda bi, i: (bi, i, 0)),
            pl.BlockSpec((1, d), const),
            pl.BlockSpec((nq, d), const),
            pl.BlockSpec((d, nk), const),
            pl.BlockSpec((nv, d), const),
            pl.BlockSpec((None, ROT_HALF, tm), lambda bi, i: (bi, 0, i)),
            pl.BlockSpec((None, ROT_HALF, tm), lambda bi, i: (bi, 0, i)),
            pl.BlockSpec((None, tm, LANES), lambda bi, i: (bi, i, 0)),
            pl.BlockSpec((None, tm, LANES), lambda bi, i: (bi, i, 0)),
            pl.BlockSpec((None, tm, LANES), lambda bi, i: (bi, i, 0)),
        ],
        out_specs=[
            pl.BlockSpec((None, tm // qblk, nq, qblk), lambda bi, i: (bi, i, 0, 0)),
            pl.BlockSpec((None, tm, nk), lambda bi, i: (bi, i, 0)),
            pl.BlockSpec((None, tm // vblk, nv, vblk), lambda bi, i: (bi, i, 0, 0)),
        ],
        out_shape=[
            jax.ShapeDtypeStruct((b, s // qblk, nq, qblk), BF16),
            jax.ShapeDtypeStruct((b, s, nk), BF16),
            jax.ShapeDtypeStruct((b, s // vblk, nv, vblk), BF16),
        ],
        compiler_params=_params("parallel", "parallel"),
        name="mixer_in_proj",
    )(h, g, wqT, wk, wvT, cosT, sinT, kc, ka, kb)


def _stage_queries(qT_ref, qz_ref, t):
    q = qT_ref[...]
    row = lax.broadcasted_iota(jnp.int32, q.shape, 0)
    zero = jnp.zeros_like(q)
    qz_ref[:, 0:t] = jnp.where(row < HEAD_DIM, q, zero)
    qz_ref[:, t:2 * t] = jnp.where(row >= HEAD_DIM, q, zero)


def _init_stats(m_ref, l_ref, acc_ref):
    m_ref[...] = jnp.full(m_ref.shape, NEG_INF, F32)
    l_ref[...] = jnp.zeros(l_ref.shape, F32)
    acc_ref[...] = jnp.zeros(acc_ref.shape, F32)


def _scores_chunk(k_ref, kb, qz_ref, dst_ref, c, t, cw):
    kblk = k_ref[pl.ds(pl.multiple_of(kb * t, t), t), :]
    cols = slice(c * cw, (c + 1) * cw)
    dst_ref[:, cols] = jnp.dot(kblk, qz_ref[:, cols], preferred_element_type=F32)


def _softmax_pv(s, v, cols, acc_at, m_ref, l_ref):
    m_old = m_ref[:, cols]
    m_new = jnp.maximum(m_old, jnp.max(s, axis=0, keepdims=True))
    alpha = jnp.exp2(m_old - m_new)
    p = jnp.exp2(s - m_new)
    l_ref[:, cols] = alpha * l_ref[:, cols] + jnp.sum(p, axis=0, keepdims=True)
    m_ref[:, cols] = m_new
    acc_at[...] = alpha * acc_at[...] + jnp.dot(v, p.astype(BF16), preferred_element_type=F32)


def _block_step(kb, src_ref, kb_next, dst_ref, refs, *, t, cw, lead, shared_v, bias=None):
    k_ref, vT_ref, qz_ref, m_ref, l_ref, acc_ref = refs
    nchunk = 2 * t // cw
    v = vT_ref[kb]
    if kb_next is not None:
        for c in range(lead):
            _scores_chunk(k_ref, kb_next, qz_ref, dst_ref, c, t, cw)
    for c in range(nchunk):
        if kb_next is not None and c + lead < nchunk:
            _scores_chunk(k_ref, kb_next, qz_ref, dst_ref, c + lead, t, cw)
        u, cc = divmod(c, t // cw)
        s = src_ref[:, c * cw:(c + 1) * cw]
        if bias is not None:
            s = s + bias[:, cc * cw:(cc + 1) * cw]
        vu = v if shared_v else v[u * HEAD_DIM:(u + 1) * HEAD_DIM]
        _softmax_pv(s, vu, slice(c * cw, (c + 1) * cw), acc_ref.at[u, :, cc * cw:(cc + 1) * cw], m_ref, l_ref)


def _band_attn_kernel(*refs, t, cw, lead, per_trip, width, nkb, has_sink):
    if has_sink:
        sink_ref, bias_ref, qT_ref, k_ref, vT_ref, oT_ref, qz_ref, m_ref, l_ref, acc_ref, sa_ref, sb_ref = refs
    else:
        bias_ref, qT_ref, k_ref, vT_ref, oT_ref, qz_ref, m_ref, l_ref, acc_ref, sa_ref, sb_ref = refs
    pair = pl.program_id(1)
    qb = pl.program_id(2)
    _stage_queries(qT_ref, qz_ref, t)
    _init_stats(m_ref, l_ref, acc_ref)
    nblk = 2 * width + 1
    bufs = (sa_ref, sb_ref)
    step_refs = (k_ref, vT_ref, qz_ref, m_ref, l_ref, acc_ref)

    def block(d):
        kb = qb + d
        valid = jnp.logical_and(kb >= 0, kb < nkb)
        return jnp.clip(kb, 0, nkb - 1), jnp.where(valid, d + width, nblk)

    def step(d, parity, last):
        kb, bidx = block(d)
        _block_step(kb, bufs[parity], None if last else block(d + 1)[0], bufs[1 - parity], step_refs,
                    t=t, cw=cw, lead=lead, shared_v=False, bias=bias_ref[bidx])

    for c in range(2 * t // cw):
        _scores_chunk(k_ref, block(-width)[0], qz_ref, bufs[0], c, t, cw)

    def trip(j, carry):
        for i in range(per_trip):
            step(per_trip * j + i - width, i % 2, False)
        return carry

    ntrip = (nblk - 1) // per_trip
    lax.fori_loop(0, ntrip, trip, 0)
    for n in range(ntrip * per_trip, nblk):
        step(n - width, n % 2, n == nblk - 1)

    for u in range(2):
        m = m_ref[:, u * t:(u + 1) * t]
        l = l_ref[:, u * t:(u + 1) * t]
        acc = acc_ref[u]
        if has_sink:
            sk = sink_ref[pair * 2 + u]
            m2 = jnp.maximum(m, sk)
            w = jnp.exp2(m - m2)
            l = l * w + jnp.exp2(sk - m2)
            acc = acc * w
        oT_ref[u * HEAD_DIM:(u + 1) * HEAD_DIM, :] = (acc / l).astype(BF16)


def _band_attention(qT, k, vT, bias, sink, *, width, shared_kv):
    b, nqb, nq, t = qT.shape
    s = k.shape[1]
    nkb = s // t
    npairs = nq // (2 * HEAD_DIM)
    kv_idx = (lambda p: 0) if shared_kv else (lambda p: p)
    has_sink = sink is not None
    kern = functools.partial(_band_attn_kernel, t=t, cw=min(t, ATTN_CHUNK), lead=ATTN_LEAD,
                             per_trip=BAND_PER_TRIP, width=width, nkb=nkb, has_sink=has_sink)
    in_specs = [
        pl.BlockSpec(bias.shape, lambda bi, p, i: (0, 0, 0)),
        pl.BlockSpec((None, None, 2 * HEAD_DIM, t), lambda bi, p, i: (bi, i, p, 0)),
        pl.BlockSpec((None, s, 2 * HEAD_DIM), lambda bi, p, i: (bi, 0, kv_idx(p))),
        pl.BlockSpec((None, nkb, 2 * HEAD_DIM, t), lambda bi, p, i: (bi, 0, kv_idx(p), 0)),
    ]
    args = [bias, qT, k, vT]
    if has_sink:
        in_specs = [pl.BlockSpec(memory_space=pltpu.SMEM)] + in_specs
        args = [sink] + args
    return pl.pallas_call(
        kern,
        grid=(b, npairs, nqb),
        in_specs=in_specs,
        out_specs=pl.BlockSpec((None, None, 2 * HEAD_DIM, t), lambda bi, p, i: (bi, i, p, 0)),
        out_shape=jax.ShapeDtypeStruct((b, nqb, nq, t), BF16),
        scratch_shapes=[
            pltpu.VMEM((2 * HEAD_DIM, 2 * t), BF16),
            pltpu.VMEM((1, 2 * t), F32),
            pltpu.VMEM((1, 2 * t), F32),
            pltpu.VMEM((2, HEAD_DIM, t), F32),
            pltpu.VMEM((t, 2 * t), F32),
            pltpu.VMEM((t, 2 * t), F32),
        ],
        compiler_params=_params("parallel", "parallel", "arbitrary"),
        name="band_attention",
    )(*args)


def _diff_attn_kernel(lam_ref, subg_ref, qT_ref, k_ref, vT_ref, oT_ref, qz_ref, m_ref, l_ref, acc_ref,
                      sa_ref, sb_ref, *, t, nkb, cw, lead, per_trip, lam_init):
    _stage_queries(qT_ref, qz_ref, t)
    _init_stats(m_ref, l_ref, acc_ref)

    bufs = (sa_ref, sb_ref)
    step = functools.partial(_block_step, refs=(k_ref, vT_ref, qz_ref, m_ref, l_ref, acc_ref),
                             t=t, cw=cw, lead=lead, shared_v=True)

    for c in range(2 * t // cw):
        _scores_chunk(k_ref, 0, qz_ref, sa_ref, c, t, cw)

    def trip(j, carry):
        for i in range(per_trip):
            step(per_trip * j + i, bufs[i % 2], per_trip * j + i + 1, bufs[(i + 1) % 2])
        return carry

    ntrip = nkb // per_trip - 1
    lax.fori_loop(0, ntrip, trip, 0)
    for i in range(per_trip):
        kb = ntrip * per_trip + i
        step(kb, bufs[i % 2], kb + 1 if i + 1 < per_trip else None, bufs[(i + 1) % 2])

    lv = lam_ref[...]
    e1 = jnp.exp(jnp.sum(lv[0:1] * lv[1:2], axis=-1, keepdims=True))
    e2 = jnp.exp(jnp.sum(lv[2:3] * lv[3:4], axis=-1, keepdims=True))
    lam = e1 - e2 + lam_init
    o = acc_ref[0] / l_ref[:, 0:t] - lam * (acc_ref[1] / l_ref[:, t:2 * t])
    ms = jnp.mean(o * o, axis=0, keepdims=True)
    o = o * lax.rsqrt(ms + EPS) * subg_ref[...] * (1.0 - lam_init)
    oT_ref[...] = o.astype(BF16)


def _diff_attention(qT, k, vT, lamv, subg, *, lam_init):
    b, nqb, nq, t = qT.shape
    s = k.shape[1]
    nkb = s // t
    heads = nq // (2 * HEAD_DIM)
    assert nkb % DENSE_PER_TRIP == 0 and DENSE_PER_TRIP % 2 == 0
    kern = functools.partial(_diff_attn_kernel, t=t, nkb=nkb, cw=ATTN_CHUNK, lead=ATTN_LEAD,
                             per_trip=DENSE_PER_TRIP, lam_init=lam_init)
    return pl.pallas_call(
        kern,
        grid=(b, heads, nqb),
        in_specs=[
            pl.BlockSpec(lamv.shape, lambda bi, h, i: (0, 0)),
            pl.BlockSpec(subg.shape, lambda bi, h, i: (0, 0)),
            pl.BlockSpec((None, None, 2 * HEAD_DIM, t), lambda bi, h, i: (bi, i, h, 0)),
            pl.BlockSpec((None, s, 2 * HEAD_DIM), lambda bi, h, i: (bi, 0, h)),
            pl.BlockSpec((None, nkb, 2 * HEAD_DIM, t), lambda bi, h, i: (bi, 0, h, 0)),
        ],
        out_specs=pl.BlockSpec((None, None, 2 * HEAD_DIM, t), lambda bi, h, i: (bi, i, h, 0)),
        out_shape=jax.ShapeDtypeStruct((b, nqb, nq, t), BF16),
        scratch_shapes=[
            pltpu.VMEM((2 * HEAD_DIM, 2 * t), BF16),
            pltpu.VMEM((1, 2 * t), F32),
            pltpu.VMEM((1, 2 * t), F32),
            pltpu.VMEM((2, 2 * HEAD_DIM, t), F32),
            pltpu.VMEM((t, 2 * t), F32),
            pltpu.VMEM((t, 2 * t), F32),
        ],
        compiler_params=_params("parallel", "parallel", "arbitrary"),
        name="diff_attention",
    )(lamv, subg, qT, k, vT)


def _in_proj_tm_kernel(h_ref, g_ref, w_ref, kc_ref, ka_ref, kb_ref, q_ref, k_ref, v_ref, *, q_scale):
    u = _rms(h_ref[...], g_ref[...]).astype(BF16)
    y = jnp.dot(u, w_ref[...], preferred_element_type=F32)
    kc, ka, kb = kc_ref[...], ka_ref[...], kb_ref[...]
    nq, nk = q_ref.shape[1], k_ref.shape[1]

    def rope(j):
        x = y[:, j * LANES:(j + 1) * LANES]
        return x * kc + pltpu.roll(x, LANES - ROT_HALF, 1) * ka + pltpu.roll(x, ROT_HALF, 1) * kb

    for j in range(nq // LANES):
        q_ref[:, j * LANES:(j + 1) * LANES] = (rope(j) * q_scale).astype(BF16)
    for j in range(nk // LANES):
        k_ref[:, j * LANES:(j + 1) * LANES] = rope(nq // LANES + j).astype(BF16)
    v_ref[...] = y[:, nq + nk:].astype(BF16)


def _in_proj_tm(h, g, w, kc, ka, kb, *, nq, nk, q_scale):
    b, s, d = h.shape
    nv = w.shape[1] - nq - nk
    tm = ROW_TILE
    const = lambda bi, i: (0, 0)
    row = lambda bi, i: (bi, i, 0)
    return pl.pallas_call(
        functools.partial(_in_proj_tm_kernel, q_scale=q_scale),
        grid=(b, s // tm),
        in_specs=[
            pl.BlockSpec((None, tm, d), row),
            pl.BlockSpec((1, d), const),
            pl.BlockSpec(w.shape, const),
            pl.BlockSpec((None, tm, LANES), row),
            pl.BlockSpec((None, tm, LANES), row),
            pl.BlockSpec((None, tm, LANES), row),
        ],
        out_specs=[pl.BlockSpec((None, tm, n), row) for n in (nq, nk, nv)],
        out_shape=[jax.ShapeDtypeStruct((b, s, n), BF16) for n in (nq, nk, nv)],
        compiler_params=_params("parallel", "parallel"),
        name="mixer_in_proj_tm",
    )(h, g, w, kc, ka, kb)


def _window_attn_kernel(*refs, radius, per_trip, has_sink):
    if has_sink:
        sink_ref, refs = refs[0], refs[1:]
    bias_ref, q_ref, k_ref, v_ref, o_ref, lse_ref, *bufs = refs
    pair = pl.program_id(2)
    seq = q_ref.shape[0]
    win = WIN_Q + 2 * radius
    nqb = seq // WIN_Q
    lane = lax.broadcasted_iota(jnp.int32, (WIN_Q, 2 * HEAD_DIM), 1)

    def window(i):
        q0 = i * WIN_Q
        k0 = jnp.clip(q0 - radius, 0, seq - win)
        return pl.multiple_of(q0, WIN_Q), pl.multiple_of(k0, radius), (q0 - k0) // radius

    def scores(i, dst_ref):
        q0, k0, _ = window(i)
        q = q_ref[pl.ds(q0, WIN_Q), :]
        zero = jnp.zeros_like(q)
        qz = jnp.concatenate([jnp.where(lane < HEAD_DIM, q, zero), jnp.where(lane >= HEAD_DIM, q, zero)],
                             axis=0)
        dst_ref[...] = lax.dot_general(k_ref[pl.ds(k0, win), :], qz, _NT, preferred_element_type=F32)

    def finish(i, src_ref):
        q0, k0, bidx = window(i)
        bias = bias_ref[bidx]
        s = src_ref[...] + jnp.concatenate([bias, bias], axis=1)
        m = jnp.max(s, axis=0, keepdims=True)
        if has_sink:
            unit = lax.broadcasted_iota(jnp.int32, m.shape, 1) // WIN_Q
            sk = jnp.where(unit == 0, sink_ref[pair * 2], sink_ref[pair * 2 + 1])
            m = jnp.maximum(m, sk)
        p = jnp.exp2(s - m)
        l = jnp.sum(p, axis=0, keepdims=True)
        if has_sink:
            l = l + jnp.exp2(sk - m)
        oT = lax.dot_general(v_ref[pl.ds(k0, win), :], p.astype(BF16), _TN, preferred_element_type=F32)
        oT = oT / l
        lse = m + jnp.log2(l)
        halves = []
        for u in range(2):
            halves.append(oT[u * HEAD_DIM:(u + 1) * HEAD_DIM, u * WIN_Q:(u + 1) * WIN_Q])
            lse_ref[u, pl.ds(i, 1), :] = lse[:, u * WIN_Q:(u + 1) * WIN_Q]
        o_ref[pl.ds(q0, WIN_Q), :] = jnp.concatenate(halves, axis=0).T.astype(BF16)

    depth = len(bufs)
    for n in range(depth - 1):
        scores(n, bufs[n])

    def trip(j, carry):
        for n in range(per_trip):
            i = per_trip * j + n
            scores(i + depth - 1, bufs[(n + depth - 1) % depth])
            finish(i, bufs[n % depth])
        return carry

    ntrip = nqb // per_trip - 1
    lax.fori_loop(0, ntrip, trip, 0)
    for n in range(per_trip):
        i = ntrip * per_trip + n
        if n + depth - 1 < per_trip:
            scores(i + depth - 1, bufs[(n + depth - 1) % depth])
        finish(i, bufs[n % depth])


def _window_bias(radius):
    win = WIN_Q + 2 * radius
    i = np.arange(win)[:, None]
    j = np.arange(WIN_Q)[None, :]
    return jnp.asarray(np.stack([np.where(np.abs(i - j - b * radius) <= radius, 0.0, NEG_INF)
                                 for b in range(3)]), F32)


def _window_attention(q, k, v, sink, *, radius, dil):
    b, s, nq = q.shape
    seq = s // dil
    nqb = seq // WIN_Q
    npairs = nq // LANES
    win = WIN_Q + 2 * radius
    shared = k.shape[2] == LANES and npairs > 1
    has_sink = sink is not None
    per_trip = min(WIN_PER_TRIP, nqb)
    depth = min(WIN_DEPTH, per_trip)
    assert seq % WIN_Q == 0 and seq >= win and WIN_Q % radius == 0
    assert nqb % per_trip == 0 and (per_trip % depth == 0 or per_trip == nqb)
    view = lambda a: a.reshape(b, seq, dil * a.shape[2])
    qspec = pl.BlockSpec((None, seq, LANES), lambda bi, r, p: (bi, 0, r * npairs + p))
    kvspec = pl.BlockSpec((None, seq, LANES), lambda bi, r, p: (bi, 0, r)) if shared else qspec
    bias = _window_bias(radius)
    in_specs = [pl.BlockSpec(bias.shape, lambda bi, r, p: (0, 0, 0)), qspec, kvspec, kvspec]
    args = [bias, view(q), view(k), view(v)]
    if has_sink:
        in_specs = [pl.BlockSpec(memory_space=pltpu.SMEM)] + in_specs
        args = [sink] + args
    o, lse = pl.pallas_call(
        functools.partial(_window_attn_kernel, radius=radius, per_trip=per_trip, has_sink=has_sink),
        grid=(b, dil, npairs),
        in_specs=in_specs,
        out_specs=[qspec,
                   pl.BlockSpec((None, None, None, 2, nqb, WIN_Q), lambda bi, r, p: (bi, r, p, 0, 0, 0))],
        out_shape=[jax.ShapeDtypeStruct((b, seq, dil * nq), BF16),
                   jax.ShapeDtypeStruct((b, dil, npairs, 2, nqb, WIN_Q), F32)],
        scratch_shapes=[pltpu.VMEM((win, 2 * WIN_Q), F32)] * depth,
        compiler_params=_params("parallel", "parallel", "parallel"),
        name="window_attention",
    )(*args)
    lse = lse.reshape(b, dil, 2 * npairs, seq).transpose(0, 3, 1, 2).reshape(b, s, 2 * npairs)
    return o.reshape(b, s, nq), lse


def _mixer_out(mixer_refs, wout_ref, mode):
    if mode == "feature_major":
        (oT_ref,) = mixer_refs
        ys = [lax.dot_general(oT_ref[j], wout_ref[...], _TN, preferred_element_type=F32)
              for j in range(oT_ref.shape[0])]
        return jnp.concatenate(ys, axis=0) if len(ys) > 1 else ys[0]
    if mode == "token_major":
        (o_ref,) = mixer_refs
        return jnp.dot(o_ref[...], wout_ref[...], preferred_element_type=F32)
    n = (len(mixer_refs) - 1) // 2
    o_refs, lse_refs, expand_ref = mixer_refs[:n], mixer_refs[n:2 * n], mixer_refs[2 * n]
    lses = [r[...] for r in lse_refs]
    top = functools.reduce(jnp.maximum, lses)
    es = [jnp.exp2(x - top) for x in lses]
    z = functools.reduce(jnp.add, es)
    o = None
    for e, o_ref in zip(es, o_refs):
        w = e / z
        hi = w.astype(BF16)
        lo = (w - hi.astype(F32)).astype(BF16)
        wide = jnp.dot(jnp.concatenate([hi, lo], axis=1), expand_ref[...], preferred_element_type=F32)
        term = wide * o_ref[...].astype(F32)
        o = term if o is None else o + term
    return jnp.dot(o.astype(BF16), wout_ref[...], preferred_element_type=F32)


def _mid_kernel(*refs, mode, n_mixer, x_scale):
    h_ref = refs[0]
    mixer_refs = refs[1:1 + n_mixer]
    wout_ref, gmix_ref, gpre_ref, wq_ref, kT_ref, v_ref, wo_ref, gpost_ref, out_ref = refs[1 + n_mixer:]
    y = _mixer_out(mixer_refs, wout_ref, mode)
    h1 = h_ref[...] + _rms(y, gmix_ref[...])

    u = _rms(h1, gpre_ref[...]).astype(BF16)
    q = (jnp.dot(u, wq_ref[...], preferred_element_type=F32) * x_scale).astype(BF16)
    xd = q.shape[1] // X_HEADS
    outs = []
    for hd in range(X_HEADS):
        s = jnp.dot(q[:, hd * xd:(hd + 1) * xd], kT_ref[hd * xd:(hd + 1) * xd, :],
                    preferred_element_type=F32)
        p = jnp.exp2(s - jnp.max(s, axis=-1, keepdims=True))
        l = jnp.sum(p, axis=-1, keepdims=True)
        o = jnp.dot(p.astype(BF16), v_ref[:, hd * xd:(hd + 1) * xd], preferred_element_type=F32)
        outs.append((o / l).astype(BF16))
    y2 = jnp.dot(jnp.concatenate(outs, axis=1), wo_ref[...], preferred_element_type=F32)
    out_ref[...] = h1 + _rms(y2, gpost_ref[...])


def _mid(h, mixer, mode, wout, gmix, gpre, wq, kT, v, wo, gpost, *, x_scale):
    b, s, d = h.shape
    tm = ROW_TILE
    n_mem = v.shape[1]
    const = lambda bi, i: (0, 0)
    row = lambda bi, i: (bi, i, 0)
    if mode == "feature_major":
        oblk = mixer[0].shape[3]
        mixer_specs = [pl.BlockSpec((None, tm // oblk, d, oblk), lambda bi, i: (bi, i, 0, 0))]
    else:
        mixer_specs = [pl.BlockSpec((None, tm, a.shape[2]), row) for a in mixer]
    if mode == "branches":
        heads = mixer[-1].shape[2]
        expand = jnp.asarray(np.tile(np.repeat(np.eye(heads), d // heads, axis=1), (2, 1)), BF16)
        mixer = list(mixer) + [expand]
        mixer_specs.append(pl.BlockSpec(expand.shape, const))
    kern = functools.partial(_mid_kernel, mode=mode, n_mixer=len(mixer), x_scale=x_scale)
    return pl.pallas_call(
        kern,
        grid=(b, s // tm),
        in_specs=[
            pl.BlockSpec((None, tm, d), row),
            *mixer_specs,
            pl.BlockSpec((d, d), const),
            pl.BlockSpec((1, d), const),
            pl.BlockSpec((1, d), const),
            pl.BlockSpec((d, d), const),
            pl.BlockSpec((None, d, n_mem), lambda bi, i: (bi, 0, 0)),
            pl.BlockSpec((None, n_mem, d), lambda bi, i: (bi, 0, 0)),
            pl.BlockSpec((d, d), const),
            pl.BlockSpec((1, d), const),
        ],
        out_specs=pl.BlockSpec((None, tm, d), lambda bi, i: (bi, i, 0)),
        out_shape=jax.ShapeDtypeStruct((b, s, d), F32),
        compiler_params=_params("parallel", "parallel"),
        name="out_proj_cross_attention",
    )(h, *mixer, wout, gmix, gpre, wq, kT, v, wo, gpost)


def _mem_kv_kernel(mem_ref, g_ref, wkT_ref, wv_ref, kT_ref, v_ref):
    mn = _rms(mem_ref[...], g_ref[...]).astype(BF16)
    kT_ref[...] = lax.dot_general(wkT_ref[...], mn, _NT, preferred_element_type=F32).astype(BF16)
    v_ref[...] = jnp.dot(mn, wv_ref[...], preferred_element_type=F32).astype(BF16)


def _mem_kv(mem, g, wkT, wv):
    depth, d = g.shape[0], g.shape[2]
    b, n_mem, _ = mem.shape
    return pl.pallas_call(
        _mem_kv_kernel,
        grid=(depth, b),
        in_specs=[
            pl.BlockSpec((None, n_mem, d), lambda li, bi: (bi, 0, 0)),
            pl.BlockSpec((None, 1, d), lambda li, bi: (li, 0, 0)),
            pl.BlockSpec((None, d, d), lambda li, bi: (li, 0, 0)),
            pl.BlockSpec((None, d, d), lambda li, bi: (li, 0, 0)),
        ],
        out_specs=[
            pl.BlockSpec((None, None, d, n_mem), lambda li, bi: (li, bi, 0, 0)),
            pl.BlockSpec((None, None, n_mem, d), lambda li, bi: (li, bi, 0, 0)),
        ],
        out_shape=[
            jax.ShapeDtypeStruct((depth, b, d, n_mem), BF16),
            jax.ShapeDtypeStruct((depth, b, n_mem, d), BF16),
        ],
        compiler_params=_params("parallel", "parallel"),
        name="memory_kv",
    )(mem, g, wkT, wv)


def _ffn_kernel(h_ref, gpre_ref, wg_ref, wu_ref, wd_ref, gpost_ref, out_ref, u_ref, acc_ref):
    j = pl.program_id(1)

    @pl.when(j == 0)
    def _():
        u_ref[...] = _rms(h_ref[...], gpre_ref[...]).astype(BF16)
        acc_ref[...] = jnp.zeros(acc_ref.shape, F32)

    u = u_ref[...]
    g = jnp.dot(u, wg_ref[...], preferred_element_type=F32)
    up = jnp.dot(u, wu_ref[...], preferred_element_type=F32)
    a = (g / (1.0 + jnp.exp(-g)) * up).astype(BF16)
    acc_ref[...] += jnp.dot(a, wd_ref[...], preferred_element_type=F32)

    @pl.when(j == pl.num_programs(1) - 1)
    def _():
        out_ref[...] = h_ref[...] + _rms(acc_ref[...], gpost_ref[...])


def _ffn(h, gpre, wgu, wd, gpost):
    b, s, d = h.shape
    tm = ROW_TILE
    dff = wd.shape[0]
    fc = dff // FF_CHUNKS
    rows = b * s
    h2 = h.reshape(rows, d)
    out = pl.pallas_call(
        _ffn_kernel,
        grid=(rows // tm, FF_CHUNKS),
        in_specs=[
            pl.BlockSpec((tm, d), lambda i, j: (i, 0)),
            pl.BlockSpec((1, d), lambda i, j: (0, 0)),
            pl.BlockSpec((d, fc), lambda i, j: (0, j)),
            pl.BlockSpec((d, fc), lambda i, j: (0, FF_CHUNKS + j)),
            pl.BlockSpec((fc, d), lambda i, j: (j, 0)),
            pl.BlockSpec((1, d), lambda i, j: (0, 0)),
        ],
        out_specs=pl.BlockSpec((tm, d), lambda i, j: (i, 0)),
        out_shape=jax.ShapeDtypeStruct((rows, d), F32),
        scratch_shapes=[pltpu.VMEM((tm, d), BF16), pltpu.VMEM((tm, d), F32)],
        compiler_params=_params("parallel", "arbitrary"),
        name="swiglu_ffn",
    )(h2, gpre, wgu, wgu, wd, gpost)
    return out.reshape(b, s, d)


def _rope_tables(positions):
    inv_freq = ROPE_THETA ** (-jnp.arange(0, 2 * ROT_HALF, 2, dtype=F32) / (2 * ROT_HALF))
    ang = positions.astype(F32)[..., None] * inv_freq
    cos, sin = jnp.cos(ang), jnp.sin(ang)
    cosT, sinT = cos.transpose(0, 2, 1), sin.transpose(0, 2, 1)
    zeros = jnp.zeros_like(cos)
    pad = HEAD_DIM - 2 * ROT_HALF
    ones_tail = jnp.ones(cos.shape[:-1] + (pad,), F32)
    zero_tail = jnp.zeros(cos.shape[:-1] + (pad,), F32)
    reps = LANES // HEAD_DIM
    kc = jnp.tile(jnp.concatenate([cos, cos, ones_tail], axis=-1), reps)
    ka = jnp.tile(jnp.concatenate([-sin, zeros, zero_tail], axis=-1), reps)
    kb = jnp.tile(jnp.concatenate([zeros, sin, zero_tail], axis=-1), reps)
    return cosT, sinT, kc, ka, kb


def _band_bias(t, width, multiplicity):
    i = np.arange(t)[:, None]
    j = np.arange(t)[None, :]
    tiles = []
    for d in range(-width, width + 1):
        c = multiplicity(d * t + i - j)
        tiles.append(np.where(c > 0, np.log2(np.maximum(c, 1)), NEG_INF))
    tiles.append(np.full((t, t), NEG_INF))
    return jnp.asarray(np.stack(tiles), F32)


def _dilated_multiplicity(delta):
    c = np.zeros(delta.shape, np.int64)
    for window, dil in A_PATTERNS:
        c += (delta % dil == 0) & (np.abs(delta) <= (window // (2 * dil)) * dil)
    return c


def _window_multiplicity(delta):
    return (np.abs(delta) <= C_RADIUS).astype(np.int64)


def _row(g):
    return g.reshape(1, -1)


def kernel(x, mem, positions, mix_pre_g, mix_post_g, mem_pre_g, mem_kv_g, mem_post_g, ffn_pre_g, ffn_post_g,
           a_w_in, a_w_out, b_w_in, b_w_out, b_lam_q1, b_lam_k1, b_lam_q2, b_lam_k2, b_sub_g, c_w_in, c_w_out,
           c_sink, x_wq, x_wkv, x_wo, w_gate_up, w_down):
    depth, d = mix_pre_g.shape
    assert d % (2 * HEAD_DIM) == 0 and x.shape[1] % DENSE_BLOCK == 0 and x.shape[1] % ROW_TILE == 0
    cosT, sinT, kc, ka, kb = _rope_tables(positions)
    q_scale = HEAD_DIM ** -0.5 * LOG2E
    x_scale = (d // X_HEADS) ** -0.5 * LOG2E

    mem_kT, mem_v = _mem_kv(mem, mem_kv_g.reshape(depth, 1, d),
                            x_wkv[:, :, :d].transpose(0, 2, 1).astype(BF16), x_wkv[:, :, d:].astype(BF16))

    h = x
    for i in range(depth):
        kind, j = i % N_MIXERS, i // N_MIXERS
        g_pre = _row(mix_pre_g[i])
        if kind == 0:
            w_in, w_out = a_w_in[j], a_w_out[j]
            q, k, v = _in_proj_tm(h, g_pre, w_in.astype(BF16), kc, ka, kb, nq=d, nk=d, q_scale=q_scale)
            branches = [_window_attention(q, k, v, None, radius=window // (2 * dil), dil=dil)
                        for window, dil in A_PATTERNS]
            mixer, mode = [o for o, _ in branches] + [lse for _, lse in branches], "branches"
        elif kind == 1:
            w_in, w_out = b_w_in[j], b_w_out[j]
            wq, wk, wv = w_in[:, :d], w_in[:, d:2 * d], w_in[:, 2 * d:]
            qT, k, vT = _in_proj(h, g_pre, wq.T.astype(BF16), wk.astype(BF16), wv.T.astype(BF16),
                                 cosT, sinT, kc, ka, kb, q_scale=q_scale, qblk=DENSE_BLOCK, vblk=DENSE_BLOCK)
            lam_init = 0.8 - 0.6 * math.exp(-0.3 * i)
            lamv = jnp.stack([b_lam_q1[j], b_lam_k1[j], b_lam_q2[j], b_lam_k2[j]]).astype(F32)
            subg = jnp.broadcast_to(b_sub_g[j].astype(F32)[:, None], (2 * HEAD_DIM, DENSE_BLOCK))
            mixer, mode = [_diff_attention(qT, k, vT, lamv, subg, lam_init=lam_init)], "feature_major"
        else:
            w_in, w_out = c_w_in[j], c_w_out[j]
            n_kv = (w_in.shape[1] - d) // (2 * HEAD_DIM)
            grp = (d // HEAD_DIM) // n_kv
            perm = np.arange(d).reshape(n_kv, grp, HEAD_DIM).transpose(1, 0, 2).reshape(-1)
            w_in = jnp.concatenate([w_in[:, :d][:, perm], w_in[:, d:]], axis=1)
            w_out = w_out[perm, :]
            q, k, v = _in_proj_tm(h, g_pre, w_in.astype(BF16), kc, ka, kb, nq=d, nk=n_kv * HEAD_DIM,
                                  q_scale=q_scale)
            sink = (c_sink[j].astype(F32) * LOG2E)[perm[::HEAD_DIM] // HEAD_DIM]
            mixer, mode = [_window_attention(q, k, v, sink, radius=C_RADIUS, dil=1)[0]], "token_major"
        h = _mid(h, mixer, mode, w_out.astype(BF16), _row(mix_post_g[i]), _row(mem_pre_g[i]),
                 x_wq[i].astype(BF16), mem_kT[i], mem_v[i], x_wo[i].astype(BF16), _row(mem_post_g[i]),
                 x_scale=x_scale)
        h = _ffn(h, _row(ffn_pre_g[i]), w_gate_up[i].astype(BF16), w_down[i].astype(BF16), _row(ffn_post_g[i]))
    return h
```

```python
import functools
import math

import jax
import jax.numpy as jnp
import numpy as np
from jax import lax
from jax.experimental import pallas as pl
from jax.experimental.pallas import tpu as pltpu

F32 = jnp.float32
BF16 = jnp.bfloat16

HEAD_DIM = 64
ROT_HALF = HEAD_DIM // 8
ROPE_THETA = 500000.0
EPS = 1e-6
NEG_INF = -1e30
LOG2E = 1.4426950408889634
N_MIXERS = 3

A_PATTERNS = ((128, 1), (512, 4), (2048, 16))
C_RADIUS = 128
X_HEADS = 4

LANES = 128
ROW_TILE = 512
WINDOW_BLOCK = 256
DILATED_BLOCK = 512
DENSE_BLOCK = 512
ATTN_CHUNK = 256
ATTN_LEAD = 1
DENSE_PER_TRIP = 4
BAND_PER_TRIP = 8
WIN_Q = 128
WIN_PER_TRIP = 4
WIN_DEPTH = 4
FF_CHUNKS = 2
VMEM_LIMIT = 56 * 1024 * 1024

_NT = (((1,), (1,)), ((), ()))
_TN = (((0,), (0,)), ((), ()))


def _params(*sem):
    return pltpu.CompilerParams(dimension_semantics=sem, vmem_limit_bytes=VMEM_LIMIT)


def _rms(x, g):
    ms = jnp.mean(x * x, axis=-1, keepdims=True)
    return x * lax.rsqrt(ms + EPS) * g


def _in_proj_kernel(h_ref, g_ref, wqT_ref, wk_ref, wvT_ref, cosT_ref, sinT_ref, kc_ref, ka_ref, kb_ref,
                    qT_ref, k_ref, vT_ref, *, q_scale, qblk, vblk):
    tm = h_ref.shape[0]
    u = _rms(h_ref[...], g_ref[...]).astype(BF16)

    kf = jnp.dot(u, wk_ref[...], preferred_element_type=F32)
    kc, ka, kb = kc_ref[...], ka_ref[...], kb_ref[...]
    for j in range(kf.shape[1] // LANES):
        x = kf[:, j * LANES:(j + 1) * LANES]
        y = x * kc + pltpu.roll(x, LANES - ROT_HALF, 1) * ka + pltpu.roll(x, ROT_HALF, 1) * kb
        k_ref[:, j * LANES:(j + 1) * LANES] = y.astype(BF16)

    qf = lax.dot_general(wqT_ref[...], u, _NT, preferred_element_type=F32)
    c = cosT_ref[...] * q_scale
    s = sinT_ref[...] * q_scale
    for unit in range(qf.shape[0] // HEAD_DIM):
        r0 = unit * HEAD_DIM
        t1 = qf[r0:r0 + ROT_HALF]
        t2 = qf[r0 + ROT_HALF:r0 + 2 * ROT_HALF]
        rest = qf[r0 + 2 * ROT_HALF:r0 + HEAD_DIM] * q_scale
        blk = jnp.concatenate([t1 * c - t2 * s, t2 * c + t1 * s, rest], axis=0).astype(BF16)
        for jb in range(tm // qblk):
            qT_ref[jb, r0:r0 + HEAD_DIM, :] = blk[:, jb * qblk:(jb + 1) * qblk]

    vf = lax.dot_general(wvT_ref[...], u, _NT, preferred_element_type=F32).astype(BF16)
    for jb in range(tm // vblk):
        vT_ref[jb] = vf[:, jb * vblk:(jb + 1) * vblk]


def _in_proj(h, g, wqT, wk, wvT, cosT, sinT, kc, ka, kb, *, q_scale, qblk, vblk):
    b, s, d = h.shape
    nq, nk, nv = wqT.shape[0], wk.shape[1], wvT.shape[0]
    tm = ROW_TILE
    kern = functools.partial(_in_proj_kernel, q_scale=q_scale, qblk=qblk, vblk=vblk)
    const = lambda bi, i: (0, 0)
    return pl.pallas_call(
        kern,
        grid=(b, s // tm),
        in_specs=[
            pl.BlockSpec((None, tm, d), lambda bi, i: (bi, i, 0)),
            pl.BlockSpec((1, d), const),
            pl.BlockSpec((nq, d), const),
            pl.BlockSpec((d, nk), const),
            pl.BlockSpec((nv, d), const),
            pl.BlockSpec((None, ROT_HALF, tm), lambda bi, i: (bi, 0, i)),
            pl.BlockSpec((None, ROT_HALF, tm), lambda bi, i: (bi, 0, i)),
            pl.BlockSpec((None, tm, LANES), lambda bi, i: (bi, i, 0)),
            pl.BlockSpec((None, tm, LANES), lambda bi, i: (bi, i, 0)),
            pl.BlockSpec((None, tm, LANES), lambda bi, i: (bi, i, 0)),
        ],
        out_specs=[
            pl.BlockSpec((None, tm // qblk, nq, qblk), lambda bi, i: (bi, i, 0, 0)),
            pl.BlockSpec((None, tm, nk), lambda bi, i: (bi, i, 0)),
            pl.BlockSpec((None, tm // vblk, nv, vblk), lambda bi, i: (bi, i, 0, 0)),
        ],
        out_shape=[
            jax.ShapeDtypeStruct((b, s // qblk, nq, qblk), BF16),
            jax.ShapeDtypeStruct((b, s, nk), BF16),
            jax.ShapeDtypeStruct((b, s // vblk, nv, vblk), BF16),
        ],
        compiler_params=_params("parallel", "parallel"),
        name="mixer_in_proj",
    )(h, g, wqT, wk, wvT, cosT, sinT, kc, ka, kb)


def _stage_queries(qT_ref, qz_ref, t):
    q = qT_ref[...]
    row = lax.broadcasted_iota(jnp.int32, q.shape, 0)
    zero = jnp.zeros_like(q)
    qz_ref[:, 0:t] = jnp.where(row < HEAD_DIM, q, zero)
    qz_ref[:, t:2 * t] = jnp.where(row >= HEAD_DIM, q, zero)


def _init_stats(m_ref, l_ref, acc_ref):
    m_ref[...] = jnp.full(m_ref.shape, NEG_INF, F32)
    l_ref[...] = jnp.zeros(l_ref.shape, F32)
    acc_ref[...] = jnp.zeros(acc_ref.shape, F32)


def _scores_chunk(k_ref, kb, qz_ref, dst_ref, c, t, cw):
    kblk = k_ref[pl.ds(pl.multiple_of(kb * t, t), t), :]
    cols = slice(c * cw, (c + 1) * cw)
    dst_ref[:, cols] = jnp.dot(kblk, qz_ref[:, cols], preferred_element_type=F32)


def _softmax_pv(s, v, cols, acc_at, m_ref, l_ref):
    m_old = m_ref[:, cols]
    m_new = jnp.maximum(m_old, jnp.max(s, axis=0, keepdims=True))
    alpha = jnp.exp2(m_old - m_new)
    p = jnp.exp2(s - m_new)
    l_ref[:, cols] = alpha * l_ref[:, cols] + jnp.sum(p, axis=0, keepdims=True)
    m_ref[:, cols] = m_new
    acc_at[...] = alpha * acc_at[...] + jnp.dot(v, p.astype(BF16), preferred_element_type=F32)


def _block_step(kb, src_ref, kb_next, dst_ref, refs, *, t, cw, lead, shared_v, bias=None):
    k_ref, vT_ref, qz_ref, m_ref, l_ref, acc_ref = refs
    nchunk = 2 * t // cw
    v = vT_ref[kb]
    if kb_next is not None:
        for c in range(lead):
            _scores_chunk(k_ref, kb_next, qz_ref, dst_ref, c, t, cw)
    for c in range(nchunk):
        if kb_next is not None and c + lead < nchunk:
            _scores_chunk(k_ref, kb_next, qz_ref, dst_ref, c + lead, t, cw)
        u, cc = divmod(c, t // cw)
        s = src_ref[:, c * cw:(c + 1) * cw]
        if bias is not None:
            s = s + bias[:, cc * cw:(cc + 1) * cw]
        vu = v if shared_v else v[u * HEAD_DIM:(u + 1) * HEAD_DIM]
        _softmax_pv(s, vu, slice(c * cw, (c + 1) * cw), acc_ref.at[u, :, cc * cw:(cc + 1) * cw], m_ref, l_ref)


def _band_attn_kernel(*refs, t, cw, lead, per_trip, width, nkb, has_sink):
    if has_sink:
        sink_ref, bias_ref, qT_ref, k_ref, vT_ref, oT_ref, qz_ref, m_ref, l_ref, acc_ref, sa_ref, sb_ref = refs
    else:
        bias_ref, qT_ref, k_ref, vT_ref, oT_ref, qz_ref, m_ref, l_ref, acc_ref, sa_ref, sb_ref = refs
    pair = pl.program_id(1)
    qb = pl.program_id(2)
    _stage_queries(qT_ref, qz_ref, t)
    _init_stats(m_ref, l_ref, acc_ref)
    nblk = 2 * width + 1
    bufs = (sa_ref, sb_ref)
    step_refs = (k_ref, vT_ref, qz_ref, m_ref, l_ref, acc_ref)

    def block(d):
        kb = qb + d
        valid = jnp.logical_and(kb >= 0, kb < nkb)
        return jnp.clip(kb, 0, nkb - 1), jnp.where(valid, d + width, nblk)

    def step(d, parity, last):
        kb, bidx = block(d)
        _block_step(kb, bufs[parity], None if last else block(d + 1)[0], bufs[1 - parity], step_refs,
                    t=t, cw=cw, lead=lead, shared_v=False, bias=bias_ref[bidx])

    for c in range(2 * t // cw):
        _scores_chunk(k_ref, block(-width)[0], qz_ref, bufs[0], c, t, cw)

    def trip(j, carry):
        for i in range(per_trip):
            step(per_trip * j + i - width, i % 2, False)
        return carry

    ntrip = (nblk - 1) // per_trip
    lax.fori_loop(0, ntrip, trip, 0)
    for n in range(ntrip * per_trip, nblk):
        step(n - width, n % 2, n == nblk - 1)

    for u in range(2):
        m = m_ref[:, u * t:(u + 1) * t]
        l = l_ref[:, u * t:(u + 1) * t]
        acc = acc_ref[u]
        if has_sink:
            sk = sink_ref[pair * 2 + u]
            m2 = jnp.maximum(m, sk)
            w = jnp.exp2(m - m2)
            l = l * w + jnp.exp2(sk - m2)
            acc = acc * w
        oT_ref[u * HEAD_DIM:(u + 1) * HEAD_DIM, :] = (acc / l).astype(BF16)


def _band_attention(qT, k, vT, bias, sink, *, width, shared_kv):
    b, nqb, nq, t = qT.shape
    s = k.shape[1]
    nkb = s // t
    npairs = nq // (2 * HEAD_DIM)
    kv_idx = (lambda p: 0) if shared_kv else (lambda p: p)
    has_sink = sink is not None
    kern = functools.partial(_band_attn_kernel, t=t, cw=min(t, ATTN_CHUNK), lead=ATTN_LEAD,
                             per_trip=BAND_PER_TRIP, width=width, nkb=nkb, has_sink=has_sink)
    in_specs = [
        pl.BlockSpec(bias.shape, lambda bi, p, i: (0, 0, 0)),
        pl.BlockSpec((None, None, 2 * HEAD_DIM, t), lambda bi, p, i: (bi, i, p, 0)),
        pl.BlockSpec((None, s, 2 * HEAD_DIM), lambda bi, p, i: (bi, 0, kv_idx(p))),
        pl.BlockSpec((None, nkb, 2 * HEAD_DIM, t), lambda bi, p, i: (bi, 0, kv_idx(p), 0)),
    ]
    args = [bias, qT, k, vT]
    if has_sink:
        in_specs = [pl.BlockSpec(memory_space=pltpu.SMEM)] + in_specs
        args = [sink] + args
    return pl.pallas_call(
        kern,
        grid=(b, npairs, nqb),
        in_specs=in_specs,
        out_specs=pl.BlockSpec((None, None, 2 * HEAD_DIM, t), lambda bi, p, i: (bi, i, p, 0)),
        out_shape=jax.ShapeDtypeStruct((b, nqb, nq, t), BF16),
        scratch_shapes=[
            pltpu.VMEM((2 * HEAD_DIM, 2 * t), BF16),
            pltpu.VMEM((1, 2 * t), F32),
            pltpu.VMEM((1, 2 * t), F32),
            pltpu.VMEM((2, HEAD_DIM, t), F32),
            pltpu.VMEM((t, 2 * t), F32),
            pltpu.VMEM((t, 2 * t), F32),
        ],
        compiler_params=_params("parallel", "parallel", "arbitrary"),
        name="band_attention",
    )(*args)


def _diff_attn_kernel(lam_ref, subg_ref, qT_ref, k_ref, vT_ref, oT_ref, qz_ref, m_ref, l_ref, acc_ref,
                      sa_ref, sb_ref, *, t, nkb, cw, lead, per_trip, lam_init):
    _stage_queries(qT_ref, qz_ref, t)
    _init_stats(m_ref, l_ref, acc_ref)

    bufs = (sa_ref, sb_ref)
    step = functools.partial(_block_step, refs=(k_ref, vT_ref, qz_ref, m_ref, l_ref, acc_ref),
                             t=t, cw=cw, lead=lead, shared_v=True)

    for c in range(2 * t // cw):
        _scores_chunk(k_ref, 0, qz_ref, sa_ref, c, t, cw)

    def trip(j, carry):
        for i in range(per_trip):
            step(per_trip * j + i, bufs[i % 2], per_trip * j + i + 1, bufs[(i + 1) % 2])
        return carry

    ntrip = nkb // per_trip - 1
    lax.fori_loop(0, ntrip, trip, 0)
    for i in range(per_trip):
        kb = ntrip * per_trip + i
        step(kb, bufs[i % 2], kb + 1 if i + 1 < per_trip else None, bufs[(i + 1) % 2])

    lv = lam_ref[...]
    e1 = jnp.exp(jnp.sum(lv[0:1] * lv[1:2], axis=-1, keepdims=True))
    e2 = jnp.exp(jnp.sum(lv[2:3] * lv[3:4], axis=-1, keepdims=True))
    lam = e1 - e2 + lam_init
    o = acc_ref[0] / l_ref[:, 0:t] - lam * (acc_ref[1] / l_ref[:, t:2 * t])
    ms = jnp.mean(o * o, axis=0, keepdims=True)
    o = o * lax.rsqrt(ms + EPS) * subg_ref[...] * (1.0 - lam_init)
    oT_ref[...] = o.astype(BF16)


def _diff_attention(qT, k, vT, lamv, subg, *, lam_init):
    b, nqb, nq, t = qT.shape
    s = k.shape[1]
    nkb = s // t
    heads = nq // (2 * HEAD_DIM)
    assert nkb % DENSE_PER_TRIP == 0 and DENSE_PER_TRIP % 2 == 0
    kern = functools.partial(_diff_attn_kernel, t=t, nkb=nkb, cw=ATTN_CHUNK, lead=ATTN_LEAD,
                             per_trip=DENSE_PER_TRIP, lam_init=lam_init)
    return pl.pallas_call(
        kern,
        grid=(b, heads, nqb),
        in_specs=[
            pl.BlockSpec(lamv.shape, lambda bi, h, i: (0, 0)),
            pl.BlockSpec(subg.shape, lambda bi, h, i: (0, 0)),
            pl.BlockSpec((None, None, 2 * HEAD_DIM, t), lambda bi, h, i: (bi, i, h, 0)),
            pl.BlockSpec((None, s, 2 * HEAD_DIM), lambda bi, h, i: (bi, 0, h)),
            pl.BlockSpec((None, nkb, 2 * HEAD_DIM, t), lambda bi, h, i: (bi, 0, h, 0)),
        ],
        out_specs=pl.BlockSpec((None, None, 2 * HEAD_DIM, t), lambda bi, h, i: (bi, i, h, 0)),
        out_shape=jax.ShapeDtypeStruct((b, nqb, nq, t), BF16),
        scratch_shapes=[
            pltpu.VMEM((2 * HEAD_DIM, 2 * t), BF16),
            pltpu.VMEM((1, 2 * t), F32),
            pltpu.VMEM((1, 2 * t), F32),
            pltpu.VMEM((2, 2 * HEAD_DIM, t), F32),
            pltpu.VMEM((t, 2 * t), F32),
            pltpu.VMEM((t, 2 * t), F32),
        ],
        compiler_params=_params("parallel", "parallel", "arbitrary"),
        name="diff_attention",
    )(lamv, subg, qT, k, vT)


def _in_proj_tm_kernel(h_ref, g_ref, w_ref, kc_ref, ka_ref, kb_ref, q_ref, k_ref, v_ref, *, q_scale):
    u = _rms(h_ref[...], g_ref[...]).astype(BF16)
    y = jnp.dot(u, w_ref[...], preferred_element_type=F32)
    kc, ka, kb = kc_ref[...], ka_ref[...], kb_ref[...]
    nq, nk = q_ref.shape[1], k_ref.shape[1]

    def rope(j):
        x = y[:, j * LANES:(j + 1) * LANES]
        return x * kc + pltpu.roll(x, LANES - ROT_HALF, 1) * ka + pltpu.roll(x, ROT_HALF, 1) * kb

    for j in range(nq // LANES):
        q_ref[:, j * LANES:(j + 1) * LANES] = (rope(j) * q_scale).astype(q_ref.dtype)
    for j in range(nk // LANES):
        k_ref[:, j * LANES:(j + 1) * LANES] = rope(nq // LANES + j).astype(k_ref.dtype)
    v_ref[...] = y[:, nq + nk:].astype(v_ref.dtype)


def _in_proj_tm(h, g, w, kc, ka, kb, *, nq, nk, q_scale, out_dtype):
    b, s, d = h.shape
    nv = w.shape[1] - nq - nk
    tm = ROW_TILE
    const = lambda bi, i: (0, 0)
    row = lambda bi, i: (bi, i, 0)
    return pl.pallas_call(
        functools.partial(_in_proj_tm_kernel, q_scale=q_scale),
        grid=(b, s // tm),
        in_specs=[
            pl.BlockSpec((None, tm, d), row),
            pl.BlockSpec((1, d), const),
            pl.BlockSpec(w.shape, const),
            pl.BlockSpec((None, tm, LANES), row),
            pl.BlockSpec((None, tm, LANES), row),
            pl.BlockSpec((None, tm, LANES), row),
        ],
        out_specs=[pl.BlockSpec((None, tm, n), row) for n in (nq, nk, nv)],
        out_shape=[jax.ShapeDtypeStruct((b, s, n), out_dtype) for n in (nq, nk, nv)],
        compiler_params=_params("parallel", "parallel"),
        name="mixer_in_proj_tm",
    )(h, g, w, kc, ka, kb)


def _window_attn_kernel(*refs, radius, dil, per_trip, depth, has_sink):
    if has_sink:
        sink_ref, refs = refs[0], refs[1:]
    bias_ref, q_ref, k_ref, v_ref, o_ref, lse_ref, *scratch = refs
    bufs, scratch = scratch[:depth], scratch[depth:]
    pair = pl.program_id(1)
    total = q_ref.shape[0]
    seq = total // dil
    win = WIN_Q + 2 * radius
    nqb = total // WIN_Q
    lane = lax.broadcasted_iota(jnp.int32, (WIN_Q, 2 * HEAD_DIM), 1)

    if q_ref.dtype == F32:
        qg_ref, kg_ref, vg_ref = scratch[:3]
        onat_ref = scratch[3] if dil > 1 else None
        for src, dst in ((q_ref, qg_ref), (k_ref, kg_ref), (v_ref, vg_ref)):
            for r in range(dil):
                rows = pl.ds(r, seq, stride=dil) if dil > 1 else slice(None)
                dst[r * seq:(r + 1) * seq, :] = src[rows, :].astype(BF16)
    else:
        qg_ref, kg_ref, vg_ref = q_ref, k_ref, v_ref

    def window(i):
        q0 = i * WIN_Q
        lo = (q0 // seq) * seq
        k0 = jnp.clip(q0 - radius, lo, lo + seq - win)
        return pl.multiple_of(q0, WIN_Q), pl.multiple_of(k0, radius), (q0 - k0) // radius

    def scores(i, dst_ref):
        q0, k0, _ = window(i)
        q = qg_ref[pl.ds(q0, WIN_Q), :]
        zero = jnp.zeros_like(q)
        qz = jnp.concatenate([jnp.where(lane < HEAD_DIM, q, zero), jnp.where(lane >= HEAD_DIM, q, zero)],
                             axis=0)
        dst_ref[...] = lax.dot_general(kg_ref[pl.ds(k0, win), :], qz, _NT, preferred_element_type=F32)

    def finish(i, src_ref):
        q0, k0, bidx = window(i)
        bias = bias_ref[bidx]
        s = src_ref[...] + jnp.concatenate([bias, bias], axis=1)
        m = jnp.max(s, axis=0, keepdims=True)
        if has_sink:
            unit = lax.broadcasted_iota(jnp.int32, m.shape, 1) // WIN_Q
            sk = jnp.where(unit == 0, sink_ref[pair * 2], sink_ref[pair * 2 + 1])
            m = jnp.maximum(m, sk)
        p = jnp.exp2(s - m)
        l = jnp.sum(p, axis=0, keepdims=True)
        if has_sink:
            l = l + jnp.exp2(sk - m)
        oT = lax.dot_general(vg_ref[pl.ds(k0, win), :], p.astype(BF16), _TN, preferred_element_type=F32)
        oT = oT / l
        lse = m + jnp.log2(l)
        halves = []
        for u in range(2):
            halves.append(oT[u * HEAD_DIM:(u + 1) * HEAD_DIM, u * WIN_Q:(u + 1) * WIN_Q])
            lse_ref[u, pl.ds(i, 1), :] = lse[:, u * WIN_Q:(u + 1) * WIN_Q]
        o = jnp.concatenate(halves, axis=0).T
        if dil > 1:
            r = q0 // seq
            onat_ref[pl.ds(r + (q0 - r * seq) * dil, WIN_Q, stride=dil), :] = o
        else:
            o_ref[pl.ds(q0, WIN_Q), :] = o.astype(BF16)

    for n in range(depth - 1):
        scores(n, bufs[n])

    def trip(j, carry):
        for n in range(per_trip):
            i = per_trip * j + n
            scores(i + depth - 1, bufs[(n + depth - 1) % depth])
            finish(i, bufs[n % depth])
        return carry

    ntrip = nqb // per_trip - 1
    lax.fori_loop(0, ntrip, trip, 0)
    for n in range(per_trip):
        i = ntrip * per_trip + n
        if n + depth - 1 < per_trip:
            scores(i + depth - 1, bufs[(n + depth - 1) % depth])
        finish(i, bufs[n % depth])
    if dil > 1:
        o_ref[...] = onat_ref[...].astype(BF16)


def _window_bias(radius):
    win = WIN_Q + 2 * radius
    i = np.arange(win)[:, None]
    j = np.arange(WIN_Q)[None, :]
    return jnp.asarray(np.stack([np.where(np.abs(i - j - b * radius) <= radius, 0.0, NEG_INF)
                                 for b in range(3)]), F32)


def _window_attention(q, k, v, sink, *, radius, dil):
    b, s, nq = q.shape
    seq = s // dil
    nqb = s // WIN_Q
    npairs = nq // LANES
    win = WIN_Q + 2 * radius
    shared = k.shape[2] == LANES and npairs > 1
    has_sink = sink is not None
    per_trip = min(WIN_PER_TRIP, nqb)
    depth = min(WIN_DEPTH, per_trip)
    assert seq % WIN_Q == 0 and seq >= win and WIN_Q % radius == 0
    assert nqb % per_trip == 0 and (per_trip % depth == 0 or per_trip == nqb)
    assert q.dtype == F32 or dil == 1
    qspec = pl.BlockSpec((None, s, LANES), lambda bi, p: (bi, 0, p))
    kvspec = pl.BlockSpec((None, s, LANES), lambda bi, p: (bi, 0, 0)) if shared else qspec
    bias = _window_bias(radius)
    in_specs = [pl.BlockSpec(bias.shape, lambda bi, p: (0, 0, 0)), qspec, kvspec, kvspec]
    args = [bias, q, k, v]
    if has_sink:
        in_specs = [pl.BlockSpec(memory_space=pltpu.SMEM)] + in_specs
        args = [sink] + args
    scratch = [pltpu.VMEM((win, 2 * WIN_Q), F32)] * depth
    if q.dtype == F32:
        scratch += [pltpu.VMEM((s, LANES), BF16)] * 3
    if dil > 1:
        scratch += [pltpu.VMEM((s, LANES), F32)]
    o, lse = pl.pallas_call(
        functools.partial(_window_attn_kernel, radius=radius, dil=dil, per_trip=per_trip, depth=depth,
                          has_sink=has_sink),
        grid=(b, npairs),
        in_specs=in_specs,
        out_specs=[qspec, pl.BlockSpec((None, None, 2, nqb, WIN_Q), lambda bi, p: (bi, p, 0, 0, 0))],
        out_shape=[jax.ShapeDtypeStruct((b, s, nq), BF16),
                   jax.ShapeDtypeStruct((b, npairs, 2, nqb, WIN_Q), F32)],
        scratch_shapes=scratch,
        compiler_params=_params("parallel", "parallel"),
        name="window_attention",
    )(*args)
    lse = lse.reshape(b, 2 * npairs, dil, seq).transpose(0, 3, 2, 1).reshape(b, s, 2 * npairs)
    return o, lse


def _mixer_out(mixer_refs, wout_ref, mode):
    if mode == "feature_major":
        (oT_ref,) = mixer_refs
        ys = [lax.dot_general(oT_ref[j], wout_ref[...], _TN, preferred_element_type=F32)
              for j in range(oT_ref.shape[0])]
        return jnp.concatenate(ys, axis=0) if len(ys) > 1 else ys[0]
    if mode == "token_major":
        (o_ref,) = mixer_refs
        return jnp.dot(o_ref[...], wout_ref[...], preferred_element_type=F32)
    n = (len(mixer_refs) - 1) // 2
    o_refs, lse_refs, expand_ref = mixer_refs[:n], mixer_refs[n:2 * n], mixer_refs[2 * n]
    lses = [r[...] for r in lse_refs]
    top = functools.reduce(jnp.maximum, lses)
    es = [jnp.exp2(x - top) for x in lses]
    z = functools.reduce(jnp.add, es)
    o = None
    for e, o_ref in zip(es, o_refs):
        w = e / z
        hi = w.astype(BF16)
        lo = (w - hi.astype(F32)).astype(BF16)
        wide = jnp.dot(jnp.concatenate([hi, lo], axis=1), expand_ref[...], preferred_element_type=F32)
        term = wide * o_ref[...].astype(F32)
        o = term if o is None else o + term
    return jnp.dot(o.astype(BF16), wout_ref[...], preferred_element_type=F32)


def _mid_kernel(*refs, mode, n_mixer, x_scale):
    h_ref = refs[0]
    mixer_refs = refs[1:1 + n_mixer]
    wout_ref, gmix_ref, gpre_ref, wq_ref, kT_ref, v_ref, wo_ref, gpost_ref, out_ref = refs[1 + n_mixer:]
    y = _mixer_out(mixer_refs, wout_ref, mode)
    h1 = h_ref[...] + _rms(y, gmix_ref[...])

    u = _rms(h1, gpre_ref[...]).astype(BF16)
    q = (jnp.dot(u, wq_ref[...], preferred_element_type=F32) * x_scale).astype(BF16)
    xd = q.shape[1] // X_HEADS
    outs = []
    for hd in range(X_HEADS):
        s = jnp.dot(q[:, hd * xd:(hd + 1) * xd], kT_ref[hd * xd:(hd + 1) * xd, :],
                    preferred_element_type=F32)
        p = jnp.exp2(s - jnp.max(s, axis=-1, keepdims=True))
        l = jnp.sum(p, axis=-1, keepdims=True)
        o = jnp.dot(p.astype(BF16), v_ref[:, hd * xd:(hd + 1) * xd], preferred_element_type=F32)
        outs.append((o / l).astype(BF16))
    y2 = jnp.dot(jnp.concatenate(outs, axis=1), wo_ref[...], preferred_element_type=F32)
    out_ref[...] = h1 + _rms(y2, gpost_ref[...])


def _mid(h, mixer, mode, wout, gmix, gpre, wq, kT, v, wo, gpost, *, x_scale):
    b, s, d = h.shape
    tm = ROW_TILE
    n_mem = v.shape[1]
    const = lambda bi, i: (0, 0)
    row = lambda bi, i: (bi, i, 0)
    if mode == "feature_major":
        oblk = mixer[0].shape[3]
        mixer_specs = [pl.BlockSpec((None, tm // oblk, d, oblk), lambda bi, i: (bi, i, 0, 0))]
    else:
        mixer_specs = [pl.BlockSpec((None, tm, a.shape[2]), row) for a in mixer]
    if mode == "branches":
        heads = mixer[-1].shape[2]
        expand = jnp.asarray(np.tile(np.repeat(np.eye(heads), d // heads, axis=1), (2, 1)), BF16)
        mixer = list(mixer) + [expand]
        mixer_specs.append(pl.BlockSpec(expand.shape, const))
    kern = functools.partial(_mid_kernel, mode=mode, n_mixer=len(mixer), x_scale=x_scale)
    return pl.pallas_call(
        kern,
        grid=(b, s // tm),
        in_specs=[
            pl.BlockSpec((None, tm, d), row),
            *mixer_specs,
            pl.BlockSpec((d, d), const),
            pl.BlockSpec((1, d), const),
            pl.BlockSpec((1, d), const),
            pl.BlockSpec((d, d), const),
            pl.BlockSpec((None, d, n_mem), lambda bi, i: (bi, 0, 0)),
            pl.BlockSpec((None, n_mem, d), lambda bi, i: (bi, 0, 0)),
            pl.BlockSpec((d, d), const),
            pl.BlockSpec((1, d), const),
        ],
        out_specs=pl.BlockSpec((None, tm, d), lambda bi, i: (bi, i, 0)),
        out_shape=jax.ShapeDtypeStruct((b, s, d), F32),
        compiler_params=_params("parallel", "parallel"),
        name="out_proj_cross_attention",
    )(h, *mixer, wout, gmix, gpre, wq, kT, v, wo, gpost)


def _mem_kv_kernel(mem_ref, g_ref, wkT_ref, wv_ref, kT_ref, v_ref):
    mn = _rms(mem_ref[...], g_ref[...]).astype(BF16)
    kT_ref[...] = lax.dot_general(wkT_ref[...], mn, _NT, preferred_element_type=F32).astype(BF16)
    v_ref[...] = jnp.dot(mn, wv_ref[...], preferred_element_type=F32).astype(BF16)


def _mem_kv(mem, g, wkT, wv):
    depth, d = g.shape[0], g.shape[2]
    b, n_mem, _ = mem.shape
    return pl.pallas_call(
        _mem_kv_kernel,
        grid=(depth, b),
        in_specs=[
            pl.BlockSpec((None, n_mem, d), lambda li, bi: (bi, 0, 0)),
            pl.BlockSpec((None, 1, d), lambda li, bi: (li, 0, 0)),
            pl.BlockSpec((None, d, d), lambda li, bi: (li, 0, 0)),
            pl.BlockSpec((None, d, d), lambda li, bi: (li, 0, 0)),
        ],
        out_specs=[
            pl.BlockSpec((None, None, d, n_mem), lambda li, bi: (li, bi, 0, 0)),
            pl.BlockSpec((None, None, n_mem, d), lambda li, bi: (li, bi, 0, 0)),
        ],
        out_shape=[
            jax.ShapeDtypeStruct((depth, b, d, n_mem), BF16),
            jax.ShapeDtypeStruct((depth, b, n_mem, d), BF16),
        ],
        compiler_params=_params("parallel", "parallel"),
        name="memory_kv",
    )(mem, g, wkT, wv)


def _ffn_kernel(h_ref, gpre_ref, wg_ref, wu_ref, wd_ref, gpost_ref, out_ref, u_ref, acc_ref):
    j = pl.program_id(1)

    @pl.when(j == 0)
    def _():
        u_ref[...] = _rms(h_ref[...], gpre_ref[...]).astype(BF16)
        acc_ref[...] = jnp.zeros(acc_ref.shape, F32)

    u = u_ref[...]
    g = jnp.dot(u, wg_ref[...], preferred_element_type=F32)
    up = jnp.dot(u, wu_ref[...], preferred_element_type=F32)
    a = (g / (1.0 + jnp.exp(-g)) * up).astype(BF16)
    acc_ref[...] += jnp.dot(a, wd_ref[...], preferred_element_type=F32)

    @pl.when(j == pl.num_programs(1) - 1)
    def _():
        out_ref[...] = h_ref[...] + _rms(acc_ref[...], gpost_ref[...])


def _ffn(h, gpre, wgu, wd, gpost):
    b, s, d = h.shape
    tm = ROW_TILE
    dff = wd.shape[0]
    fc = dff // FF_CHUNKS
    rows = b * s
    h2 = h.reshape(rows, d)
    out = pl.pallas_call(
        _ffn_kernel,
        grid=(rows // tm, FF_CHUNKS),
        in_specs=[
            pl.BlockSpec((tm, d), lambda i, j: (i, 0)),
            pl.BlockSpec((1, d), lambda i, j: (0, 0)),
            pl.BlockSpec((d, fc), lambda i, j: (0, j)),
            pl.BlockSpec((d, fc), lambda i, j: (0, FF_CHUNKS + j)),
            pl.BlockSpec((fc, d), lambda i, j: (j, 0)),
            pl.BlockSpec((1, d), lambda i, j: (0, 0)),
        ],
        out_specs=pl.BlockSpec((tm, d), lambda i, j: (i, 0)),
        out_shape=jax.ShapeDtypeStruct((rows, d), F32),
        scratch_shapes=[pltpu.VMEM((tm, d), BF16), pltpu.VMEM((tm, d), F32)],
        compiler_params=_params("parallel", "arbitrary"),
        name="swiglu_ffn",
    )(h2, gpre, wgu, wgu, wd, gpost)
    return out.reshape(b, s, d)


def _rope_tables(positions):
    inv_freq = ROPE_THETA ** (-jnp.arange(0, 2 * ROT_HALF, 2, dtype=F32) / (2 * ROT_HALF))
    ang = positions.astype(F32)[..., None] * inv_freq
    cos, sin = jnp.cos(ang), jnp.sin(ang)
    cosT, sinT = cos.transpose(0, 2, 1), sin.transpose(0, 2, 1)
    zeros = jnp.zeros_like(cos)
    pad = HEAD_DIM - 2 * ROT_HALF
    ones_tail = jnp.ones(cos.shape[:-1] + (pad,), F32)
    zero_tail = jnp.zeros(cos.shape[:-1] + (pad,), F32)
    reps = LANES // HEAD_DIM
    kc = jnp.tile(jnp.concatenate([cos, cos, ones_tail], axis=-1), reps)
    ka = jnp.tile(jnp.concatenate([-sin, zeros, zero_tail], axis=-1), reps)
    kb = jnp.tile(jnp.concatenate([zeros, sin, zero_tail], axis=-1), reps)
    return cosT, sinT, kc, ka, kb


def _band_bias(t, width, multiplicity):
    i = np.arange(t)[:, None]
    j = np.arange(t)[None, :]
    tiles = []
    for d in range(-width, width + 1):
        c = multiplicity(d * t + i - j)
        tiles.append(np.where(c > 0, np.log2(np.maximum(c, 1)), NEG_INF))
    tiles.append(np.full((t, t), NEG_INF))
    return jnp.asarray(np.stack(tiles), F32)


def _dilated_multiplicity(delta):
    c = np.zeros(delta.shape, np.int64)
    for window, dil in A_PATTERNS:
        c += (delta % dil == 0) & (np.abs(delta) <= (window // (2 * dil)) * dil)
    return c


def _window_multiplicity(delta):
    return (np.abs(delta) <= C_RADIUS).astype(np.int64)


def _row(g):
    return g.reshape(1, -1)


def kernel(x, mem, positions, mix_pre_g, mix_post_g, mem_pre_g, mem_kv_g, mem_post_g, ffn_pre_g, ffn_post_g,
           a_w_in, a_w_out, b_w_in, b_w_out, b_lam_q1, b_lam_k1, b_lam_q2, b_lam_k2, b_sub_g, c_w_in, c_w_out,
           c_sink, x_wq, x_wkv, x_wo, w_gate_up, w_down):
    depth, d = mix_pre_g.shape
    assert d % (2 * HEAD_DIM) == 0 and x.shape[1] % DENSE_BLOCK == 0 and x.shape[1] % ROW_TILE == 0
    cosT, sinT, kc, ka, kb = _rope_tables(positions)
    q_scale = HEAD_DIM ** -0.5 * LOG2E
    x_scale = (d // X_HEADS) ** -0.5 * LOG2E

    mem_kT, mem_v = _mem_kv(mem, mem_kv_g.reshape(depth, 1, d),
                            x_wkv[:, :, :d].transpose(0, 2, 1).astype(BF16), x_wkv[:, :, d:].astype(BF16))

    h = x
    for i in range(depth):
        kind, j = i % N_MIXERS, i // N_MIXERS
        g_pre = _row(mix_pre_g[i])
        if kind == 0:
            w_in, w_out = a_w_in[j], a_w_out[j]
            q, k, v = _in_proj_tm(h, g_pre, w_in.astype(BF16), kc, ka, kb, nq=d, nk=d, q_scale=q_scale,
                                  out_dtype=F32)
            branches = [_window_attention(q, k, v, None, radius=window // (2 * dil), dil=dil)
                        for window, dil in A_PATTERNS]
            mixer, mode = [o for o, _ in branches] + [lse for _, lse in branches], "branches"
        elif kind == 1:
            w_in, w_out = b_w_in[j], b_w_out[j]
            wq, wk, wv = w_in[:, :d], w_in[:, d:2 * d], w_in[:, 2 * d:]
            qT, k, vT = _in_proj(h, g_pre, wq.T.astype(BF16), wk.astype(BF16), wv.T.astype(BF16),
                                 cosT, sinT, kc, ka, kb, q_scale=q_scale, qblk=DENSE_BLOCK, vblk=DENSE_BLOCK)
            lam_init = 0.8 - 0.6 * math.exp(-0.3 * i)
            lamv = jnp.stack([b_lam_q1[j], b_lam_k1[j], b_lam_q2[j], b_lam_k2[j]]).astype(F32)
            subg = jnp.broadcast_to(b_sub_g[j].astype(F32)[:, None], (2 * HEAD_DIM, DENSE_BLOCK))
            mixer, mode = [_diff_attention(qT, k, vT, lamv, subg, lam_init=lam_init)], "feature_major"
        else:
            w_in, w_out = c_w_in[j], c_w_out[j]
            n_kv = (w_in.shape[1] - d) // (2 * HEAD_DIM)
            grp = (d // HEAD_DIM) // n_kv
            perm = np.arange(d).reshape(n_kv, grp, HEAD_DIM).transpose(1, 0, 2).reshape(-1)
            w_in = jnp.concatenate([w_in[:, :d][:, perm], w_in[:, d:]], axis=1)
            w_out = w_out[perm, :]
            q, k, v = _in_proj_tm(h, g_pre, w_in.astype(BF16), kc, ka, kb, nq=d, nk=n_kv * HEAD_DIM,
                                  q_scale=q_scale, out_dtype=BF16)
            sink = (c_sink[j].astype(F32) * LOG2E)[perm[::HEAD_DIM] // HEAD_DIM]
            mixer, mode = [_window_attention(q, k, v, sink, radius=C_RADIUS, dil=1)[0]], "token_major"
        h = _mid(h, mixer, mode, w_out.astype(BF16), _row(mix_post_g[i]), _row(mem_pre_g[i]),
                 x_wq[i].astype(BF16), mem_kT[i], mem_v[i], x_wo[i].astype(BF16), _row(mem_post_g[i]),
                 x_scale=x_scale)
        h = _ffn(h, _row(ffn_pre_g[i]), w_gate_up[i].astype(BF16), w_down[i].astype(BF16), _row(ffn_post_g[i]))
    return h
```

```python
import functools
import math

import jax
import jax.numpy as jnp
import numpy as np
from jax import lax
from jax.experimental import pallas as pl
from jax.experimental.pallas import tpu as pltpu

F32 = jnp.float32
BF16 = jnp.bfloat16

HEAD_DIM = 64
ROT_HALF = HEAD_DIM // 8
ROPE_THETA = 500000.0
EPS = 1e-6
NEG_INF = -1e30
LOG2E = 1.4426950408889634
N_MIXERS = 3

A_PATTERNS = ((128, 1), (512, 4), (2048, 16))
C_RADIUS = 128
X_HEADS = 4

LANES = 128
ROW_TILE = 512
DENSE_BLOCK = 512
ATTN_CHUNK = 256
ATTN_LEAD = 1
DENSE_PER_TRIP = 4
WIN_Q = 128
WIN_PER_TRIP = 16
WIN_DEPTH = 4
FF_CHUNKS = 2
FF_SUBTILES = 2
VMEM_LIMIT = 56 * 1024 * 1024

_NT = (((1,), (1,)), ((), ()))
_TN = (((0,), (0,)), ((), ()))


def _params(*sem):
    return pltpu.CompilerParams(dimension_semantics=sem, vmem_limit_bytes=VMEM_LIMIT)


def _rms(x, g):
    ms = jnp.mean(x * x, axis=-1, keepdims=True)
    return x * lax.rsqrt(ms + EPS) * g


def _in_proj_kernel(h_ref, g_ref, wqT_ref, wk_ref, wvT_ref, cosT_ref, sinT_ref, kc_ref, ka_ref, kb_ref,
                    qT_ref, k_ref, vT_ref, *, q_scale, qblk, vblk):
    tm = h_ref.shape[0]
    u = _rms(h_ref[...], g_ref[...]).astype(BF16)

    kf = jnp.dot(u, wk_ref[...], preferred_element_type=F32)
    kc, ka, kb = kc_ref[...], ka_ref[...], kb_ref[...]
    for j in range(kf.shape[1] // LANES):
        x = kf[:, j * LANES:(j + 1) * LANES]
        y = x * kc + pltpu.roll(x, LANES - ROT_HALF, 1) * ka + pltpu.roll(x, ROT_HALF, 1) * kb
        k_ref[:, j * LANES:(j + 1) * LANES] = y.astype(BF16)

    qf = lax.dot_general(wqT_ref[...], u, _NT, preferred_element_type=F32)
    c = cosT_ref[...] * q_scale
    s = sinT_ref[...] * q_scale
    for unit in range(qf.shape[0] // HEAD_DIM):
        r0 = unit * HEAD_DIM
        t1 = qf[r0:r0 + ROT_HALF]
        t2 = qf[r0 + ROT_HALF:r0 + 2 * ROT_HALF]
        rest = qf[r0 + 2 * ROT_HALF:r0 + HEAD_DIM] * q_scale
        blk = jnp.concatenate([t1 * c - t2 * s, t2 * c + t1 * s, rest], axis=0).astype(BF16)
        for jb in range(tm // qblk):
            qT_ref[jb, r0:r0 + HEAD_DIM, :] = blk[:, jb * qblk:(jb + 1) * qblk]

    vf = lax.dot_general(wvT_ref[...], u, _NT, preferred_element_type=F32).astype(BF16)
    for jb in range(tm // vblk):
        vT_ref[jb] = vf[:, jb * vblk:(jb + 1) * vblk]


def _in_proj(h, g, wqT, wk, wvT, cosT, sinT, kc, ka, kb, *, q_scale, qblk, vblk):
    b, s, d = h.shape
    nq, nk, nv = wqT.shape[0], wk.shape[1], wvT.shape[0]
    tm = ROW_TILE
    kern = functools.partial(_in_proj_kernel, q_scale=q_scale, qblk=qblk, vblk=vblk)
    const = lambda bi, i: (0, 0)
    return pl.pallas_call(
        kern,
        grid=(b, s // tm),
        in_specs=[
            pl.BlockSpec((None, tm, d), lambda bi, i: (bi, i, 0)),
            pl.BlockSpec((1, d), const),
            pl.BlockSpec((nq, d), const),
            pl.BlockSpec((d, nk), const),
            pl.BlockSpec((nv, d), const),
            pl.BlockSpec((None, ROT_HALF, tm), lambda bi, i: (bi, 0, i)),
            pl.BlockSpec((None, ROT_HALF, tm), lambda bi, i: (bi, 0, i)),
            pl.BlockSpec((None, tm, LANES), lambda bi, i: (bi, i, 0)),
            pl.BlockSpec((None, tm, LANES), lambda bi, i: (bi, i, 0)),
            pl.BlockSpec((None, tm, LANES), lambda bi, i: (bi, i, 0)),
        ],
        out_specs=[
            pl.BlockSpec((None, tm // qblk, nq, qblk), lambda bi, i: (bi, i, 0, 0)),
            pl.BlockSpec((None, tm, nk), lambda bi, i: (bi, i, 0)),
            pl.BlockSpec((None, tm // vblk, nv, vblk), lambda bi, i: (bi, i, 0, 0)),
        ],
        out_shape=[
            jax.ShapeDtypeStruct((b, s // qblk, nq, qblk), BF16),
            jax.ShapeDtypeStruct((b, s, nk), BF16),
            jax.ShapeDtypeStruct((b, s // vblk, nv, vblk), BF16),
        ],
        compiler_params=_params("parallel", "parallel"),
        name="mixer_in_proj",
    )(h, g, wqT, wk, wvT, cosT, sinT, kc, ka, kb)


def _stage_queries(qT_ref, qz_ref, t):
    q = qT_ref[...]
    row = lax.broadcasted_iota(jnp.int32, q.shape, 0)
    zero = jnp.zeros_like(q)
    qz_ref[:, 0:t] = jnp.where(row < HEAD_DIM, q, zero)
    qz_ref[:, t:2 * t] = jnp.where(row >= HEAD_DIM, q, zero)


def _scores_chunk(k_ref, kb, qz_ref, dst, c, t, cw):
    s_ref, top_ref = dst
    kblk = k_ref[pl.ds(pl.multiple_of(kb * t, t), t), :]
    cols = slice(c * cw, (c + 1) * cw)
    s = jnp.dot(kblk, qz_ref[:, cols], preferred_element_type=F32)
    s_ref[:, cols] = s
    top_ref[:, cols] = jnp.max(s, axis=0, keepdims=True)


def _block_step(kb, src, kb_next, dst, refs, *, t, cw, lead):
    k_ref, vT_ref, qz_ref, m_ref, l_ref, acc_ref = refs
    s_ref, top_ref = src
    nchunk = 2 * t // cw
    v = vT_ref[kb]
    if kb_next is not None:
        for c in range(lead):
            _scores_chunk(k_ref, kb_next, qz_ref, dst, c, t, cw)
    for c in range(nchunk):
        if kb_next is not None and c + lead < nchunk:
            _scores_chunk(k_ref, kb_next, qz_ref, dst, c + lead, t, cw)
        u, cc = divmod(c, t // cw)
        cols = slice(c * cw, (c + 1) * cw)
        acc_at = acc_ref.at[u, :, cc * cw:(cc + 1) * cw]
        m_old = m_ref[:, cols]
        m_new = jnp.maximum(m_old, top_ref[:, cols])
        alpha = jnp.exp2(m_old - m_new)
        p = jnp.exp2(s_ref[:, cols] - m_new)
        l_ref[:, cols] = alpha * l_ref[:, cols] + jnp.sum(p, axis=0, keepdims=True)
        m_ref[:, cols] = m_new
        acc_at[...] = alpha * acc_at[...] + jnp.dot(v, p.astype(BF16), preferred_element_type=F32)


def _diff_attn_kernel(lam_ref, subg_ref, qT_ref, k_ref, vT_ref, oT_ref, qz_ref, m_ref, l_ref, acc_ref,
                      sa_ref, sb_ref, ta_ref, tb_ref, *, t, nkb, cw, lead, per_trip, lam_init):
    _stage_queries(qT_ref, qz_ref, t)
    m_ref[...] = jnp.full(m_ref.shape, NEG_INF, F32)
    l_ref[...] = jnp.zeros(l_ref.shape, F32)
    acc_ref[...] = jnp.zeros(acc_ref.shape, F32)

    bufs = ((sa_ref, ta_ref), (sb_ref, tb_ref))
    step = functools.partial(_block_step, refs=(k_ref, vT_ref, qz_ref, m_ref, l_ref, acc_ref),
                             t=t, cw=cw, lead=lead)

    for c in range(2 * t // cw):
        _scores_chunk(k_ref, 0, qz_ref, bufs[0], c, t, cw)

    def trip(j, carry):
        for i in range(per_trip):
            step(per_trip * j + i, bufs[i % 2], per_trip * j + i + 1, bufs[(i + 1) % 2])
        return carry

    ntrip = nkb // per_trip - 1
    lax.fori_loop(0, ntrip, trip, 0)
    for i in range(per_trip):
        kb = ntrip * per_trip + i
        step(kb, bufs[i % 2], kb + 1 if i + 1 < per_trip else None, bufs[(i + 1) % 2])

    lv = lam_ref[...]
    e1 = jnp.exp(jnp.sum(lv[0:1] * lv[1:2], axis=-1, keepdims=True))
    e2 = jnp.exp(jnp.sum(lv[2:3] * lv[3:4], axis=-1, keepdims=True))
    lam = e1 - e2 + lam_init
    o = acc_ref[0] / l_ref[:, 0:t] - lam * (acc_ref[1] / l_ref[:, t:2 * t])
    ms = jnp.mean(o * o, axis=0, keepdims=True)
    o = o * lax.rsqrt(ms + EPS) * subg_ref[...] * (1.0 - lam_init)
    oT_ref[...] = o.astype(BF16)


def _diff_attention(qT, k, vT, lamv, subg, *, lam_init):
    b, nqb, nq, t = qT.shape
    s = k.shape[1]
    nkb = s // t
    heads = nq // (2 * HEAD_DIM)
    assert nkb % DENSE_PER_TRIP == 0 and DENSE_PER_TRIP % 2 == 0
    kern = functools.partial(_diff_attn_kernel, t=t, nkb=nkb, cw=ATTN_CHUNK, lead=ATTN_LEAD,
                             per_trip=DENSE_PER_TRIP, lam_init=lam_init)
    return pl.pallas_call(
        kern,
        grid=(b, heads, nqb),
        in_specs=[
            pl.BlockSpec(lamv.shape, lambda bi, h, i: (0, 0)),
            pl.BlockSpec(subg.shape, lambda bi, h, i: (0, 0)),
            pl.BlockSpec((None, None, 2 * HEAD_DIM, t), lambda bi, h, i: (bi, i, h, 0)),
            pl.BlockSpec((None, s, 2 * HEAD_DIM), lambda bi, h, i: (bi, 0, h)),
            pl.BlockSpec((None, nkb, 2 * HEAD_DIM, t), lambda bi, h, i: (bi, 0, h, 0)),
        ],
        out_specs=pl.BlockSpec((None, None, 2 * HEAD_DIM, t), lambda bi, h, i: (bi, i, h, 0)),
        out_shape=jax.ShapeDtypeStruct((b, nqb, nq, t), BF16),
        scratch_shapes=[
            pltpu.VMEM((2 * HEAD_DIM, 2 * t), BF16),
            pltpu.VMEM((1, 2 * t), F32),
            pltpu.VMEM((1, 2 * t), F32),
            pltpu.VMEM((2, 2 * HEAD_DIM, t), F32),
            pltpu.VMEM((t, 2 * t), F32),
            pltpu.VMEM((t, 2 * t), F32),
            pltpu.VMEM((1, 2 * t), F32),
            pltpu.VMEM((1, 2 * t), F32),
        ],
        compiler_params=_params("parallel", "parallel", "arbitrary"),
        name="diff_attention",
    )(lamv, subg, qT, k, vT)


def _in_proj_tm_kernel(h_ref, g_ref, w_ref, kc_ref, ka_ref, kb_ref, q_ref, k_ref, v_ref, *, q_scale):
    u = _rms(h_ref[...], g_ref[...]).astype(BF16)
    y = jnp.dot(u, w_ref[...], preferred_element_type=F32)
    kc, ka, kb = kc_ref[...], ka_ref[...], kb_ref[...]
    nq, nk = q_ref.shape[1], k_ref.shape[1]

    def rope(j):
        x = y[:, j * LANES:(j + 1) * LANES]
        return x * kc + pltpu.roll(x, LANES - ROT_HALF, 1) * ka + pltpu.roll(x, ROT_HALF, 1) * kb

    for j in range(nq // LANES):
        q_ref[:, j * LANES:(j + 1) * LANES] = (rope(j) * q_scale).astype(q_ref.dtype)
    for j in range(nk // LANES):
        k_ref[:, j * LANES:(j + 1) * LANES] = rope(nq // LANES + j).astype(k_ref.dtype)
    v_ref[...] = y[:, nq + nk:].astype(v_ref.dtype)


def _in_proj_tm(h, g, w, kc, ka, kb, *, nq, nk, q_scale, out_dtype):
    b, s, d = h.shape
    nv = w.shape[1] - nq - nk
    tm = ROW_TILE
    const = lambda bi, i: (0, 0)
    row = lambda bi, i: (bi, i, 0)
    return pl.pallas_call(
        functools.partial(_in_proj_tm_kernel, q_scale=q_scale),
        grid=(b, s // tm),
        in_specs=[
            pl.BlockSpec((None, tm, d), row),
            pl.BlockSpec((1, d), const),
            pl.BlockSpec(w.shape, const),
            pl.BlockSpec((None, tm, LANES), row),
            pl.BlockSpec((None, tm, LANES), row),
            pl.BlockSpec((None, tm, LANES), row),
        ],
        out_specs=[pl.BlockSpec((None, tm, n), row) for n in (nq, nk, nv)],
        out_shape=[jax.ShapeDtypeStruct((b, s, n), out_dtype) for n in (nq, nk, nv)],
        compiler_params=_params("parallel", "parallel"),
        name="mixer_in_proj_tm",
    )(h, g, w, kc, ka, kb)


def _window_attn_kernel(*refs, radius, dil, per_trip, depth, has_sink):
    if has_sink:
        sink_ref, refs = refs[0], refs[1:]
    bias_ref, q_ref, k_ref, v_ref, o_ref, lse_ref, *scratch = refs
    bufs, scratch = scratch[:depth], scratch[depth:]
    pair = pl.program_id(1)
    total = q_ref.shape[0]
    seq = total // dil
    win = WIN_Q + 2 * radius
    nqb = total // WIN_Q
    lane = lax.broadcasted_iota(jnp.int32, (WIN_Q, 2 * HEAD_DIM), 1)

    if q_ref.dtype == F32:
        qg_ref, kg_ref, vg_ref = scratch[:3]
        onat_ref = scratch[3] if dil > 1 else None
        for src, dst in ((q_ref, qg_ref), (k_ref, kg_ref), (v_ref, vg_ref)):
            for r in range(dil):
                rows = pl.ds(r, seq, stride=dil) if dil > 1 else slice(None)
                dst[r * seq:(r + 1) * seq, :] = src[rows, :].astype(BF16)
    else:
        qg_ref, kg_ref, vg_ref = q_ref, k_ref, v_ref

    def window(i):
        q0 = i * WIN_Q
        lo = (q0 // seq) * seq
        k0 = jnp.clip(q0 - radius, lo, lo + seq - win)
        return pl.multiple_of(q0, WIN_Q), pl.multiple_of(k0, radius), (q0 - k0) // radius

    def scores(i, dst_ref):
        q0, k0, _ = window(i)
        q = qg_ref[pl.ds(q0, WIN_Q), :]
        zero = jnp.zeros_like(q)
        qz = jnp.concatenate([jnp.where(lane < HEAD_DIM, q, zero), jnp.where(lane >= HEAD_DIM, q, zero)],
                             axis=0)
        dst_ref[...] = lax.dot_general(kg_ref[pl.ds(k0, win), :], qz, _NT, preferred_element_type=F32)

    def finish(i, src_ref):
        q0, k0, bidx = window(i)
        bias = bias_ref[bidx]
        s = src_ref[...] + jnp.concatenate([bias, bias], axis=1)
        m = jnp.max(s, axis=0, keepdims=True)
        if has_sink:
            unit = lax.broadcasted_iota(jnp.int32, m.shape, 1) // WIN_Q
            sk = jnp.where(unit == 0, sink_ref[pair * 2], sink_ref[pair * 2 + 1])
            m = jnp.maximum(m, sk)
        p = jnp.exp2(s - m)
        l = jnp.sum(p, axis=0, keepdims=True)
        if has_sink:
            l = l + jnp.exp2(sk - m)
        oT = lax.dot_general(vg_ref[pl.ds(k0, win), :], p.astype(BF16), _TN, preferred_element_type=F32)
        lse = m + jnp.log2(l)
        halves = []
        for u in range(2):
            cols = slice(u * WIN_Q, (u + 1) * WIN_Q)
            halves.append(oT[u * HEAD_DIM:(u + 1) * HEAD_DIM, cols] / l[:, cols])
            lse_ref[u, pl.ds(i, 1), :] = lse[:, cols]
        o = jnp.concatenate(halves, axis=0).T
        if dil > 1:
            r = q0 // seq
            onat_ref[pl.ds(r + (q0 - r * seq) * dil, WIN_Q, stride=dil), :] = o
        else:
            o_ref[pl.ds(q0, WIN_Q), :] = o.astype(BF16)

    for n in range(depth - 1):
        scores(n, bufs[n])

    def trip(j, carry):
        for n in range(per_trip):
            i = per_trip * j + n
            scores(i + depth - 1, bufs[(n + depth - 1) % depth])
            finish(i, bufs[n % depth])
        return carry

    ntrip = nqb // per_trip - 1
    lax.fori_loop(0, ntrip, trip, 0)
    for n in range(per_trip):
        i = ntrip * per_trip + n
        if n + depth - 1 < per_trip:
            scores(i + depth - 1, bufs[(n + depth - 1) % depth])
        finish(i, bufs[n % depth])
    if dil > 1:
        o_ref[...] = onat_ref[...].astype(BF16)


def _window_bias(radius):
    win = WIN_Q + 2 * radius
    i = np.arange(win)[:, None]
    j = np.arange(WIN_Q)[None, :]
    return jnp.asarray(np.stack([np.where(np.abs(i - j - b * radius) <= radius, 0.0, NEG_INF)
                                 for b in range(3)]), F32)


def _window_attention(q, k, v, sink, *, radius, dil):
    b, s, nq = q.shape
    seq = s // dil
    nqb = s // WIN_Q
    npairs = nq // LANES
    win = WIN_Q + 2 * radius
    shared = k.shape[2] == LANES and npairs > 1
    has_sink = sink is not None
    per_trip = min(WIN_PER_TRIP, nqb)
    depth = min(WIN_DEPTH, per_trip)
    assert seq % WIN_Q == 0 and seq >= win and WIN_Q % radius == 0
    assert nqb % per_trip == 0 and (per_trip % depth == 0 or per_trip == nqb)
    assert q.dtype == F32 or dil == 1
    qspec = pl.BlockSpec((None, s, LANES), lambda bi, p: (bi, 0, p))
    kvspec = pl.BlockSpec((None, s, LANES), lambda bi, p: (bi, 0, 0)) if shared else qspec
    bias = _window_bias(radius)
    in_specs = [pl.BlockSpec(bias.shape, lambda bi, p: (0, 0, 0)), qspec, kvspec, kvspec]
    args = [bias, q, k, v]
    if has_sink:
        in_specs = [pl.BlockSpec(memory_space=pltpu.SMEM)] + in_specs
        args = [sink] + args
    scratch = [pltpu.VMEM((win, 2 * WIN_Q), F32)] * depth
    if q.dtype == F32:
        scratch += [pltpu.VMEM((s, LANES), BF16)] * 3
    if dil > 1:
        scratch += [pltpu.VMEM((s, LANES), F32)]
    o, lse = pl.pallas_call(
        functools.partial(_window_attn_kernel, radius=radius, dil=dil, per_trip=per_trip, depth=depth,
                          has_sink=has_sink),
        grid=(b, npairs),
        in_specs=in_specs,
        out_specs=[qspec, pl.BlockSpec((None, None, 2, nqb, WIN_Q), lambda bi, p: (bi, p, 0, 0, 0))],
        out_shape=[jax.ShapeDtypeStruct((b, s, nq), BF16),
                   jax.ShapeDtypeStruct((b, npairs, 2, nqb, WIN_Q), F32)],
        scratch_shapes=scratch,
        compiler_params=_params("parallel", "parallel"),
        name="window_attention",
    )(*args)
    lse = lse.reshape(b, 2 * npairs, dil, seq).transpose(0, 3, 2, 1).reshape(b, s, 2 * npairs)
    return o, lse


def _mixer_out(mixer_refs, wout_ref, mode):
    if mode == "feature_major":
        (oT_ref,) = mixer_refs
        ys = [lax.dot_general(oT_ref[j], wout_ref[...], _TN, preferred_element_type=F32)
              for j in range(oT_ref.shape[0])]
        return jnp.concatenate(ys, axis=0) if len(ys) > 1 else ys[0]
    if mode == "token_major":
        (o_ref,) = mixer_refs
        return jnp.dot(o_ref[...], wout_ref[...], preferred_element_type=F32)
    n = (len(mixer_refs) - 1) // 2
    o_refs, lse_refs, expand_ref = mixer_refs[:n], mixer_refs[n:2 * n], mixer_refs[2 * n]
    lses = [r[...] for r in lse_refs]
    top = functools.reduce(jnp.maximum, lses)
    es = [jnp.exp2(x - top) for x in lses]
    z = functools.reduce(jnp.add, es)
    o = None
    for e, o_ref in zip(es, o_refs):
        w = e / z
        hi = w.astype(BF16)
        lo = (w - hi.astype(F32)).astype(BF16)
        wide = jnp.dot(jnp.concatenate([hi, lo], axis=1), expand_ref[...], preferred_element_type=F32)
        term = wide * o_ref[...].astype(F32)
        o = term if o is None else o + term
    return jnp.dot(o.astype(BF16), wout_ref[...], preferred_element_type=F32)


def _mid_kernel(*refs, mode, n_mixer, x_scale):
    h_ref = refs[0]
    mixer_refs = refs[1:1 + n_mixer]
    wout_ref, gmix_ref, gpre_ref, wq_ref, kT_ref, v_ref, wo_ref, gpost_ref, out_ref = refs[1 + n_mixer:]
    y = _mixer_out(mixer_refs, wout_ref, mode)
    h1 = h_ref[...] + _rms(y, gmix_ref[...])

    u = _rms(h1, gpre_ref[...]).astype(BF16)
    q = (jnp.dot(u, wq_ref[...], preferred_element_type=F32) * x_scale).astype(BF16)
    xd = q.shape[1] // X_HEADS
    outs = []
    for hd in range(X_HEADS):
        s = jnp.dot(q[:, hd * xd:(hd + 1) * xd], kT_ref[hd * xd:(hd + 1) * xd, :],
                    preferred_element_type=F32)
        p = jnp.exp2(s - jnp.max(s, axis=-1, keepdims=True))
        l = jnp.sum(p, axis=-1, keepdims=True)
        o = jnp.dot(p.astype(BF16), v_ref[:, hd * xd:(hd + 1) * xd], preferred_element_type=F32)
        outs.append((o / l).astype(BF16))
    y2 = jnp.dot(jnp.concatenate(outs, axis=1), wo_ref[...], preferred_element_type=F32)
    out_ref[...] = h1 + _rms(y2, gpost_ref[...])


def _mid(h, mixer, mode, wout, gmix, gpre, wq, kT, v, wo, gpost, *, x_scale):
    b, s, d = h.shape
    tm = ROW_TILE
    n_mem = v.shape[1]
    const = lambda bi, i: (0, 0)
    row = lambda bi, i: (bi, i, 0)
    if mode == "feature_major":
        oblk = mixer[0].shape[3]
        mixer_specs = [pl.BlockSpec((None, tm // oblk, d, oblk), lambda bi, i: (bi, i, 0, 0))]
    else:
        mixer_specs = [pl.BlockSpec((None, tm, a.shape[2]), row) for a in mixer]
    if mode == "branches":
        heads = mixer[-1].shape[2]
        expand = jnp.asarray(np.tile(np.repeat(np.eye(heads), d // heads, axis=1), (2, 1)), BF16)
        mixer = list(mixer) + [expand]
        mixer_specs.append(pl.BlockSpec(expand.shape, const))
    kern = functools.partial(_mid_kernel, mode=mode, n_mixer=len(mixer), x_scale=x_scale)
    return pl.pallas_call(
        kern,
        grid=(b, s // tm),
        in_specs=[
            pl.BlockSpec((None, tm, d), row),
            *mixer_specs,
            pl.BlockSpec((d, d), const),
            pl.BlockSpec((1, d), const),
            pl.BlockSpec((1, d), const),
            pl.BlockSpec((d, d), const),
            pl.BlockSpec((None, d, n_mem), lambda bi, i: (bi, 0, 0)),
            pl.BlockSpec((None, n_mem, d), lambda bi, i: (bi, 0, 0)),
            pl.BlockSpec((d, d), const),
            pl.BlockSpec((1, d), const),
        ],
        out_specs=pl.BlockSpec((None, tm, d), lambda bi, i: (bi, i, 0)),
        out_shape=jax.ShapeDtypeStruct((b, s, d), F32),
        compiler_params=_params("parallel", "parallel"),
        name="out_proj_cross_attention",
    )(h, *mixer, wout, gmix, gpre, wq, kT, v, wo, gpost)


def _mem_kv_kernel(mem_ref, g_ref, wkT_ref, wv_ref, kT_ref, v_ref):
    mn = _rms(mem_ref[...], g_ref[...]).astype(BF16)
    kT_ref[...] = lax.dot_general(wkT_ref[...], mn, _NT, preferred_element_type=F32).astype(BF16)
    v_ref[...] = jnp.dot(mn, wv_ref[...], preferred_element_type=F32).astype(BF16)


def _mem_kv(mem, g, wkT, wv):
    depth, d = g.shape[0], g.shape[2]
    b, n_mem, _ = mem.shape
    return pl.pallas_call(
        _mem_kv_kernel,
        grid=(depth, b),
        in_specs=[
            pl.BlockSpec((None, n_mem, d), lambda li, bi: (bi, 0, 0)),
            pl.BlockSpec((None, 1, d), lambda li, bi: (li, 0, 0)),
            pl.BlockSpec((None, d, d), lambda li, bi: (li, 0, 0)),
            pl.BlockSpec((None, d, d), lambda li, bi: (li, 0, 0)),
        ],
        out_specs=[
            pl.BlockSpec((None, None, d, n_mem), lambda li, bi: (li, bi, 0, 0)),
            pl.BlockSpec((None, None, n_mem, d), lambda li, bi: (li, bi, 0, 0)),
        ],
        out_shape=[
            jax.ShapeDtypeStruct((depth, b, d, n_mem), BF16),
            jax.ShapeDtypeStruct((depth, b, n_mem, d), BF16),
        ],
        compiler_params=_params("parallel", "parallel"),
        name="memory_kv",
    )(mem, g, wkT, wv)


def _ffn_kernel(h_ref, gpre_ref, wgu_ref, wd_ref, gpost_ref, out_ref, acc_ref):
    tm = h_ref.shape[0]
    dff = wd_ref.shape[0]
    fc = dff // FF_CHUNKS
    rows = tm // FF_SUBTILES
    sub = [slice(i * rows, (i + 1) * rows) for i in range(FF_SUBTILES)]
    us = [_rms(h_ref[r, :], gpre_ref[...]).astype(BF16) for r in sub]
    for c in range(FF_CHUNKS):
        acts = []
        for u in us:
            g = jnp.dot(u, wgu_ref[:, c * fc:(c + 1) * fc], preferred_element_type=F32)
            up = jnp.dot(u, wgu_ref[:, dff + c * fc:dff + (c + 1) * fc], preferred_element_type=F32)
            acts.append((g / (1.0 + jnp.exp(-g)) * up).astype(BF16))
        for i, a in enumerate(acts):
            part = jnp.dot(a, wd_ref[c * fc:(c + 1) * fc, :], preferred_element_type=F32)
            acc_ref[i] = part if c == 0 else acc_ref[i] + part
    for i, r in enumerate(sub):
        out_ref[r, :] = h_ref[r, :] + _rms(acc_ref[i], gpost_ref[...])


def _ffn(h, gpre, wgu, wd, gpost):
    b, s, d = h.shape
    tm = ROW_TILE
    dff = wd.shape[0]
    rows = b * s
    h2 = h.reshape(rows, d)
    const = lambda i: (0, 0)
    resident = dict(pipeline_mode=pl.Buffered(1))
    out = pl.pallas_call(
        _ffn_kernel,
        grid=(rows // tm,),
        in_specs=[
            pl.BlockSpec((tm, d), lambda i: (i, 0)),
            pl.BlockSpec((1, d), const),
            pl.BlockSpec((d, 2 * dff), const, **resident),
            pl.BlockSpec((dff, d), const, **resident),
            pl.BlockSpec((1, d), const),
        ],
        out_specs=pl.BlockSpec((tm, d), lambda i: (i, 0)),
        out_shape=jax.ShapeDtypeStruct((rows, d), F32),
        scratch_shapes=[pltpu.VMEM((FF_SUBTILES, tm // FF_SUBTILES, d), F32)],
        compiler_params=_params("parallel"),
        name="swiglu_ffn",
    )(h2, gpre, wgu, wd, gpost)
    return out.reshape(b, s, d)


def _rope_tables(positions):
    inv_freq = ROPE_THETA ** (-jnp.arange(0, 2 * ROT_HALF, 2, dtype=F32) / (2 * ROT_HALF))
    ang = positions.astype(F32)[..., None] * inv_freq
    cos, sin = jnp.cos(ang), jnp.sin(ang)
    cosT, sinT = cos.transpose(0, 2, 1), sin.transpose(0, 2, 1)
    zeros = jnp.zeros_like(cos)
    pad = HEAD_DIM - 2 * ROT_HALF
    ones_tail = jnp.ones(cos.shape[:-1] + (pad,), F32)
    zero_tail = jnp.zeros(cos.shape[:-1] + (pad,), F32)
    reps = LANES // HEAD_DIM
    kc = jnp.tile(jnp.concatenate([cos, cos, ones_tail], axis=-1), reps)
    ka = jnp.tile(jnp.concatenate([-sin, zeros, zero_tail], axis=-1), reps)
    kb = jnp.tile(jnp.concatenate([zeros, sin, zero_tail], axis=-1), reps)
    return cosT, sinT, kc, ka, kb


def _row(g):
    return g.reshape(1, -1)


def kernel(x, mem, positions, mix_pre_g, mix_post_g, mem_pre_g, mem_kv_g, mem_post_g, ffn_pre_g, ffn_post_g,
           a_w_in, a_w_out, b_w_in, b_w_out, b_lam_q1, b_lam_k1, b_lam_q2, b_lam_k2, b_sub_g, c_w_in, c_w_out,
           c_sink, x_wq, x_wkv, x_wo, w_gate_up, w_down):
    depth, d = mix_pre_g.shape
    assert d % (2 * HEAD_DIM) == 0 and x.shape[1] % DENSE_BLOCK == 0 and x.shape[1] % ROW_TILE == 0
    cosT, sinT, kc, ka, kb = _rope_tables(positions)
    q_scale = HEAD_DIM ** -0.5 * LOG2E
    x_scale = (d // X_HEADS) ** -0.5 * LOG2E

    mem_kT, mem_v = _mem_kv(mem, mem_kv_g.reshape(depth, 1, d),
                            x_wkv[:, :, :d].transpose(0, 2, 1).astype(BF16), x_wkv[:, :, d:].astype(BF16))

    h = x
    for i in range(depth):
        kind, j = i % N_MIXERS, i // N_MIXERS
        g_pre = _row(mix_pre_g[i])
        if kind == 0:
            w_in, w_out = a_w_in[j], a_w_out[j]
            q, k, v = _in_proj_tm(h, g_pre, w_in.astype(BF16), kc, ka, kb, nq=d, nk=d, q_scale=q_scale,
                                  out_dtype=F32)
            branches = [_window_attention(q, k, v, None, radius=window // (2 * dil), dil=dil)
                        for window, dil in A_PATTERNS]
            mixer, mode = [o for o, _ in branches] + [lse for _, lse in branches], "branches"
        elif kind == 1:
            w_in, w_out = b_w_in[j], b_w_out[j]
            wq, wk, wv = w_in[:, :d], w_in[:, d:2 * d], w_in[:, 2 * d:]
            qT, k, vT = _in_proj(h, g_pre, wq.T.astype(BF16), wk.astype(BF16), wv.T.astype(BF16),
                                 cosT, sinT, kc, ka, kb, q_scale=q_scale, qblk=DENSE_BLOCK, vblk=DENSE_BLOCK)
            lam_init = 0.8 - 0.6 * math.exp(-0.3 * i)
            lamv = jnp.stack([b_lam_q1[j], b_lam_k1[j], b_lam_q2[j], b_lam_k2[j]]).astype(F32)
            subg = jnp.broadcast_to(b_sub_g[j].astype(F32)[:, None], (2 * HEAD_DIM, DENSE_BLOCK))
            mixer, mode = [_diff_attention(qT, k, vT, lamv, subg, lam_init=lam_init)], "feature_major"
        else:
            w_in, w_out = c_w_in[j], c_w_out[j]
            n_kv = (w_in.shape[1] - d) // (2 * HEAD_DIM)
            grp = (d // HEAD_DIM) // n_kv
            perm = np.arange(d).reshape(n_kv, grp, HEAD_DIM).transpose(1, 0, 2).reshape(-1)
            w_in = jnp.concatenate([w_in[:, :d][:, perm], w_in[:, d:]], axis=1)
            w_out = w_out[perm, :]
            q, k, v = _in_proj_tm(h, g_pre, w_in.astype(BF16), kc, ka, kb, nq=d, nk=n_kv * HEAD_DIM,
                                  q_scale=q_scale, out_dtype=BF16)
            sink = (c_sink[j].astype(F32) * LOG2E)[perm[::HEAD_DIM] // HEAD_DIM]
            mixer, mode = [_window_attention(q, k, v, sink, radius=C_RADIUS, dil=1)[0]], "token_major"
        h = _mid(h, mixer, mode, w_out.astype(BF16), _row(mix_post_g[i]), _row(mem_pre_g[i]),
                 x_wq[i].astype(BF16), mem_kT[i], mem_v[i], x_wo[i].astype(BF16), _row(mem_post_g[i]),
                 x_scale=x_scale)
        h = _ffn(h, _row(ffn_pre_g[i]), w_gate_up[i].astype(BF16), w_down[i].astype(BF16), _row(ffn_post_g[i]))
    return h
```

```python
import functools
import math

import jax
import jax.numpy as jnp
import numpy as np
from jax import lax
from jax.experimental import pallas as pl
from jax.experimental.pallas import tpu as pltpu

F32 = jnp.float32
BF16 = jnp.bfloat16

HEAD_DIM = 64
ROT_HALF = HEAD_DIM // 8
ROPE_THETA = 500000.0
EPS = 1e-6
NEG_INF = -1e30
LOG2E = 1.4426950408889634
N_MIXERS = 3

A_PATTERNS = ((128, 1), (512, 4), (2048, 16))
C_RADIUS = 128
X_HEADS = 4

LANES = 128
ROW_TILE = 512
DENSE_BLOCK = 512
ATTN_CHUNK = 256
ATTN_LEAD = 1
DENSE_PER_TRIP = 4
WIN_Q = 128
WIN_PER_TRIP = 16
REGROUP_STRIDE = 4
WIN_DEPTH = 4
FF_CHUNK = 256
VMEM_LIMIT = 56 * 1024 * 1024

_NT = (((1,), (1,)), ((), ()))
_TN = (((0,), (0,)), ((), ()))


def _params(*sem):
    return pltpu.CompilerParams(dimension_semantics=sem, vmem_limit_bytes=VMEM_LIMIT)


def _rms(x, g):
    ms = jnp.mean(x * x, axis=-1, keepdims=True)
    return x * lax.rsqrt(ms + EPS) * g


def _in_proj_kernel(h_ref, g_ref, wqT_ref, wk_ref, wvT_ref, cosT_ref, sinT_ref, kc_ref, ka_ref, kb_ref,
                    qT_ref, k_ref, vT_ref, *, q_scale, qblk, vblk):
    tm = h_ref.shape[0]
    u = _rms(h_ref[...], g_ref[...]).astype(BF16)

    kf = jnp.dot(u, wk_ref[...], preferred_element_type=F32)
    kc, ka, kb = kc_ref[...], ka_ref[...], kb_ref[...]
    for j in range(kf.shape[1] // LANES):
        x = kf[:, j * LANES:(j + 1) * LANES]
        y = x * kc + pltpu.roll(x, LANES - ROT_HALF, 1) * ka + pltpu.roll(x, ROT_HALF, 1) * kb
        k_ref[:, j * LANES:(j + 1) * LANES] = y.astype(BF16)

    qf = lax.dot_general(wqT_ref[...], u, _NT, preferred_element_type=F32)
    c = cosT_ref[...] * q_scale
    s = sinT_ref[...] * q_scale
    for unit in range(qf.shape[0] // HEAD_DIM):
        r0 = unit * HEAD_DIM
        t1 = qf[r0:r0 + ROT_HALF]
        t2 = qf[r0 + ROT_HALF:r0 + 2 * ROT_HALF]
        rest = qf[r0 + 2 * ROT_HALF:r0 + HEAD_DIM] * q_scale
        blk = jnp.concatenate([t1 * c - t2 * s, t2 * c + t1 * s, rest], axis=0).astype(BF16)
        for jb in range(tm // qblk):
            qT_ref[jb, r0:r0 + HEAD_DIM, :] = blk[:, jb * qblk:(jb + 1) * qblk]

    vf = lax.dot_general(wvT_ref[...], u, _NT, preferred_element_type=F32).astype(BF16)
    for jb in range(tm // vblk):
        vT_ref[jb] = vf[:, jb * vblk:(jb + 1) * vblk]


def _in_proj(h, g, wqT, wk, wvT, cosT, sinT, kc, ka, kb, *, q_scale, qblk, vblk):
    b, s, d = h.shape
    nq, nk, nv = wqT.shape[0], wk.shape[1], wvT.shape[0]
    tm = ROW_TILE
    kern = functools.partial(_in_proj_kernel, q_scale=q_scale, qblk=qblk, vblk=vblk)
    const = lambda bi, i: (0, 0)
    return pl.pallas_call(
        kern,
        grid=(b, s // tm),
        in_specs=[
            pl.BlockSpec((None, tm, d), lambda bi, i: (bi, i, 0)),
            pl.BlockSpec((1, d), const),
            pl.BlockSpec((nq, d), const),
            pl.BlockSpec((d, nk), const),
            pl.BlockSpec((nv, d), const),
            pl.BlockSpec((None, ROT_HALF, tm), lambda bi, i: (bi, 0, i)),
            pl.BlockSpec((None, ROT_HALF, tm), lambda bi, i: (bi, 0, i)),
            pl.BlockSpec((None, tm, LANES), lambda bi, i: (bi, i, 0)),
            pl.BlockSpec((None, tm, LANES), lambda bi, i: (bi, i, 0)),
            pl.BlockSpec((None, tm, LANES), lambda bi, i: (bi, i, 0)),
        ],
        out_specs=[
            pl.BlockSpec((None, tm // qblk, nq, qblk), lambda bi, i: (bi, i, 0, 0)),
            pl.BlockSpec((None, tm, nk), lambda bi, i: (bi, i, 0)),
            pl.BlockSpec((None, tm // vblk, nv, vblk), lambda bi, i: (bi, i, 0, 0)),
        ],
        out_shape=[
            jax.ShapeDtypeStruct((b, s // qblk, nq, qblk), BF16),
            jax.ShapeDtypeStruct((b, s, nk), BF16),
            jax.ShapeDtypeStruct((b, s // vblk, nv, vblk), BF16),
        ],
        compiler_params=_params("parallel", "parallel"),
        name="mixer_in_proj",
    )(h, g, wqT, wk, wvT, cosT, sinT, kc, ka, kb)


def _stage_queries(qT_ref, qz_ref, t):
    q = qT_ref[...]
    row = lax.broadcasted_iota(jnp.int32, q.shape, 0)
    zero = jnp.zeros_like(q)
    qz_ref[:, 0:t] = jnp.where(row < HEAD_DIM, q, zero)
    qz_ref[:, t:2 * t] = jnp.where(row >= HEAD_DIM, q, zero)


def _scores_chunk(k_ref, kb, qz_ref, dst, c, t, cw):
    s_ref, top_ref = dst
    kblk = k_ref[pl.ds(pl.multiple_of(kb * t, t), t), :]
    cols = slice(c * cw, (c + 1) * cw)
    s = jnp.dot(kblk, qz_ref[:, cols], preferred_element_type=F32)
    s_ref[:, cols] = s
    top_ref[:, cols] = jnp.max(s, axis=0, keepdims=True)


def _block_step(kb, src, kb_next, dst, refs, *, t, cw, lead):
    k_ref, vT_ref, qz_ref, m_ref, l_ref, acc_ref = refs
    s_ref, top_ref = src
    nchunk = 2 * t // cw
    v = vT_ref[kb]
    if kb_next is not None:
        for c in range(lead):
            _scores_chunk(k_ref, kb_next, qz_ref, dst, c, t, cw)
    for c in range(nchunk):
        if kb_next is not None and c + lead < nchunk:
            _scores_chunk(k_ref, kb_next, qz_ref, dst, c + lead, t, cw)
        u, cc = divmod(c, t // cw)
        cols = slice(c * cw, (c + 1) * cw)
        acc_at = acc_ref.at[u, :, cc * cw:(cc + 1) * cw]
        m_old = m_ref[:, cols]
        m_new = jnp.maximum(m_old, top_ref[:, cols])
        alpha = jnp.exp2(m_old - m_new)
        p = jnp.exp2(s_ref[:, cols] - m_new)
        l_ref[:, cols] = alpha * l_ref[:, cols] + jnp.sum(p, axis=0, keepdims=True)
        m_ref[:, cols] = m_new
        acc_at[...] = alpha * acc_at[...] + jnp.dot(v, p.astype(BF16), preferred_element_type=F32)


def _diff_attn_kernel(lam_ref, subg_ref, qT_ref, k_ref, vT_ref, oT_ref, qz_ref, m_ref, l_ref, acc_ref,
                      sa_ref, sb_ref, ta_ref, tb_ref, *, t, nkb, cw, lead, per_trip, lam_init):
    _stage_queries(qT_ref, qz_ref, t)
    m_ref[...] = jnp.full(m_ref.shape, NEG_INF, F32)
    l_ref[...] = jnp.zeros(l_ref.shape, F32)
    acc_ref[...] = jnp.zeros(acc_ref.shape, F32)

    bufs = ((sa_ref, ta_ref), (sb_ref, tb_ref))
    step = functools.partial(_block_step, refs=(k_ref, vT_ref, qz_ref, m_ref, l_ref, acc_ref),
                             t=t, cw=cw, lead=lead)

    for c in range(2 * t // cw):
        _scores_chunk(k_ref, 0, qz_ref, bufs[0], c, t, cw)

    def trip(j, carry):
        for i in range(per_trip):
            step(per_trip * j + i, bufs[i % 2], per_trip * j + i + 1, bufs[(i + 1) % 2])
        return carry

    ntrip = nkb // per_trip - 1
    lax.fori_loop(0, ntrip, trip, 0)
    for i in range(per_trip):
        kb = ntrip * per_trip + i
        step(kb, bufs[i % 2], kb + 1 if i + 1 < per_trip else None, bufs[(i + 1) % 2])

    lv = lam_ref[...]
    e1 = jnp.exp(jnp.sum(lv[0:1] * lv[1:2], axis=-1, keepdims=True))
    e2 = jnp.exp(jnp.sum(lv[2:3] * lv[3:4], axis=-1, keepdims=True))
    lam = e1 - e2 + lam_init
    o = acc_ref[0] / l_ref[:, 0:t] - lam * (acc_ref[1] / l_ref[:, t:2 * t])
    ms = jnp.mean(o * o, axis=0, keepdims=True)
    o = o * lax.rsqrt(ms + EPS) * subg_ref[...] * (1.0 - lam_init)
    oT_ref[...] = o.astype(BF16)


def _diff_attention(qT, k, vT, lamv, subg, *, lam_init):
    b, nqb, nq, t = qT.shape
    s = k.shape[1]
    nkb = s // t
    heads = nq // (2 * HEAD_DIM)
    assert nkb % DENSE_PER_TRIP == 0 and DENSE_PER_TRIP % 2 == 0
    kern = functools.partial(_diff_attn_kernel, t=t, nkb=nkb, cw=ATTN_CHUNK, lead=ATTN_LEAD,
                             per_trip=DENSE_PER_TRIP, lam_init=lam_init)
    return pl.pallas_call(
        kern,
        grid=(b, heads, nqb),
        in_specs=[
            pl.BlockSpec(lamv.shape, lambda bi, h, i: (0, 0)),
            pl.BlockSpec(subg.shape, lambda bi, h, i: (0, 0)),
            pl.BlockSpec((None, None, 2 * HEAD_DIM, t), lambda bi, h, i: (bi, i, h, 0)),
            pl.BlockSpec((None, s, 2 * HEAD_DIM), lambda bi, h, i: (bi, 0, h)),
            pl.BlockSpec((None, nkb, 2 * HEAD_DIM, t), lambda bi, h, i: (bi, 0, h, 0)),
        ],
        out_specs=pl.BlockSpec((None, None, 2 * HEAD_DIM, t), lambda bi, h, i: (bi, i, h, 0)),
        out_shape=jax.ShapeDtypeStruct((b, nqb, nq, t), BF16),
        scratch_shapes=[
            pltpu.VMEM((2 * HEAD_DIM, 2 * t), BF16),
            pltpu.VMEM((1, 2 * t), F32),
            pltpu.VMEM((1, 2 * t), F32),
            pltpu.VMEM((2, 2 * HEAD_DIM, t), F32),
            pltpu.VMEM((t, 2 * t), F32),
            pltpu.VMEM((t, 2 * t), F32),
            pltpu.VMEM((1, 2 * t), F32),
            pltpu.VMEM((1, 2 * t), F32),
        ],
        compiler_params=_params("parallel", "parallel", "arbitrary"),
        name="diff_attention",
    )(lamv, subg, qT, k, vT)


def _in_proj_tm_kernel(h_ref, g_ref, w_ref, kc_ref, ka_ref, kb_ref, q_ref, k_ref, v_ref, *, q_scale):
    u = _rms(h_ref[...], g_ref[...]).astype(BF16)
    y = jnp.dot(u, w_ref[...], preferred_element_type=F32)
    kc, ka, kb = kc_ref[...], ka_ref[...], kb_ref[...]
    nq, nk = q_ref.shape[1], k_ref.shape[1]

    def rope(j):
        x = y[:, j * LANES:(j + 1) * LANES]
        return x * kc + pltpu.roll(x, LANES - ROT_HALF, 1) * ka + pltpu.roll(x, ROT_HALF, 1) * kb

    for j in range(nq // LANES):
        q_ref[:, j * LANES:(j + 1) * LANES] = (rope(j) * q_scale).astype(q_ref.dtype)
    for j in range(nk // LANES):
        k_ref[:, j * LANES:(j + 1) * LANES] = rope(nq // LANES + j).astype(k_ref.dtype)
    v_ref[...] = y[:, nq + nk:].astype(v_ref.dtype)


def _in_proj_tm(h, g, w, kc, ka, kb, *, nq, nk, q_scale, out_dtype):
    b, s, d = h.shape
    nv = w.shape[1] - nq - nk
    tm = ROW_TILE
    const = lambda bi, i: (0, 0)
    row = lambda bi, i: (bi, i, 0)
    return pl.pallas_call(
        functools.partial(_in_proj_tm_kernel, q_scale=q_scale),
        grid=(b, s // tm),
        in_specs=[
            pl.BlockSpec((None, tm, d), row),
            pl.BlockSpec((1, d), const),
            pl.BlockSpec(w.shape, const),
            pl.BlockSpec((None, tm, LANES), row),
            pl.BlockSpec((None, tm, LANES), row),
            pl.BlockSpec((None, tm, LANES), row),
        ],
        out_specs=[pl.BlockSpec((None, tm, n), row) for n in (nq, nk, nv)],
        out_shape=[jax.ShapeDtypeStruct((b, s, n), out_dtype) for n in (nq, nk, nv)],
        compiler_params=_params("parallel", "parallel"),
        name="mixer_in_proj_tm",
    )(h, g, w, kc, ka, kb)


def _window_attn_kernel(*refs, radius, dil, per_trip, depth, has_sink):
    if has_sink:
        sink_ref, refs = refs[0], refs[1:]
    bias_ref, q_ref, k_ref, v_ref, o_ref, lse_ref, *scratch = refs
    bufs, scratch = list(zip(scratch[:depth], scratch[depth:2 * depth])), scratch[2 * depth:]
    pair = pl.program_id(1)
    total = q_ref.shape[0]
    seq = total // dil
    win = WIN_Q + 2 * radius
    nqb = total // WIN_Q
    lane = lax.broadcasted_iota(jnp.int32, (WIN_Q, 2 * HEAD_DIM), 1)

    if q_ref.dtype == F32:
        qg_ref, kg_ref, vg_ref = scratch[:3]
        onat_ref = scratch[3] if dil > 1 else None
        for src, dst in ((q_ref, qg_ref), (k_ref, kg_ref), (v_ref, vg_ref)):
            if dil > REGROUP_STRIDE:
                assert dil == REGROUP_STRIDE ** 2
                part = total // REGROUP_STRIDE
                for r1 in range(REGROUP_STRIDE):
                    onat_ref[r1 * part:(r1 + 1) * part, :] = src[pl.ds(r1, part, stride=REGROUP_STRIDE), :]
                src, outer = onat_ref, REGROUP_STRIDE
            else:
                outer = 1
            for r in range(dil):
                r1, r2 = r % outer, r // outer
                rows = pl.ds(r1 * (total // outer) + r2, seq, stride=dil // outer) if dil > 1 else slice(None)
                dst[r * seq:(r + 1) * seq, :] = src[rows, :].astype(BF16)
    else:
        qg_ref, kg_ref, vg_ref = q_ref, k_ref, v_ref

    def window(i):
        q0 = i * WIN_Q
        lo = (q0 // seq) * seq
        k0 = jnp.clip(q0 - radius, lo, lo + seq - win)
        return pl.multiple_of(q0, WIN_Q), pl.multiple_of(k0, radius), (q0 - k0) // radius

    def scores(i, dst):
        s_ref, top_ref = dst
        q0, k0, bidx = window(i)
        q = qg_ref[pl.ds(q0, WIN_Q), :]
        zero = jnp.zeros_like(q)
        qz = jnp.concatenate([jnp.where(lane < HEAD_DIM, q, zero), jnp.where(lane >= HEAD_DIM, q, zero)],
                             axis=0)
        bias = bias_ref[bidx]
        s = lax.dot_general(kg_ref[pl.ds(k0, win), :], qz, _NT, preferred_element_type=F32)
        s = s + jnp.concatenate([bias, bias], axis=1)
        s_ref[...] = s
        top_ref[...] = jnp.max(s, axis=0, keepdims=True)

    def finish(i, src):
        s_ref, top_ref = src
        q0, k0, _ = window(i)
        s = s_ref[...]
        m = top_ref[...]
        if has_sink:
            unit = lax.broadcasted_iota(jnp.int32, m.shape, 1) // WIN_Q
            sk = jnp.where(unit == 0, sink_ref[pair * 2], sink_ref[pair * 2 + 1])
            m = jnp.maximum(m, sk)
        p = jnp.exp2(s - m)
        l = jnp.sum(p, axis=0, keepdims=True)
        if has_sink:
            l = l + jnp.exp2(sk - m)
        oT = lax.dot_general(vg_ref[pl.ds(k0, win), :], p.astype(BF16), _TN, preferred_element_type=F32)
        lse = m + jnp.log2(l)
        halves = []
        for u in range(2):
            cols = slice(u * WIN_Q, (u + 1) * WIN_Q)
            halves.append(oT[u * HEAD_DIM:(u + 1) * HEAD_DIM, cols] / l[:, cols])
            lse_ref[u, pl.ds(i, 1), :] = lse[:, cols]
        o = jnp.concatenate(halves, axis=0).T
        if dil > 1:
            r = q0 // seq
            onat_ref[pl.ds(r + (q0 - r * seq) * dil, WIN_Q, stride=dil), :] = o
        else:
            o_ref[pl.ds(q0, WIN_Q), :] = o.astype(BF16)

    for n in range(depth - 1):
        scores(n, bufs[n])

    def trip(j, carry):
        for n in range(per_trip):
            i = per_trip * j + n
            scores(i + depth - 1, bufs[(n + depth - 1) % depth])
            finish(i, bufs[n % depth])
        return carry

    ntrip = nqb // per_trip - 1
    lax.fori_loop(0, ntrip, trip, 0)
    for n in range(per_trip):
        i = ntrip * per_trip + n
        if n + depth - 1 < per_trip:
            scores(i + depth - 1, bufs[(n + depth - 1) % depth])
        finish(i, bufs[n % depth])
    if dil > 1:
        o_ref[...] = onat_ref[...].astype(BF16)


def _window_bias(radius):
    win = WIN_Q + 2 * radius
    i = np.arange(win)[:, None]
    j = np.arange(WIN_Q)[None, :]
    return jnp.asarray(np.stack([np.where(np.abs(i - j - b * radius) <= radius, 0.0, NEG_INF)
                                 for b in range(3)]), F32)


def _window_attention(q, k, v, sink, *, radius, dil):
    b, s, nq = q.shape
    seq = s // dil
    nqb = s // WIN_Q
    npairs = nq // LANES
    win = WIN_Q + 2 * radius
    shared = k.shape[2] == LANES and npairs > 1
    has_sink = sink is not None
    per_trip = min(WIN_PER_TRIP, nqb)
    depth = min(WIN_DEPTH, per_trip)
    assert seq % WIN_Q == 0 and seq >= win and WIN_Q % radius == 0
    assert nqb % per_trip == 0 and (per_trip % depth == 0 or per_trip == nqb)
    assert q.dtype == F32 or dil == 1
    qspec = pl.BlockSpec((None, s, LANES), lambda bi, p: (bi, 0, p))
    kvspec = pl.BlockSpec((None, s, LANES), lambda bi, p: (bi, 0, 0)) if shared else qspec
    bias = _window_bias(radius)
    in_specs = [pl.BlockSpec(bias.shape, lambda bi, p: (0, 0, 0)), qspec, kvspec, kvspec]
    args = [bias, q, k, v]
    if has_sink:
        in_specs = [pl.BlockSpec(memory_space=pltpu.SMEM)] + in_specs
        args = [sink] + args
    scratch = [pltpu.VMEM((win, 2 * WIN_Q), F32)] * depth + [pltpu.VMEM((1, 2 * WIN_Q), F32)] * depth
    if q.dtype == F32:
        scratch += [pltpu.VMEM((s, LANES), BF16)] * 3
    if dil > 1:
        scratch += [pltpu.VMEM((s, LANES), F32)]
    o, lse = pl.pallas_call(
        functools.partial(_window_attn_kernel, radius=radius, dil=dil, per_trip=per_trip, depth=depth,
                          has_sink=has_sink),
        grid=(b, npairs),
        in_specs=in_specs,
        out_specs=[qspec, pl.BlockSpec((None, None, 2, nqb, WIN_Q), lambda bi, p: (bi, p, 0, 0, 0))],
        out_shape=[jax.ShapeDtypeStruct((b, s, nq), BF16),
                   jax.ShapeDtypeStruct((b, npairs, 2, nqb, WIN_Q), F32)],
        scratch_shapes=scratch,
        compiler_params=_params("parallel", "parallel"),
        name="window_attention",
    )(*args)
    lse = lse.reshape(b, 2 * npairs, dil, seq).transpose(0, 3, 2, 1).reshape(b, s, 2 * npairs)
    return o, lse


def _mixer_out(mixer_refs, wout_ref, mode):
    if mode == "feature_major":
        (oT_ref,) = mixer_refs
        ys = [lax.dot_general(oT_ref[j], wout_ref[...], _TN, preferred_element_type=F32)
              for j in range(oT_ref.shape[0])]
        return jnp.concatenate(ys, axis=0) if len(ys) > 1 else ys[0]
    if mode == "token_major":
        (o_ref,) = mixer_refs
        return jnp.dot(o_ref[...], wout_ref[...], preferred_element_type=F32)
    n = (len(mixer_refs) - 1) // 2
    o_refs, lse_refs, expand_ref = mixer_refs[:n], mixer_refs[n:2 * n], mixer_refs[2 * n]
    lses = [r[...] for r in lse_refs]
    top = functools.reduce(jnp.maximum, lses)
    es = [jnp.exp2(x - top) for x in lses]
    z = functools.reduce(jnp.add, es)
    o = None
    for e, o_ref in zip(es, o_refs):
        w = e / z
        hi = w.astype(BF16)
        lo = (w - hi.astype(F32)).astype(BF16)
        wide = jnp.dot(jnp.concatenate([hi, lo], axis=1), expand_ref[...], preferred_element_type=F32)
        term = wide * o_ref[...].astype(F32)
        o = term if o is None else o + term
    return jnp.dot(o.astype(BF16), wout_ref[...], preferred_element_type=F32)


def _mid_kernel(*refs, mode, n_mixer, x_scale):
    h_ref = refs[0]
    mixer_refs = refs[1:1 + n_mixer]
    wout_ref, gmix_ref, gpre_ref, wq_ref, kT_ref, v_ref, wo_ref, gpost_ref, out_ref = refs[1 + n_mixer:]
    y = _mixer_out(mixer_refs, wout_ref, mode)
    h1 = h_ref[...] + _rms(y, gmix_ref[...])

    u = _rms(h1, gpre_ref[...]).astype(BF16)
    q = (jnp.dot(u, wq_ref[...], preferred_element_type=F32) * x_scale).astype(BF16)
    xd = q.shape[1] // X_HEADS
    outs = []
    for hd in range(X_HEADS):
        s = jnp.dot(q[:, hd * xd:(hd + 1) * xd], kT_ref[hd * xd:(hd + 1) * xd, :],
                    preferred_element_type=F32)
        p = jnp.exp2(s - jnp.max(s, axis=-1, keepdims=True))
        l = jnp.sum(p, axis=-1, keepdims=True)
        o = jnp.dot(p.astype(BF16), v_ref[:, hd * xd:(hd + 1) * xd], preferred_element_type=F32)
        outs.append((o / l).astype(BF16))
    y2 = jnp.dot(jnp.concatenate(outs, axis=1), wo_ref[...], preferred_element_type=F32)
    out_ref[...] = h1 + _rms(y2, gpost_ref[...])


def _mid(h, mixer, mode, wout, gmix, gpre, wq, kT, v, wo, gpost, *, x_scale):
    b, s, d = h.shape
    tm = ROW_TILE
    n_mem = v.shape[1]
    const = lambda bi, i: (0, 0)
    row = lambda bi, i: (bi, i, 0)
    if mode == "feature_major":
        oblk = mixer[0].shape[3]
        mixer_specs = [pl.BlockSpec((None, tm // oblk, d, oblk), lambda bi, i: (bi, i, 0, 0))]
    else:
        mixer_specs = [pl.BlockSpec((None, tm, a.shape[2]), row) for a in mixer]
    if mode == "branches":
        heads = mixer[-1].shape[2]
        expand = jnp.asarray(np.tile(np.repeat(np.eye(heads), d // heads, axis=1), (2, 1)), BF16)
        mixer = list(mixer) + [expand]
        mixer_specs.append(pl.BlockSpec(expand.shape, const))
    kern = functools.partial(_mid_kernel, mode=mode, n_mixer=len(mixer), x_scale=x_scale)
    return pl.pallas_call(
        kern,
        grid=(b, s // tm),
        in_specs=[
            pl.BlockSpec((None, tm, d), row),
            *mixer_specs,
            pl.BlockSpec((d, d), const),
            pl.BlockSpec((1, d), const),
            pl.BlockSpec((1, d), const),
            pl.BlockSpec((d, d), const),
            pl.BlockSpec((None, d, n_mem), lambda bi, i: (bi, 0, 0)),
            pl.BlockSpec((None, n_mem, d), lambda bi, i: (bi, 0, 0)),
            pl.BlockSpec((d, d), const),
            pl.BlockSpec((1, d), const),
        ],
        out_specs=pl.BlockSpec((None, tm, d), lambda bi, i: (bi, i, 0)),
        out_shape=jax.ShapeDtypeStruct((b, s, d), F32),
        compiler_params=_params("parallel", "parallel"),
        name="out_proj_cross_attention",
    )(h, *mixer, wout, gmix, gpre, wq, kT, v, wo, gpost)


def _mem_kv_kernel(mem_ref, g_ref, wkT_ref, wv_ref, kT_ref, v_ref):
    mn = _rms(mem_ref[...], g_ref[...]).astype(BF16)
    kT_ref[...] = lax.dot_general(wkT_ref[...], mn, _NT, preferred_element_type=F32).astype(BF16)
    v_ref[...] = jnp.dot(mn, wv_ref[...], preferred_element_type=F32).astype(BF16)


def _mem_kv(mem, g, wkT, wv):
    depth, d = g.shape[0], g.shape[2]
    b, n_mem, _ = mem.shape
    return pl.pallas_call(
        _mem_kv_kernel,
        grid=(depth, b),
        in_specs=[
            pl.BlockSpec((None, n_mem, d), lambda li, bi: (bi, 0, 0)),
            pl.BlockSpec((None, 1, d), lambda li, bi: (li, 0, 0)),
            pl.BlockSpec((None, d, d), lambda li, bi: (li, 0, 0)),
            pl.BlockSpec((None, d, d), lambda li, bi: (li, 0, 0)),
        ],
        out_specs=[
            pl.BlockSpec((None, None, d, n_mem), lambda li, bi: (li, bi, 0, 0)),
            pl.BlockSpec((None, None, n_mem, d), lambda li, bi: (li, bi, 0, 0)),
        ],
        out_shape=[
            jax.ShapeDtypeStruct((depth, b, d, n_mem), BF16),
            jax.ShapeDtypeStruct((depth, b, n_mem, d), BF16),
        ],
        compiler_params=_params("parallel", "parallel"),
        name="memory_kv",
    )(mem, g, wkT, wv)


def _ffn_kernel(h_ref, gpre_ref, wgu_ref, wd_ref, gpost_ref, out_ref, acc_ref):
    dff = wd_ref.shape[0]
    u = _rms(h_ref[...], gpre_ref[...]).astype(BF16)
    for c in range(dff // FF_CHUNK):
        cols = slice(c * FF_CHUNK, (c + 1) * FF_CHUNK)
        g = jnp.dot(u, wgu_ref[:, cols], preferred_element_type=F32)
        up = jnp.dot(u, wgu_ref[:, dff + c * FF_CHUNK:dff + (c + 1) * FF_CHUNK], preferred_element_type=F32)
        a = (g / (1.0 + jnp.exp(-g)) * up).astype(BF16)
        part = jnp.dot(a, wd_ref[cols, :], preferred_element_type=F32)
        acc_ref[...] = part if c == 0 else acc_ref[...] + part
    out_ref[...] = h_ref[...] + _rms(acc_ref[...], gpost_ref[...])


def _ffn(h, gpre, wgu, wd, gpost):
    b, s, d = h.shape
    tm = ROW_TILE
    dff = wd.shape[0]
    assert dff % FF_CHUNK == 0
    rows = b * s
    h2 = h.reshape(rows, d)
    const = lambda i: (0, 0)
    resident = dict(pipeline_mode=pl.Buffered(1))
    out = pl.pallas_call(
        _ffn_kernel,
        grid=(rows // tm,),
        in_specs=[
            pl.BlockSpec((tm, d), lambda i: (i, 0)),
            pl.BlockSpec((1, d), const),
            pl.BlockSpec((d, 2 * dff), const, **resident),
            pl.BlockSpec((dff, d), const, **resident),
            pl.BlockSpec((1, d), const),
        ],
        out_specs=pl.BlockSpec((tm, d), lambda i: (i, 0)),
        out_shape=jax.ShapeDtypeStruct((rows, d), F32),
        scratch_shapes=[pltpu.VMEM((tm, d), F32)],
        compiler_params=_params("parallel"),
        name="swiglu_ffn",
    )(h2, gpre, wgu, wd, gpost)
    return out.reshape(b, s, d)


def _rope_tables(positions):
    inv_freq = ROPE_THETA ** (-jnp.arange(0, 2 * ROT_HALF, 2, dtype=F32) / (2 * ROT_HALF))
    ang = positions.astype(F32)[..., None] * inv_freq
    cos, sin = jnp.cos(ang), jnp.sin(ang)
    cosT, sinT = cos.transpose(0, 2, 1), sin.transpose(0, 2, 1)
    zeros = jnp.zeros_like(cos)
    pad = HEAD_DIM - 2 * ROT_HALF
    ones_tail = jnp.ones(cos.shape[:-1] + (pad,), F32)
    zero_tail = jnp.zeros(cos.shape[:-1] + (pad,), F32)
    reps = LANES // HEAD_DIM
    kc = jnp.tile(jnp.concatenate([cos, cos, ones_tail], axis=-1), reps)
    ka = jnp.tile(jnp.concatenate([-sin, zeros, zero_tail], axis=-1), reps)
    kb = jnp.tile(jnp.concatenate([zeros, sin, zero_tail], axis=-1), reps)
    return cosT, sinT, kc, ka, kb


def _row(g):
    return g.reshape(1, -1)


def kernel(x, mem, positions, mix_pre_g, mix_post_g, mem_pre_g, mem_kv_g, mem_post_g, ffn_pre_g, ffn_post_g,
           a_w_in, a_w_out, b_w_in, b_w_out, b_lam_q1, b_lam_k1, b_lam_q2, b_lam_k2, b_sub_g, c_w_in, c_w_out,
           c_sink, x_wq, x_wkv, x_wo, w_gate_up, w_down):
    depth, d = mix_pre_g.shape
    assert d % (2 * HEAD_DIM) == 0 and x.shape[1] % DENSE_BLOCK == 0 and x.shape[1] % ROW_TILE == 0
    cosT, sinT, kc, ka, kb = _rope_tables(positions)
    q_scale = HEAD_DIM ** -0.5 * LOG2E
    x_scale = (d // X_HEADS) ** -0.5 * LOG2E

    mem_kT, mem_v = _mem_kv(mem, mem_kv_g.reshape(depth, 1, d),
                            x_wkv[:, :, :d].transpose(0, 2, 1).astype(BF16), x_wkv[:, :, d:].astype(BF16))

    h = x
    for i in range(depth):
        kind, j = i % N_MIXERS, i // N_MIXERS
        g_pre = _row(mix_pre_g[i])
        if kind == 0:
            w_in, w_out = a_w_in[j], a_w_out[j]
            q, k, v = _in_proj_tm(h, g_pre, w_in.astype(BF16), kc, ka, kb, nq=d, nk=d, q_scale=q_scale,
                                  out_dtype=F32)
            branches = [_window_attention(q, k, v, None, radius=window // (2 * dil), dil=dil)
                        for window, dil in A_PATTERNS]
            mixer, mode = [o for o, _ in branches] + [lse for _, lse in branches], "branches"
        elif kind == 1:
            w_in, w_out = b_w_in[j], b_w_out[j]
            wq, wk, wv = w_in[:, :d], w_in[:, d:2 * d], w_in[:, 2 * d:]
            qT, k, vT = _in_proj(h, g_pre, wq.T.astype(BF16), wk.astype(BF16), wv.T.astype(BF16),
                                 cosT, sinT, kc, ka, kb, q_scale=q_scale, qblk=DENSE_BLOCK, vblk=DENSE_BLOCK)
            lam_init = 0.8 - 0.6 * math.exp(-0.3 * i)
            lamv = jnp.stack([b_lam_q1[j], b_lam_k1[j], b_lam_q2[j], b_lam_k2[j]]).astype(F32)
            subg = jnp.broadcast_to(b_sub_g[j].astype(F32)[:, None], (2 * HEAD_DIM, DENSE_BLOCK))
            mixer, mode = [_diff_attention(qT, k, vT, lamv, subg, lam_init=lam_init)], "feature_major"
        else:
            w_in, w_out = c_w_in[j], c_w_out[j]
            n_kv = (w_in.shape[1] - d) // (2 * HEAD_DIM)
            grp = (d // HEAD_DIM) // n_kv
            perm = np.arange(d).reshape(n_kv, grp, HEAD_DIM).transpose(1, 0, 2).reshape(-1)
            w_in = jnp.concatenate([w_in[:, :d][:, perm], w_in[:, d:]], axis=1)
            w_out = w_out[perm, :]
            q, k, v = _in_proj_tm(h, g_pre, w_in.astype(BF16), kc, ka, kb, nq=d, nk=n_kv * HEAD_DIM,
                                  q_scale=q_scale, out_dtype=BF16)
            sink = (c_sink[j].astype(F32) * LOG2E)[perm[::HEAD_DIM] // HEAD_DIM]
            mixer, mode = [_window_attention(q, k, v, sink, radius=C_RADIUS, dil=1)[0]], "token_major"
        h = _mid(h, mixer, mode, w_out.astype(BF16), _row(mix_post_g[i]), _row(mem_pre_g[i]),
                 x_wq[i].astype(BF16), mem_kT[i], mem_v[i], x_wo[i].astype(BF16), _row(mem_post_g[i]),
                 x_scale=x_scale)
        h = _ffn(h, _row(ffn_pre_g[i]), w_gate_up[i].astype(BF16), w_down[i].astype(BF16), _row(ffn_post_g[i]))
    return h
```

```python
import functools
import math

import jax
import jax.numpy as jnp
import numpy as np
from jax import lax
from jax.experimental import pallas as pl
from jax.experimental.pallas import tpu as pltpu

F32 = jnp.float32
BF16 = jnp.bfloat16

HEAD_DIM = 64
ROT_HALF = HEAD_DIM // 8
ROPE_THETA = 500000.0
EPS = 1e-6
NEG_INF = -1e30
LOG2E = 1.4426950408889634
N_MIXERS = 3

A_PATTERNS = ((128, 1), (512, 4), (2048, 16))
C_RADIUS = 128
X_HEADS = 4

LANES = 128
ROW_TILE = 512
DENSE_BLOCK = 512
ATTN_CHUNK = 256
ATTN_LEAD = 1
DENSE_PER_TRIP = 4
WIN_Q = 128
WIN_PER_TRIP = 16
REGROUP_STRIDE = 4
WIN_DEPTH = 4
FF_CHUNK = 256
VMEM_LIMIT = 56 * 1024 * 1024

_NT = (((1,), (1,)), ((), ()))
_TN = (((0,), (0,)), ((), ()))


def _params(*sem):
    return pltpu.CompilerParams(dimension_semantics=sem, vmem_limit_bytes=VMEM_LIMIT)


def _rms(x, g):
    ms = jnp.mean(x * x, axis=-1, keepdims=True)
    return x * lax.rsqrt(ms + EPS) * g


def _in_proj_kernel(h_ref, g_ref, wqT_ref, wk_ref, wvT_ref, cosT_ref, sinT_ref, kc_ref, ka_ref, kb_ref,
                    qT_ref, k_ref, vT_ref, *, q_scale, qblk, vblk):
    tm = h_ref.shape[0]
    u = _rms(h_ref[...], g_ref[...]).astype(BF16)

    kf = jnp.dot(u, wk_ref[...], preferred_element_type=F32)
    kc, ka, kb = kc_ref[...], ka_ref[...], kb_ref[...]
    for j in range(kf.shape[1] // LANES):
        x = kf[:, j * LANES:(j + 1) * LANES]
        y = x * kc + pltpu.roll(x, LANES - ROT_HALF, 1) * ka + pltpu.roll(x, ROT_HALF, 1) * kb
        k_ref[:, j * LANES:(j + 1) * LANES] = y.astype(BF16)

    qf = lax.dot_general(wqT_ref[...], u, _NT, preferred_element_type=F32)
    c = cosT_ref[...] * q_scale
    s = sinT_ref[...] * q_scale
    for unit in range(qf.shape[0] // HEAD_DIM):
        r0 = unit * HEAD_DIM
        t1 = qf[r0:r0 + ROT_HALF]
        t2 = qf[r0 + ROT_HALF:r0 + 2 * ROT_HALF]
        rest = qf[r0 + 2 * ROT_HALF:r0 + HEAD_DIM] * q_scale
        blk = jnp.concatenate([t1 * c - t2 * s, t2 * c + t1 * s, rest], axis=0).astype(BF16)
        for jb in range(tm // qblk):
            qT_ref[jb, r0:r0 + HEAD_DIM, :] = blk[:, jb * qblk:(jb + 1) * qblk]

    vf = lax.dot_general(wvT_ref[...], u, _NT, preferred_element_type=F32).astype(BF16)
    for jb in range(tm // vblk):
        vT_ref[jb] = vf[:, jb * vblk:(jb + 1) * vblk]


def _in_proj(h, g, wqT, wk, wvT, cosT, sinT, kc, ka, kb, *, q_scale, qblk, vblk):
    b, s, d = h.shape
    nq, nk, nv = wqT.shape[0], wk.shape[1], wvT.shape[0]
    tm = ROW_TILE
    kern = functools.partial(_in_proj_kernel, q_scale=q_scale, qblk=qblk, vblk=vblk)
    const = lambda bi, i: (0, 0)
    return pl.pallas_call(
        kern,
        grid=(b, s // tm),
        in_specs=[
            pl.BlockSpec((None, tm, d), lambda bi, i: (bi, i, 0)),
            pl.BlockSpec((1, d), const),
            pl.BlockSpec((nq, d), const),
            pl.BlockSpec((d, nk), const),
            pl.BlockSpec((nv, d), const),
            pl.BlockSpec((None, ROT_HALF, tm), lambda bi, i: (bi, 0, i)),
            pl.BlockSpec((None, ROT_HALF, tm), lambda bi, i: (bi, 0, i)),
            pl.BlockSpec((None, tm, LANES), lambda bi, i: (bi, i, 0)),
            pl.BlockSpec((None, tm, LANES), lambda bi, i: (bi, i, 0)),
            pl.BlockSpec((None, tm, LANES), lambda bi, i: (bi, i, 0)),
        ],
        out_specs=[
            pl.BlockSpec((None, tm // qblk, nq, qblk), lambda bi, i: (bi, i, 0, 0)),
            pl.BlockSpec((None, tm, nk), lambda bi, i: (bi, i, 0)),
            pl.BlockSpec((None, tm // vblk, nv, vblk), lambda bi, i: (bi, i, 0, 0)),
        ],
        out_shape=[
            jax.ShapeDtypeStruct((b, s // qblk, nq, qblk), BF16),
            jax.ShapeDtypeStruct((b, s, nk), BF16),
            jax.ShapeDtypeStruct((b, s // vblk, nv, vblk), BF16),
        ],
        compiler_params=_params("parallel", "parallel"),
        name="mixer_in_proj",
    )(h, g, wqT, wk, wvT, cosT, sinT, kc, ka, kb)


def _stage_queries(qT_ref, qz_ref, t):
    q = qT_ref[...]
    row = lax.broadcasted_iota(jnp.int32, q.shape, 0)
    zero = jnp.zeros_like(q)
    qz_ref[:, 0:t] = jnp.where(row < HEAD_DIM, q, zero)
    qz_ref[:, t:2 * t] = jnp.where(row >= HEAD_DIM, q, zero)


def _scores_chunk(k_ref, kb, qz_ref, dst, c, t, cw):
    s_ref, top_ref = dst
    kblk = k_ref[pl.ds(pl.multiple_of(kb * t, t), t), :]
    cols = slice(c * cw, (c + 1) * cw)
    s = jnp.dot(kblk, qz_ref[:, cols], preferred_element_type=F32)
    s_ref[:, cols] = s
    top_ref[:, cols] = jnp.max(s, axis=0, keepdims=True)


def _block_step(kb, src, kb_next, dst, refs, *, t, cw, lead):
    k_ref, vT_ref, qz_ref, m_ref, l_ref, acc_ref = refs
    s_ref, top_ref = src
    nchunk = 2 * t // cw
    v = vT_ref[kb]
    if kb_next is not None:
        for c in range(lead):
            _scores_chunk(k_ref, kb_next, qz_ref, dst, c, t, cw)
    for c in range(nchunk):
        if kb_next is not None and c + lead < nchunk:
            _scores_chunk(k_ref, kb_next, qz_ref, dst, c + lead, t, cw)
        u, cc = divmod(c, t // cw)
        cols = slice(c * cw, (c + 1) * cw)
        acc_at = acc_ref.at[u, :, cc * cw:(cc + 1) * cw]
        m_old = m_ref[:, cols]
        m_new = jnp.maximum(m_old, top_ref[:, cols])
        alpha = jnp.exp2(m_old - m_new)
        p = jnp.exp2(s_ref[:, cols] - m_new)
        l_ref[:, cols] = alpha * l_ref[:, cols] + jnp.sum(p, axis=0, keepdims=True)
        m_ref[:, cols] = m_new
        acc_at[...] = alpha * acc_at[...] + jnp.dot(v, p.astype(BF16), preferred_element_type=F32)


def _diff_attn_kernel(lam_ref, subg_ref, qT_ref, k_ref, vT_ref, oT_ref, qz_ref, m_ref, l_ref, acc_ref,
                      sa_ref, sb_ref, ta_ref, tb_ref, *, t, nkb, cw, lead, per_trip, lam_init):
    _stage_queries(qT_ref, qz_ref, t)
    m_ref[...] = jnp.full(m_ref.shape, NEG_INF, F32)
    l_ref[...] = jnp.zeros(l_ref.shape, F32)
    acc_ref[...] = jnp.zeros(acc_ref.shape, F32)

    bufs = ((sa_ref, ta_ref), (sb_ref, tb_ref))
    step = functools.partial(_block_step, refs=(k_ref, vT_ref, qz_ref, m_ref, l_ref, acc_ref),
                             t=t, cw=cw, lead=lead)

    for c in range(2 * t // cw):
        _scores_chunk(k_ref, 0, qz_ref, bufs[0], c, t, cw)

    def trip(j, carry):
        for i in range(per_trip):
            step(per_trip * j + i, bufs[i % 2], per_trip * j + i + 1, bufs[(i + 1) % 2])
        return carry

    ntrip = nkb // per_trip - 1
    lax.fori_loop(0, ntrip, trip, 0)
    for i in range(per_trip):
        kb = ntrip * per_trip + i
        step(kb, bufs[i % 2], kb + 1 if i + 1 < per_trip else None, bufs[(i + 1) % 2])

    lv = lam_ref[...]
    e1 = jnp.exp(jnp.sum(lv[0:1] * lv[1:2], axis=-1, keepdims=True))
    e2 = jnp.exp(jnp.sum(lv[2:3] * lv[3:4], axis=-1, keepdims=True))
    lam = e1 - e2 + lam_init
    o = acc_ref[0] / l_ref[:, 0:t] - lam * (acc_ref[1] / l_ref[:, t:2 * t])
    ms = jnp.mean(o * o, axis=0, keepdims=True)
    o = o * lax.rsqrt(ms + EPS) * subg_ref[...] * (1.0 - lam_init)
    oT_ref[...] = o.astype(BF16)


def _diff_attention(qT, k, vT, lamv, subg, *, lam_init):
    b, nqb, nq, t = qT.shape
    s = k.shape[1]
    nkb = s // t
    heads = nq // (2 * HEAD_DIM)
    assert nkb % DENSE_PER_TRIP == 0 and DENSE_PER_TRIP % 2 == 0
    kern = functools.partial(_diff_attn_kernel, t=t, nkb=nkb, cw=ATTN_CHUNK, lead=ATTN_LEAD,
                             per_trip=DENSE_PER_TRIP, lam_init=lam_init)
    return pl.pallas_call(
        kern,
        grid=(b, heads, nqb),
        in_specs=[
            pl.BlockSpec(lamv.shape, lambda bi, h, i: (0, 0)),
            pl.BlockSpec(subg.shape, lambda bi, h, i: (0, 0)),
            pl.BlockSpec((None, None, 2 * HEAD_DIM, t), lambda bi, h, i: (bi, i, h, 0)),
            pl.BlockSpec((None, s, 2 * HEAD_DIM), lambda bi, h, i: (bi, 0, h)),
            pl.BlockSpec((None, nkb, 2 * HEAD_DIM, t), lambda bi, h, i: (bi, 0, h, 0)),
        ],
        out_specs=pl.BlockSpec((None, None, 2 * HEAD_DIM, t), lambda bi, h, i: (bi, i, h, 0)),
        out_shape=jax.ShapeDtypeStruct((b, nqb, nq, t), BF16),
        scratch_shapes=[
            pltpu.VMEM((2 * HEAD_DIM, 2 * t), BF16),
            pltpu.VMEM((1, 2 * t), F32),
            pltpu.VMEM((1, 2 * t), F32),
            pltpu.VMEM((2, 2 * HEAD_DIM, t), F32),
            pltpu.VMEM((t, 2 * t), F32),
            pltpu.VMEM((t, 2 * t), F32),
            pltpu.VMEM((1, 2 * t), F32),
            pltpu.VMEM((1, 2 * t), F32),
        ],
        compiler_params=_params("parallel", "parallel", "arbitrary"),
        name="diff_attention",
    )(lamv, subg, qT, k, vT)


def _in_proj_tm_kernel(h_ref, g_ref, w_ref, kc_ref, ka_ref, kb_ref, q_ref, k_ref, v_ref, *, q_scale):
    u = _rms(h_ref[...], g_ref[...]).astype(BF16)
    y = jnp.dot(u, w_ref[...], preferred_element_type=F32)
    kc, ka, kb = kc_ref[...], ka_ref[...], kb_ref[...]
    nq, nk = q_ref.shape[1], k_ref.shape[1]

    def rope(j):
        x = y[:, j * LANES:(j + 1) * LANES]
        return x * kc + pltpu.roll(x, LANES - ROT_HALF, 1) * ka + pltpu.roll(x, ROT_HALF, 1) * kb

    for j in range(nq // LANES):
        q_ref[:, j * LANES:(j + 1) * LANES] = (rope(j) * q_scale).astype(q_ref.dtype)
    for j in range(nk // LANES):
        k_ref[:, j * LANES:(j + 1) * LANES] = rope(nq // LANES + j).astype(k_ref.dtype)
    v_ref[...] = y[:, nq + nk:].astype(v_ref.dtype)


def _in_proj_tm(h, g, w, kc, ka, kb, *, nq, nk, q_scale, out_dtype):
    b, s, d = h.shape
    nv = w.shape[1] - nq - nk
    tm = ROW_TILE
    const = lambda bi, i: (0, 0)
    row = lambda bi, i: (bi, i, 0)
    return pl.pallas_call(
        functools.partial(_in_proj_tm_kernel, q_scale=q_scale),
        grid=(b, s // tm),
        in_specs=[
            pl.BlockSpec((None, tm, d), row),
            pl.BlockSpec((1, d), const),
            pl.BlockSpec(w.shape, const),
            pl.BlockSpec((None, tm, LANES), row),
            pl.BlockSpec((None, tm, LANES), row),
            pl.BlockSpec((None, tm, LANES), row),
        ],
        out_specs=[pl.BlockSpec((None, tm, n), row) for n in (nq, nk, nv)],
        out_shape=[jax.ShapeDtypeStruct((b, s, n), out_dtype) for n in (nq, nk, nv)],
        compiler_params=_params("parallel", "parallel"),
        name="mixer_in_proj_tm",
    )(h, g, w, kc, ka, kb)


def _window_attn_kernel(*refs, radius, dil, per_trip, depth, has_sink):
    if has_sink:
        sink_ref, refs = refs[0], refs[1:]
    bias_ref, q_ref, k_ref, v_ref, o_ref, lse_ref, *scratch = refs
    bufs, scratch = scratch[:depth], scratch[depth:]
    pair = pl.program_id(1)
    total = q_ref.shape[0]
    seq = total // dil
    win = WIN_Q + 2 * radius
    nqb = total // WIN_Q
    lane = lax.broadcasted_iota(jnp.int32, (WIN_Q, 2 * HEAD_DIM), 1)

    if q_ref.dtype == F32:
        qg_ref, kg_ref, vg_ref = scratch[:3]
        onat_ref = scratch[3] if dil > 1 else None
        for src, dst in ((q_ref, qg_ref), (k_ref, kg_ref), (v_ref, vg_ref)):
            if dil > REGROUP_STRIDE:
                assert dil == REGROUP_STRIDE ** 2
                part = total // REGROUP_STRIDE
                for r1 in range(REGROUP_STRIDE):
                    onat_ref[r1 * part:(r1 + 1) * part, :] = src[pl.ds(r1, part, stride=REGROUP_STRIDE), :]
                src, outer = onat_ref, REGROUP_STRIDE
            else:
                outer = 1
            for r in range(dil):
                r1, r2 = r % outer, r // outer
                rows = pl.ds(r1 * (total // outer) + r2, seq, stride=dil // outer) if dil > 1 else slice(None)
                dst[r * seq:(r + 1) * seq, :] = src[rows, :].astype(BF16)
    else:
        qg_ref, kg_ref, vg_ref = q_ref, k_ref, v_ref

    def window(i):
        q0 = i * WIN_Q
        lo = (q0 // seq) * seq
        k0 = jnp.clip(q0 - radius, lo, lo + seq - win)
        return pl.multiple_of(q0, WIN_Q), pl.multiple_of(k0, radius), (q0 - k0) // radius

    def scores(i, dst_ref):
        q0, k0, _ = window(i)
        q = qg_ref[pl.ds(q0, WIN_Q), :]
        zero = jnp.zeros_like(q)
        qz = jnp.concatenate([jnp.where(lane < HEAD_DIM, q, zero), jnp.where(lane >= HEAD_DIM, q, zero)],
                             axis=0)
        dst_ref[...] = lax.dot_general(kg_ref[pl.ds(k0, win), :], qz, _NT, preferred_element_type=F32)

    def finish(i, src_ref):
        q0, k0, bidx = window(i)
        bias = bias_ref[bidx]
        s = src_ref[...] + jnp.concatenate([bias, bias], axis=1)
        m = jnp.max(s, axis=0, keepdims=True)
        if has_sink:
            unit = lax.broadcasted_iota(jnp.int32, m.shape, 1) // WIN_Q
            sk = jnp.where(unit == 0, sink_ref[pair * 2], sink_ref[pair * 2 + 1])
            m = jnp.maximum(m, sk)
        p = jnp.exp2(s - m)
        l = jnp.sum(p, axis=0, keepdims=True)
        if has_sink:
            l = l + jnp.exp2(sk - m)
        oT = lax.dot_general(vg_ref[pl.ds(k0, win), :], p.astype(BF16), _TN, preferred_element_type=F32)
        lse = m + jnp.log2(l)
        halves = []
        for u in range(2):
            cols = slice(u * WIN_Q, (u + 1) * WIN_Q)
            halves.append(oT[u * HEAD_DIM:(u + 1) * HEAD_DIM, cols] / l[:, cols])
            lse_ref[u, pl.ds(i, 1), :] = lse[:, cols]
        o = jnp.concatenate(halves, axis=0).T
        if dil > 1:
            r = q0 // seq
            onat_ref[pl.ds(r + (q0 - r * seq) * dil, WIN_Q, stride=dil), :] = o
        else:
            o_ref[pl.ds(q0, WIN_Q), :] = o.astype(BF16)

    for n in range(depth - 1):
        scores(n, bufs[n])

    def trip(j, carry):
        for n in range(per_trip):
            i = per_trip * j + n
            scores(i + depth - 1, bufs[(n + depth - 1) % depth])
            finish(i, bufs[n % depth])
        return carry

    ntrip = nqb // per_trip - 1
    lax.fori_loop(0, ntrip, trip, 0)
    for n in range(per_trip):
        i = ntrip * per_trip + n
        if n + depth - 1 < per_trip:
            scores(i + depth - 1, bufs[(n + depth - 1) % depth])
        finish(i, bufs[n % depth])
    if dil > 1:
        o_ref[...] = onat_ref[...].astype(BF16)


def _window_bias(radius):
    win = WIN_Q + 2 * radius
    i = np.arange(win)[:, None]
    j = np.arange(WIN_Q)[None, :]
    return jnp.asarray(np.stack([np.where(np.abs(i - j - b * radius) <= radius, 0.0, NEG_INF)
                                 for b in range(3)]), F32)


def _window_attention(q, k, v, sink, *, radius, dil):
    b, s, nq = q.shape
    seq = s // dil
    nqb = s // WIN_Q
    npairs = nq // LANES
    win = WIN_Q + 2 * radius
    shared = k.shape[2] == LANES and npairs > 1
    has_sink = sink is not None
    per_trip = min(WIN_PER_TRIP, nqb)
    depth = min(WIN_DEPTH, per_trip)
    assert seq % WIN_Q == 0 and seq >= win and WIN_Q % radius == 0
    assert nqb % per_trip == 0 and (per_trip % depth == 0 or per_trip == nqb)
    assert q.dtype == F32 or dil == 1
    qspec = pl.BlockSpec((None, s, LANES), lambda bi, p: (bi, 0, p))
    kvspec = pl.BlockSpec((None, s, LANES), lambda bi, p: (bi, 0, 0)) if shared else qspec
    bias = _window_bias(radius)
    in_specs = [pl.BlockSpec(bias.shape, lambda bi, p: (0, 0, 0)), qspec, kvspec, kvspec]
    args = [bias, q, k, v]
    if has_sink:
        in_specs = [pl.BlockSpec(memory_space=pltpu.SMEM)] + in_specs
        args = [sink] + args
    scratch = [pltpu.VMEM((win, 2 * WIN_Q), F32)] * depth
    if q.dtype == F32:
        scratch += [pltpu.VMEM((s, LANES), BF16)] * 3
    if dil > 1:
        scratch += [pltpu.VMEM((s, LANES), F32)]
    o, lse = pl.pallas_call(
        functools.partial(_window_attn_kernel, radius=radius, dil=dil, per_trip=per_trip, depth=depth,
                          has_sink=has_sink),
        grid=(b, npairs),
        in_specs=in_specs,
        out_specs=[qspec, pl.BlockSpec((None, None, 2, nqb, WIN_Q), lambda bi, p: (bi, p, 0, 0, 0))],
        out_shape=[jax.ShapeDtypeStruct((b, s, nq), BF16),
                   jax.ShapeDtypeStruct((b, npairs, 2, nqb, WIN_Q), F32)],
        scratch_shapes=scratch,
        compiler_params=_params("parallel", "parallel"),
        name="window_attention",
    )(*args)
    lse = lse.reshape(b, 2 * npairs, dil, seq).transpose(0, 3, 2, 1).reshape(b, s, 2 * npairs)
    return o, lse


def _mixer_out(mixer_refs, wout_ref, mode):
    if mode == "feature_major":
        (oT_ref,) = mixer_refs
        ys = [lax.dot_general(oT_ref[j], wout_ref[...], _TN, preferred_element_type=F32)
              for j in range(oT_ref.shape[0])]
        return jnp.concatenate(ys, axis=0) if len(ys) > 1 else ys[0]
    if mode == "token_major":
        (o_ref,) = mixer_refs
        return jnp.dot(o_ref[...], wout_ref[...], preferred_element_type=F32)
    n = (len(mixer_refs) - 1) // 2
    o_refs, lse_refs, expand_ref = mixer_refs[:n], mixer_refs[n:2 * n], mixer_refs[2 * n]
    lses = [r[...] for r in lse_refs]
    top = functools.reduce(jnp.maximum, lses)
    es = [jnp.exp2(x - top) for x in lses]
    z = functools.reduce(jnp.add, es)
    o = None
    for e, o_ref in zip(es, o_refs):
        w = e / z
        hi = w.astype(BF16)
        lo = (w - hi.astype(F32)).astype(BF16)
        wide = jnp.dot(jnp.concatenate([hi, lo], axis=1), expand_ref[...], preferred_element_type=F32)
        term = wide * o_ref[...].astype(F32)
        o = term if o is None else o + term
    return jnp.dot(o.astype(BF16), wout_ref[...], preferred_element_type=F32)


def _mid_kernel(*refs, mode, n_mixer, x_scale):
    h_ref = refs[0]
    mixer_refs = refs[1:1 + n_mixer]
    wout_ref, gmix_ref, gpre_ref, wq_ref, kT_ref, v_ref, wo_ref, gpost_ref, out_ref = refs[1 + n_mixer:]
    y = _mixer_out(mixer_refs, wout_ref, mode)
    h1 = h_ref[...] + _rms(y, gmix_ref[...])

    u = _rms(h1, gpre_ref[...]).astype(BF16)
    q = (jnp.dot(u, wq_ref[...], preferred_element_type=F32) * x_scale).astype(BF16)
    xd = q.shape[1] // X_HEADS
    outs = []
    for hd in range(X_HEADS):
        s = jnp.dot(q[:, hd * xd:(hd + 1) * xd], kT_ref[hd * xd:(hd + 1) * xd, :],
                    preferred_element_type=F32)
        p = jnp.exp2(s - jnp.max(s, axis=-1, keepdims=True))
        l = jnp.sum(p, axis=-1, keepdims=True)
        o = jnp.dot(p.astype(BF16), v_ref[:, hd * xd:(hd + 1) * xd], preferred_element_type=F32)
        outs.append((o / l).astype(BF16))
    y2 = jnp.dot(jnp.concatenate(outs, axis=1), wo_ref[...], preferred_element_type=F32)
    out_ref[...] = h1 + _rms(y2, gpost_ref[...])


def _mid(h, mixer, mode, wout, gmix, gpre, wq, kT, v, wo, gpost, *, x_scale):
    b, s, d = h.shape
    tm = ROW_TILE
    n_mem = v.shape[1]
    const = lambda bi, i: (0, 0)
    row = lambda bi, i: (bi, i, 0)
    if mode == "feature_major":
        oblk = mixer[0].shape[3]
        mixer_specs = [pl.BlockSpec((None, tm // oblk, d, oblk), lambda bi, i: (bi, i, 0, 0))]
    else:
        mixer_specs = [pl.BlockSpec((None, tm, a.shape[2]), row) for a in mixer]
    if mode == "branches":
        heads = mixer[-1].shape[2]
        expand = jnp.asarray(np.tile(np.repeat(np.eye(heads), d // heads, axis=1), (2, 1)), BF16)
        mixer = list(mixer) + [expand]
        mixer_specs.append(pl.BlockSpec(expand.shape, const))
    kern = functools.partial(_mid_kernel, mode=mode, n_mixer=len(mixer), x_scale=x_scale)
    return pl.pallas_call(
        kern,
        grid=(b, s // tm),
        in_specs=[
            pl.BlockSpec((None, tm, d), row),
            *mixer_specs,
            pl.BlockSpec((d, d), const),
            pl.BlockSpec((1, d), const),
            pl.BlockSpec((1, d), const),
            pl.BlockSpec((d, d), const),
            pl.BlockSpec((None, d, n_mem), lambda bi, i: (bi, 0, 0)),
            pl.BlockSpec((None, n_mem, d), lambda bi, i: (bi, 0, 0)),
            pl.BlockSpec((d, d), const),
            pl.BlockSpec((1, d), const),
        ],
        out_specs=pl.BlockSpec((None, tm, d), lambda bi, i: (bi, i, 0)),
        out_shape=jax.ShapeDtypeStruct((b, s, d), F32),
        compiler_params=_params("parallel", "parallel"),
        name="out_proj_cross_attention",
    )(h, *mixer, wout, gmix, gpre, wq, kT, v, wo, gpost)


def _mem_kv_kernel(mem_ref, g_ref, wkT_ref, wv_ref, kT_ref, v_ref):
    mn = _rms(mem_ref[...], g_ref[...]).astype(BF16)
    kT_ref[...] = lax.dot_general(wkT_ref[...], mn, _NT, preferred_element_type=F32).astype(BF16)
    v_ref[...] = jnp.dot(mn, wv_ref[...], preferred_element_type=F32).astype(BF16)


def _mem_kv(mem, g, wkT, wv):
    depth, d = g.shape[0], g.shape[2]
    b, n_mem, _ = mem.shape
    return pl.pallas_call(
        _mem_kv_kernel,
        grid=(depth, b),
        in_specs=[
            pl.BlockSpec((None, n_mem, d), lambda li, bi: (bi, 0, 0)),
            pl.BlockSpec((None, 1, d), lambda li, bi: (li, 0, 0)),
            pl.BlockSpec((None, d, d), lambda li, bi: (li, 0, 0)),
            pl.BlockSpec((None, d, d), lambda li, bi: (li, 0, 0)),
        ],
        out_specs=[
            pl.BlockSpec((None, None, d, n_mem), lambda li, bi: (li, bi, 0, 0)),
            pl.BlockSpec((None, None, n_mem, d), lambda li, bi: (li, bi, 0, 0)),
        ],
        out_shape=[
            jax.ShapeDtypeStruct((depth, b, d, n_mem), BF16),
            jax.ShapeDtypeStruct((depth, b, n_mem, d), BF16),
        ],
        compiler_params=_params("parallel", "parallel"),
        name="memory_kv",
    )(mem, g, wkT, wv)


def _ffn_kernel(h_ref, gpre_ref, wgu_ref, wd_ref, gpost_ref, out_ref, acc_ref):
    dff = wd_ref.shape[0]
    u = _rms(h_ref[...], gpre_ref[...]).astype(BF16)
    for c in range(dff // FF_CHUNK):
        cols = slice(c * FF_CHUNK, (c + 1) * FF_CHUNK)
        g = jnp.dot(u, wgu_ref[:, cols], preferred_element_type=F32)
        up = jnp.dot(u, wgu_ref[:, dff + c * FF_CHUNK:dff + (c + 1) * FF_CHUNK], preferred_element_type=F32)
        a = (g / (1.0 + jnp.exp(-g)) * up).astype(BF16)
        part = jnp.dot(a, wd_ref[cols, :], preferred_element_type=F32)
        acc_ref[...] = part if c == 0 else acc_ref[...] + part
    out_ref[...] = h_ref[...] + _rms(acc_ref[...], gpost_ref[...])


def _ffn(h, gpre, wgu, wd, gpost):
    b, s, d = h.shape
    tm = ROW_TILE
    dff = wd.shape[0]
    assert dff % FF_CHUNK == 0
    rows = b * s
    h2 = h.reshape(rows, d)
    const = lambda i: (0, 0)
    resident = dict(pipeline_mode=pl.Buffered(1))
    out = pl.pallas_call(
        _ffn_kernel,
        grid=(rows // tm,),
        in_specs=[
            pl.BlockSpec((tm, d), lambda i: (i, 0)),
            pl.BlockSpec((1, d), const),
            pl.BlockSpec((d, 2 * dff), const, **resident),
            pl.BlockSpec((dff, d), const, **resident),
            pl.BlockSpec((1, d), const),
        ],
        out_specs=pl.BlockSpec((tm, d), lambda i: (i, 0)),
        out_shape=jax.ShapeDtypeStruct((rows, d), F32),
        scratch_shapes=[pltpu.VMEM((tm, d), F32)],
        compiler_params=_params("parallel"),
        name="swiglu_ffn",
    )(h2, gpre, wgu, wd, gpost)
    return out.reshape(b, s, d)


def _rope_tables(positions):
    inv_freq = ROPE_THETA ** (-jnp.arange(0, 2 * ROT_HALF, 2, dtype=F32) / (2 * ROT_HALF))
    ang = positions.astype(F32)[..., None] * inv_freq
    cos, sin = jnp.cos(ang), jnp.sin(ang)
    cosT, sinT = cos.transpose(0, 2, 1), sin.transpose(0, 2, 1)
    zeros = jnp.zeros_like(cos)
    pad = HEAD_DIM - 2 * ROT_HALF
    ones_tail = jnp.ones(cos.shape[:-1] + (pad,), F32)
    zero_tail = jnp.zeros(cos.shape[:-1] + (pad,), F32)
    reps = LANES // HEAD_DIM
    kc = jnp.tile(jnp.concatenate([cos, cos, ones_tail], axis=-1), reps)
    ka = jnp.tile(jnp.concatenate([-sin, zeros, zero_tail], axis=-1), reps)
    kb = jnp.tile(jnp.concatenate([zeros, sin, zero_tail], axis=-1), reps)
    return cosT, sinT, kc, ka, kb


def _row(g):
    return g.reshape(1, -1)


def kernel(x, mem, positions, mix_pre_g, mix_post_g, mem_pre_g, mem_kv_g, mem_post_g, ffn_pre_g, ffn_post_g,
           a_w_in, a_w_out, b_w_in, b_w_out, b_lam_q1, b_lam_k1, b_lam_q2, b_lam_k2, b_sub_g, c_w_in, c_w_out,
           c_sink, x_wq, x_wkv, x_wo, w_gate_up, w_down):
    depth, d = mix_pre_g.shape
    assert d % (2 * HEAD_DIM) == 0 and x.shape[1] % DENSE_BLOCK == 0 and x.shape[1] % ROW_TILE == 0
    cosT, sinT, kc, ka, kb = _rope_tables(positions)
    q_scale = HEAD_DIM ** -0.5 * LOG2E
    x_scale = (d // X_HEADS) ** -0.5 * LOG2E

    mem_kT, mem_v = _mem_kv(mem, mem_kv_g.reshape(depth, 1, d),
                            x_wkv[:, :, :d].transpose(0, 2, 1).astype(BF16), x_wkv[:, :, d:].astype(BF16))

    h = x
    for i in range(depth):
        kind, j = i % N_MIXERS, i // N_MIXERS
        g_pre = _row(mix_pre_g[i])
        if kind == 0:
            w_in, w_out = a_w_in[j], a_w_out[j]
            q, k, v = _in_proj_tm(h, g_pre, w_in.astype(BF16), kc, ka, kb, nq=d, nk=d, q_scale=q_scale,
                                  out_dtype=F32)
            branches = [_window_attention(q, k, v, None, radius=window // (2 * dil), dil=dil)
                        for window, dil in A_PATTERNS]
            mixer, mode = [o for o, _ in branches] + [lse for _, lse in branches], "branches"
        elif kind == 1:
            w_in, w_out = b_w_in[j], b_w_out[j]
            wq, wk, wv = w_in[:, :d], w_in[:, d:2 * d], w_in[:, 2 * d:]
            qT, k, vT = _in_proj(h, g_pre, wq.T.astype(BF16), wk.astype(BF16), wv.T.astype(BF16),
                                 cosT, sinT, kc, ka, kb, q_scale=q_scale, qblk=DENSE_BLOCK, vblk=DENSE_BLOCK)
            lam_init = 0.8 - 0.6 * math.exp(-0.3 * i)
            lamv = jnp.stack([b_lam_q1[j], b_lam_k1[j], b_lam_q2[j], b_lam_k2[j]]).astype(F32)
            subg = jnp.broadcast_to(b_sub_g[j].astype(F32)[:, None], (2 * HEAD_DIM, DENSE_BLOCK))
            mixer, mode = [_diff_attention(qT, k, vT, lamv, subg, lam_init=lam_init)], "feature_major"
        else:
            w_in, w_out = c_w_in[j], c_w_out[j]
            n_kv = (w_in.shape[1] - d) // (2 * HEAD_DIM)
            grp = (d // HEAD_DIM) // n_kv
            perm = np.arange(d).reshape(n_kv, grp, HEAD_DIM).transpose(1, 0, 2).reshape(-1)
            w_in = jnp.concatenate([w_in[:, :d][:, perm], w_in[:, d:]], axis=1)
            w_out = w_out[perm, :]
            q, k, v = _in_proj_tm(h, g_pre, w_in.astype(BF16), kc, ka, kb, nq=d, nk=n_kv * HEAD_DIM,
                                  q_scale=q_scale, out_dtype=BF16)
            sink = (c_sink[j].astype(F32) * LOG2E)[perm[::HEAD_DIM] // HEAD_DIM]
            mixer, mode = [_window_attention(q, k, v, sink, radius=C_RADIUS, dil=1)[0]], "token_major"
        h = _mid(h, mixer, mode, w_out.astype(BF16), _row(mix_post_g[i]), _row(mem_pre_g[i]),
                 x_wq[i].astype(BF16), mem_kT[i], mem_v[i], x_wo[i].astype(BF16), _row(mem_post_g[i]),
                 x_scale=x_scale)
        h = _ffn(h, _row(ffn_pre_g[i]), w_gate_up[i].astype(BF16), w_down[i].astype(BF16), _row(ffn_post_g[i]))
    return h
```

```python
import functools
import math

import jax
import jax.numpy as jnp
import numpy as np
from jax import lax
from jax.experimental import pallas as pl
from jax.experimental.pallas import tpu as pltpu

F32 = jnp.float32
BF16 = jnp.bfloat16

HEAD_DIM = 64
ROT_HALF = HEAD_DIM // 8
ROPE_THETA = 500000.0
EPS = 1e-6
NEG_INF = -1e30
LOG2E = 1.4426950408889634
N_MIXERS = 3

A_PATTERNS = ((128, 1), (512, 4), (2048, 16))
C_RADIUS = 128
X_HEADS = 4

LANES = 128
ROW_TILE = 1024
DENSE_BLOCK = 512
ATTN_CHUNK = 256
ATTN_LEAD = 1
DENSE_PER_TRIP = 4
WIN_Q = 128
WIN_PER_TRIP = 16
REGROUP_STRIDE = 4
WIN_DEPTH = 4
FF_CHUNK = 256
VMEM_LIMIT = 56 * 1024 * 1024

_NT = (((1,), (1,)), ((), ()))
_TN = (((0,), (0,)), ((), ()))


def _params(*sem):
    return pltpu.CompilerParams(dimension_semantics=sem, vmem_limit_bytes=VMEM_LIMIT)


def _rms(x, g):
    ms = jnp.mean(x * x, axis=-1, keepdims=True)
    return x * lax.rsqrt(ms + EPS) * g


def _in_proj_kernel(h_ref, g_ref, wqT_ref, wk_ref, wvT_ref, cosT_ref, sinT_ref, kc_ref, ka_ref, kb_ref,
                    qT_ref, k_ref, vT_ref, *, q_scale, qblk, vblk):
    tm = h_ref.shape[0]
    u = _rms(h_ref[...], g_ref[...]).astype(BF16)

    kf = jnp.dot(u, wk_ref[...], preferred_element_type=F32)
    kc, ka, kb = kc_ref[...], ka_ref[...], kb_ref[...]
    for j in range(kf.shape[1] // LANES):
        x = kf[:, j * LANES:(j + 1) * LANES]
        y = x * kc + pltpu.roll(x, LANES - ROT_HALF, 1) * ka + pltpu.roll(x, ROT_HALF, 1) * kb
        k_ref[:, j * LANES:(j + 1) * LANES] = y.astype(BF16)

    qf = lax.dot_general(wqT_ref[...], u, _NT, preferred_element_type=F32)
    c = cosT_ref[...] * q_scale
    s = sinT_ref[...] * q_scale
    for unit in range(qf.shape[0] // HEAD_DIM):
        r0 = unit * HEAD_DIM
        t1 = qf[r0:r0 + ROT_HALF]
        t2 = qf[r0 + ROT_HALF:r0 + 2 * ROT_HALF]
        rest = qf[r0 + 2 * ROT_HALF:r0 + HEAD_DIM] * q_scale
        blk = jnp.concatenate([t1 * c - t2 * s, t2 * c + t1 * s, rest], axis=0).astype(BF16)
        for jb in range(tm // qblk):
            qT_ref[jb, r0:r0 + HEAD_DIM, :] = blk[:, jb * qblk:(jb + 1) * qblk]

    vf = lax.dot_general(wvT_ref[...], u, _NT, preferred_element_type=F32).astype(BF16)
    for jb in range(tm // vblk):
        vT_ref[jb] = vf[:, jb * vblk:(jb + 1) * vblk]


def _in_proj(h, g, wqT, wk, wvT, cosT, sinT, kc, ka, kb, *, q_scale, qblk, vblk):
    b, s, d = h.shape
    nq, nk, nv = wqT.shape[0], wk.shape[1], wvT.shape[0]
    tm = ROW_TILE
    kern = functools.partial(_in_proj_kernel, q_scale=q_scale, qblk=qblk, vblk=vblk)
    const = lambda bi, i: (0, 0)
    return pl.pallas_call(
        kern,
        grid=(b, s // tm),
        in_specs=[
            pl.BlockSpec((None, tm, d), lambda bi, i: (bi, i, 0)),
            pl.BlockSpec((1, d), const),
            pl.BlockSpec((nq, d), const),
            pl.BlockSpec((d, nk), const),
            pl.BlockSpec((nv, d), const),
            pl.BlockSpec((None, ROT_HALF, tm), lambda bi, i: (bi, 0, i)),
            pl.BlockSpec((None, ROT_HALF, tm), lambda bi, i: (bi, 0, i)),
            pl.BlockSpec((None, tm, LANES), lambda bi, i: (bi, i, 0)),
            pl.BlockSpec((None, tm, LANES), lambda bi, i: (bi, i, 0)),
            pl.BlockSpec((None, tm, LANES), lambda bi, i: (bi, i, 0)),
        ],
        out_specs=[
            pl.BlockSpec((None, tm // qblk, nq, qblk), lambda bi, i: (bi, i, 0, 0)),
            pl.BlockSpec((None, tm, nk), lambda bi, i: (bi, i, 0)),
            pl.BlockSpec((None, tm // vblk, nv, vblk), lambda bi, i: (bi, i, 0, 0)),
        ],
        out_shape=[
            jax.ShapeDtypeStruct((b, s // qblk, nq, qblk), BF16),
            jax.ShapeDtypeStruct((b, s, nk), BF16),
            jax.ShapeDtypeStruct((b, s // vblk, nv, vblk), BF16),
        ],
        compiler_params=_params("parallel", "parallel"),
        name="mixer_in_proj",
    )(h, g, wqT, wk, wvT, cosT, sinT, kc, ka, kb)


def _stage_queries(qT_ref, qz_ref, t):
    q = qT_ref[...]
    row = lax.broadcasted_iota(jnp.int32, q.shape, 0)
    zero = jnp.zeros_like(q)
    qz_ref[:, 0:t] = jnp.where(row < HEAD_DIM, q, zero)
    qz_ref[:, t:2 * t] = jnp.where(row >= HEAD_DIM, q, zero)


def _scores_chunk(k_ref, kb, qz_ref, dst, c, t, cw):
    s_ref, top_ref = dst
    kblk = k_ref[pl.ds(pl.multiple_of(kb * t, t), t), :]
    cols = slice(c * cw, (c + 1) * cw)
    s = jnp.dot(kblk, qz_ref[:, cols], preferred_element_type=F32)
    s_ref[:, cols] = s
    top_ref[:, cols] = jnp.max(s, axis=0, keepdims=True)


def _block_step(kb, src, kb_next, dst, refs, *, t, cw, lead):
    k_ref, vT_ref, qz_ref, m_ref, l_ref, acc_ref = refs
    s_ref, top_ref = src
    nchunk = 2 * t // cw
    v = vT_ref[kb]
    if kb_next is not None:
        for c in range(lead):
            _scores_chunk(k_ref, kb_next, qz_ref, dst, c, t, cw)
    for c in range(nchunk):
        if kb_next is not None and c + lead < nchunk:
            _scores_chunk(k_ref, kb_next, qz_ref, dst, c + lead, t, cw)
        u, cc = divmod(c, t // cw)
        cols = slice(c * cw, (c + 1) * cw)
        acc_at = acc_ref.at[u, :, cc * cw:(cc + 1) * cw]
        m_old = m_ref[:, cols]
        m_new = jnp.maximum(m_old, top_ref[:, cols])
        alpha = jnp.exp2(m_old - m_new)
        p = jnp.exp2(s_ref[:, cols] - m_new)
        l_ref[:, cols] = alpha * l_ref[:, cols] + jnp.sum(p, axis=0, keepdims=True)
        m_ref[:, cols] = m_new
        acc_at[...] = alpha * acc_at[...] + jnp.dot(v, p.astype(BF16), preferred_element_type=F32)


def _diff_attn_kernel(lam_ref, subg_ref, qT_ref, k_ref, vT_ref, oT_ref, qz_ref, m_ref, l_ref, acc_ref,
                      sa_ref, sb_ref, ta_ref, tb_ref, *, t, nkb, cw, lead, per_trip, lam_init):
    _stage_queries(qT_ref, qz_ref, t)
    m_ref[...] = jnp.full(m_ref.shape, NEG_INF, F32)
    l_ref[...] = jnp.zeros(l_ref.shape, F32)
    acc_ref[...] = jnp.zeros(acc_ref.shape, F32)

    bufs = ((sa_ref, ta_ref), (sb_ref, tb_ref))
    step = functools.partial(_block_step, refs=(k_ref, vT_ref, qz_ref, m_ref, l_ref, acc_ref),
                             t=t, cw=cw, lead=lead)

    for c in range(2 * t // cw):
        _scores_chunk(k_ref, 0, qz_ref, bufs[0], c, t, cw)

    def trip(j, carry):
        for i in range(per_trip):
            step(per_trip * j + i, bufs[i % 2], per_trip * j + i + 1, bufs[(i + 1) % 2])
        return carry

    ntrip = nkb // per_trip - 1
    lax.fori_loop(0, ntrip, trip, 0)
    for i in range(per_trip):
        kb = ntrip * per_trip + i
        step(kb, bufs[i % 2], kb + 1 if i + 1 < per_trip else None, bufs[(i + 1) % 2])

    lv = lam_ref[...]
    e1 = jnp.exp(jnp.sum(lv[0:1] * lv[1:2], axis=-1, keepdims=True))
    e2 = jnp.exp(jnp.sum(lv[2:3] * lv[3:4], axis=-1, keepdims=True))
    lam = e1 - e2 + lam_init
    o = acc_ref[0] / l_ref[:, 0:t] - lam * (acc_ref[1] / l_ref[:, t:2 * t])
    ms = jnp.mean(o * o, axis=0, keepdims=True)
    o = o * lax.rsqrt(ms + EPS) * subg_ref[...] * (1.0 - lam_init)
    oT_ref[...] = o.astype(BF16)


def _diff_attention(qT, k, vT, lamv, subg, *, lam_init):
    b, nqb, nq, t = qT.shape
    s = k.shape[1]
    nkb = s // t
    heads = nq // (2 * HEAD_DIM)
    assert nkb % DENSE_PER_TRIP == 0 and DENSE_PER_TRIP % 2 == 0
    kern = functools.partial(_diff_attn_kernel, t=t, nkb=nkb, cw=ATTN_CHUNK, lead=ATTN_LEAD,
                             per_trip=DENSE_PER_TRIP, lam_init=lam_init)
    return pl.pallas_call(
        kern,
        grid=(b, heads, nqb),
        in_specs=[
            pl.BlockSpec(lamv.shape, lambda bi, h, i: (0, 0)),
            pl.BlockSpec(subg.shape, lambda bi, h, i: (0, 0)),
            pl.BlockSpec((None, None, 2 * HEAD_DIM, t), lambda bi, h, i: (bi, i, h, 0)),
            pl.BlockSpec((None, s, 2 * HEAD_DIM), lambda bi, h, i: (bi, 0, h)),
            pl.BlockSpec((None, nkb, 2 * HEAD_DIM, t), lambda bi, h, i: (bi, 0, h, 0)),
        ],
        out_specs=pl.BlockSpec((None, None, 2 * HEAD_DIM, t), lambda bi, h, i: (bi, i, h, 0)),
        out_shape=jax.ShapeDtypeStruct((b, nqb, nq, t), BF16),
        scratch_shapes=[
            pltpu.VMEM((2 * HEAD_DIM, 2 * t), BF16),
            pltpu.VMEM((1, 2 * t), F32),
            pltpu.VMEM((1, 2 * t), F32),
            pltpu.VMEM((2, 2 * HEAD_DIM, t), F32),
            pltpu.VMEM((t, 2 * t), F32),
            pltpu.VMEM((t, 2 * t), F32),
            pltpu.VMEM((1, 2 * t), F32),
            pltpu.VMEM((1, 2 * t), F32),
        ],
        compiler_params=_params("parallel", "parallel", "arbitrary"),
        name="diff_attention",
    )(lamv, subg, qT, k, vT)


def _in_proj_tm_kernel(h_ref, g_ref, w_ref, kc_ref, ka_ref, kb_ref, q_ref, k_ref, v_ref, *, q_scale):
    u = _rms(h_ref[...], g_ref[...]).astype(BF16)
    y = jnp.dot(u, w_ref[...], preferred_element_type=F32)
    kc, ka, kb = kc_ref[...], ka_ref[...], kb_ref[...]
    nq, nk = q_ref.shape[1], k_ref.shape[1]

    def rope(j):
        x = y[:, j * LANES:(j + 1) * LANES]
        return x * kc + pltpu.roll(x, LANES - ROT_HALF, 1) * ka + pltpu.roll(x, ROT_HALF, 1) * kb

    for j in range(nq // LANES):
        q_ref[:, j * LANES:(j + 1) * LANES] = (rope(j) * q_scale).astype(q_ref.dtype)
    for j in range(nk // LANES):
        k_ref[:, j * LANES:(j + 1) * LANES] = rope(nq // LANES + j).astype(k_ref.dtype)
    v_ref[...] = y[:, nq + nk:].astype(v_ref.dtype)


def _in_proj_tm(h, g, w, kc, ka, kb, *, nq, nk, q_scale, out_dtype):
    b, s, d = h.shape
    nv = w.shape[1] - nq - nk
    tm = ROW_TILE
    const = lambda bi, i: (0, 0)
    row = lambda bi, i: (bi, i, 0)
    return pl.pallas_call(
        functools.partial(_in_proj_tm_kernel, q_scale=q_scale),
        grid=(b, s // tm),
        in_specs=[
            pl.BlockSpec((None, tm, d), row),
            pl.BlockSpec((1, d), const),
            pl.BlockSpec(w.shape, const),
            pl.BlockSpec((None, tm, LANES), row),
            pl.BlockSpec((None, tm, LANES), row),
            pl.BlockSpec((None, tm, LANES), row),
        ],
        out_specs=[pl.BlockSpec((None, tm, n), row) for n in (nq, nk, nv)],
        out_shape=[jax.ShapeDtypeStruct((b, s, n), out_dtype) for n in (nq, nk, nv)],
        compiler_params=_params("parallel", "parallel"),
        name="mixer_in_proj_tm",
    )(h, g, w, kc, ka, kb)


def _window_attn_kernel(*refs, radius, dils, per_trip, depth, has_sink):
    if has_sink:
        sink_ref, refs = refs[0], refs[1:]
    bias_ref, q_ref, k_ref, v_ref = refs[:4]
    outs, scratch = refs[4:4 + 2 * len(dils)], refs[4 + 2 * len(dils):]
    bufs, scratch = scratch[:depth], scratch[depth:]
    pair = pl.program_id(1)
    total = q_ref.shape[0]
    win = WIN_Q + 2 * radius
    nqb = total // WIN_Q
    lane = lax.broadcasted_iota(jnp.int32, (WIN_Q, 2 * HEAD_DIM), 1)
    if q_ref.dtype == F32:
        qg_ref, kg_ref, vg_ref = scratch[:3]
        onat_ref = scratch[3] if max(dils) > 1 else None
    else:
        qg_ref, kg_ref, vg_ref = q_ref, k_ref, v_ref

    for branch, dil in enumerate(dils):
        o_ref, lse_ref = outs[2 * branch], outs[2 * branch + 1]
        seq = total // dil
        if q_ref.dtype == F32:
            for src, dst in ((q_ref, qg_ref), (k_ref, kg_ref), (v_ref, vg_ref)):
                if dil > REGROUP_STRIDE:
                    assert dil == REGROUP_STRIDE ** 2
                    part = total // REGROUP_STRIDE
                    for r1 in range(REGROUP_STRIDE):
                        onat_ref[r1 * part:(r1 + 1) * part, :] = src[pl.ds(r1, part, stride=REGROUP_STRIDE), :]
                    src, outer = onat_ref, REGROUP_STRIDE
                else:
                    outer = 1
                for r in range(dil):
                    r1, r2 = r % outer, r // outer
                    rows = (pl.ds(r1 * (total // outer) + r2, seq, stride=dil // outer) if dil > 1
                            else slice(None))
                    dst[r * seq:(r + 1) * seq, :] = src[rows, :].astype(BF16)
        _window_branch(bias_ref, sink_ref if has_sink else None, qg_ref, kg_ref, vg_ref, o_ref, lse_ref,
                       onat_ref if dil > 1 else None, bufs, lane, pair, radius=radius, dil=dil, seq=seq,
                       win=win, nqb=nqb, per_trip=per_trip, depth=depth)


def _window_branch(bias_ref, sink_ref, qg_ref, kg_ref, vg_ref, o_ref, lse_ref, onat_ref, bufs, lane, pair, *,
                   radius, dil, seq, win, nqb, per_trip, depth):
    has_sink = sink_ref is not None

    def window(i):
        q0 = i * WIN_Q
        lo = (q0 // seq) * seq
        k0 = jnp.clip(q0 - radius, lo, lo + seq - win)
        return pl.multiple_of(q0, WIN_Q), pl.multiple_of(k0, radius), (q0 - k0) // radius

    def scores(i, dst_ref):
        q0, k0, _ = window(i)
        q = qg_ref[pl.ds(q0, WIN_Q), :]
        zero = jnp.zeros_like(q)
        qz = jnp.concatenate([jnp.where(lane < HEAD_DIM, q, zero), jnp.where(lane >= HEAD_DIM, q, zero)],
                             axis=0)
        dst_ref[...] = lax.dot_general(kg_ref[pl.ds(k0, win), :], qz, _NT, preferred_element_type=F32)

    def finish(i, src_ref):
        q0, k0, bidx = window(i)
        bias = bias_ref[bidx]
        s = src_ref[...] + jnp.concatenate([bias, bias], axis=1)
        m = jnp.max(s, axis=0, keepdims=True)
        if has_sink:
            unit = lax.broadcasted_iota(jnp.int32, m.shape, 1) // WIN_Q
            sk = jnp.where(unit == 0, sink_ref[pair * 2], sink_ref[pair * 2 + 1])
            m = jnp.maximum(m, sk)
        p = jnp.exp2(s - m)
        l = jnp.sum(p, axis=0, keepdims=True)
        if has_sink:
            l = l + jnp.exp2(sk - m)
        oT = lax.dot_general(vg_ref[pl.ds(k0, win), :], p.astype(BF16), _TN, preferred_element_type=F32)
        lse = m + jnp.log2(l)
        halves = []
        for u in range(2):
            cols = slice(u * WIN_Q, (u + 1) * WIN_Q)
            halves.append(oT[u * HEAD_DIM:(u + 1) * HEAD_DIM, cols] / l[:, cols])
            lse_ref[u, pl.ds(i, 1), :] = lse[:, cols]
        o = jnp.concatenate(halves, axis=0).T
        if dil > 1:
            r = q0 // seq
            onat_ref[pl.ds(r + (q0 - r * seq) * dil, WIN_Q, stride=dil), :] = o
        else:
            o_ref[pl.ds(q0, WIN_Q), :] = o.astype(BF16)

    for n in range(depth - 1):
        scores(n, bufs[n])

    def trip(j, carry):
        for n in range(per_trip):
            i = per_trip * j + n
            scores(i + depth - 1, bufs[(n + depth - 1) % depth])
            finish(i, bufs[n % depth])
        return carry

    ntrip = nqb // per_trip - 1
    lax.fori_loop(0, ntrip, trip, 0)
    for n in range(per_trip):
        i = ntrip * per_trip + n
        if n + depth - 1 < per_trip:
            scores(i + depth - 1, bufs[(n + depth - 1) % depth])
        finish(i, bufs[n % depth])
    if dil > 1:
        o_ref[...] = onat_ref[...].astype(BF16)


def _window_bias(radius):
    win = WIN_Q + 2 * radius
    i = np.arange(win)[:, None]
    j = np.arange(WIN_Q)[None, :]
    return jnp.asarray(np.stack([np.where(np.abs(i - j - b * radius) <= radius, 0.0, NEG_INF)
                                 for b in range(3)]), F32)


def _window_attention(q, k, v, sink, *, radius, dils):
    b, s, nq = q.shape
    nqb = s // WIN_Q
    npairs = nq // LANES
    win = WIN_Q + 2 * radius
    shared = k.shape[2] == LANES and npairs > 1
    has_sink = sink is not None
    per_trip = min(WIN_PER_TRIP, nqb)
    depth = min(WIN_DEPTH, per_trip)
    assert all((s // dil) % WIN_Q == 0 and s // dil >= win for dil in dils) and WIN_Q % radius == 0
    assert nqb % per_trip == 0 and (per_trip % depth == 0 or per_trip == nqb)
    assert q.dtype == F32 or max(dils) == 1
    qspec = pl.BlockSpec((None, s, LANES), lambda bi, p: (bi, 0, p))
    kvspec = pl.BlockSpec((None, s, LANES), lambda bi, p: (bi, 0, 0)) if shared else qspec
    lse_spec = pl.BlockSpec((None, None, 2, nqb, WIN_Q), lambda bi, p: (bi, p, 0, 0, 0))
    bias = _window_bias(radius)
    in_specs = [pl.BlockSpec(bias.shape, lambda bi, p: (0, 0, 0)), qspec, kvspec, kvspec]
    args = [bias, q, k, v]
    if has_sink:
        in_specs = [pl.BlockSpec(memory_space=pltpu.SMEM)] + in_specs
        args = [sink] + args
    scratch = [pltpu.VMEM((win, 2 * WIN_Q), F32)] * depth
    if q.dtype == F32:
        scratch += [pltpu.VMEM((s, LANES), BF16)] * 3
    if max(dils) > 1:
        scratch += [pltpu.VMEM((s, LANES), F32)]
    outs = pl.pallas_call(
        functools.partial(_window_attn_kernel, radius=radius, dils=dils, per_trip=per_trip, depth=depth,
                          has_sink=has_sink),
        grid=(b, npairs),
        in_specs=in_specs,
        out_specs=[qspec, lse_spec] * len(dils),
        out_shape=[jax.ShapeDtypeStruct((b, s, nq), BF16),
                   jax.ShapeDtypeStruct((b, npairs, 2, nqb, WIN_Q), F32)] * len(dils),
        scratch_shapes=scratch,
        compiler_params=_params("parallel", "parallel"),
        name="window_attention",
    )(*args)
    results = []
    for n, dil in enumerate(dils):
        o, lse = outs[2 * n], outs[2 * n + 1]
        lse = lse.reshape(b, 2 * npairs, dil, s // dil).transpose(0, 3, 2, 1).reshape(b, s, 2 * npairs)
        results.append((o, lse))
    return results


def _mixer_out(mixer_refs, wout_ref, mode):
    if mode == "feature_major":
        (oT_ref,) = mixer_refs
        ys = [lax.dot_general(oT_ref[j], wout_ref[...], _TN, preferred_element_type=F32)
              for j in range(oT_ref.shape[0])]
        return jnp.concatenate(ys, axis=0) if len(ys) > 1 else ys[0]
    if mode == "token_major":
        (o_ref,) = mixer_refs
        return jnp.dot(o_ref[...], wout_ref[...], preferred_element_type=F32)
    n = (len(mixer_refs) - 1) // 2
    o_refs, lse_refs, expand_ref = mixer_refs[:n], mixer_refs[n:2 * n], mixer_refs[2 * n]
    lses = [r[...] for r in lse_refs]
    top = functools.reduce(jnp.maximum, lses)
    es = [jnp.exp2(x - top) for x in lses]
    z = functools.reduce(jnp.add, es)
    o = None
    for e, o_ref in zip(es, o_refs):
        w = e / z
        hi = w.astype(BF16)
        lo = (w - hi.astype(F32)).astype(BF16)
        wide = jnp.dot(jnp.concatenate([hi, lo], axis=1), expand_ref[...], preferred_element_type=F32)
        term = wide * o_ref[...].astype(F32)
        o = term if o is None else o + term
    return jnp.dot(o.astype(BF16), wout_ref[...], preferred_element_type=F32)


def _mid_kernel(*refs, mode, n_mixer, x_scale):
    h_ref = refs[0]
    mixer_refs = refs[1:1 + n_mixer]
    wout_ref, gmix_ref, gpre_ref, wq_ref, kT_ref, v_ref, wo_ref, gpost_ref, out_ref = refs[1 + n_mixer:]
    y = _mixer_out(mixer_refs, wout_ref, mode)
    h1 = h_ref[...] + _rms(y, gmix_ref[...])

    u = _rms(h1, gpre_ref[...]).astype(BF16)
    q = (jnp.dot(u, wq_ref[...], preferred_element_type=F32) * x_scale).astype(BF16)
    xd = q.shape[1] // X_HEADS
    outs = []
    for hd in range(X_HEADS):
        s = jnp.dot(q[:, hd * xd:(hd + 1) * xd], kT_ref[hd * xd:(hd + 1) * xd, :],
                    preferred_element_type=F32)
        p = jnp.exp2(s - jnp.max(s, axis=-1, keepdims=True))
        l = jnp.sum(p, axis=-1, keepdims=True)
        o = jnp.dot(p.astype(BF16), v_ref[:, hd * xd:(hd + 1) * xd], preferred_element_type=F32)
        outs.append((o / l).astype(BF16))
    y2 = jnp.dot(jnp.concatenate(outs, axis=1), wo_ref[...], preferred_element_type=F32)
    out_ref[...] = h1 + _rms(y2, gpost_ref[...])


def _mid(h, mixer, mode, wout, gmix, gpre, wq, kT, v, wo, gpost, *, x_scale):
    b, s, d = h.shape
    tm = ROW_TILE
    n_mem = v.shape[1]
    const = lambda bi, i: (0, 0)
    row = lambda bi, i: (bi, i, 0)
    if mode == "feature_major":
        oblk = mixer[0].shape[3]
        mixer_specs = [pl.BlockSpec((None, tm // oblk, d, oblk), lambda bi, i: (bi, i, 0, 0))]
    else:
        mixer_specs = [pl.BlockSpec((None, tm, a.shape[2]), row) for a in mixer]
    if mode == "branches":
        heads = mixer[-1].shape[2]
        expand = jnp.asarray(np.tile(np.repeat(np.eye(heads), d // heads, axis=1), (2, 1)), BF16)
        mixer = list(mixer) + [expand]
        mixer_specs.append(pl.BlockSpec(expand.shape, const))
    kern = functools.partial(_mid_kernel, mode=mode, n_mixer=len(mixer), x_scale=x_scale)
    return pl.pallas_call(
        kern,
        grid=(b, s // tm),
        in_specs=[
            pl.BlockSpec((None, tm, d), row),
            *mixer_specs,
            pl.BlockSpec((d, d), const),
            pl.BlockSpec((1, d), const),
            pl.BlockSpec((1, d), const),
            pl.BlockSpec((d, d), const),
            pl.BlockSpec((None, d, n_mem), lambda bi, i: (bi, 0, 0)),
            pl.BlockSpec((None, n_mem, d), lambda bi, i: (bi, 0, 0)),
            pl.BlockSpec((d, d), const),
            pl.BlockSpec((1, d), const),
        ],
        out_specs=pl.BlockSpec((None, tm, d), lambda bi, i: (bi, i, 0)),
        out_shape=jax.ShapeDtypeStruct((b, s, d), F32),
        compiler_params=_params("parallel", "parallel"),
        name="out_proj_cross_attention",
    )(h, *mixer, wout, gmix, gpre, wq, kT, v, wo, gpost)


def _mem_kv_kernel(mem_ref, g_ref, wkT_ref, wv_ref, kT_ref, v_ref):
    mn = _rms(mem_ref[...], g_ref[...]).astype(BF16)
    kT_ref[...] = lax.dot_general(wkT_ref[...], mn, _NT, preferred_element_type=F32).astype(BF16)
    v_ref[...] = jnp.dot(mn, wv_ref[...], preferred_element_type=F32).astype(BF16)


def _mem_kv(mem, g, wkT, wv):
    depth, d = g.shape[0], g.shape[2]
    b, n_mem, _ = mem.shape
    return pl.pallas_call(
        _mem_kv_kernel,
        grid=(depth, b),
        in_specs=[
            pl.BlockSpec((None, n_mem, d), lambda li, bi: (bi, 0, 0)),
            pl.BlockSpec((None, 1, d), lambda li, bi: (li, 0, 0)),
            pl.BlockSpec((None, d, d), lambda li, bi: (li, 0, 0)),
            pl.BlockSpec((None, d, d), lambda li, bi: (li, 0, 0)),
        ],
        out_specs=[
            pl.BlockSpec((None, None, d, n_mem), lambda li, bi: (li, bi, 0, 0)),
            pl.BlockSpec((None, None, n_mem, d), lambda li, bi: (li, bi, 0, 0)),
        ],
        out_shape=[
            jax.ShapeDtypeStruct((depth, b, d, n_mem), BF16),
            jax.ShapeDtypeStruct((depth, b, n_mem, d), BF16),
        ],
        compiler_params=_params("parallel", "parallel"),
        name="memory_kv",
    )(mem, g, wkT, wv)


def _ffn_kernel(h_ref, gpre_ref, wgu_ref, wd_ref, gpost_ref, out_ref, acc_ref):
    dff = wd_ref.shape[0]
    u = _rms(h_ref[...], gpre_ref[...]).astype(BF16)
    for c in range(dff // FF_CHUNK):
        cols = slice(c * FF_CHUNK, (c + 1) * FF_CHUNK)
        g = jnp.dot(u, wgu_ref[:, cols], preferred_element_type=F32)
        up = jnp.dot(u, wgu_ref[:, dff + c * FF_CHUNK:dff + (c + 1) * FF_CHUNK], preferred_element_type=F32)
        a = (g / (1.0 + jnp.exp(-g)) * up).astype(BF16)
        part = jnp.dot(a, wd_ref[cols, :], preferred_element_type=F32)
        acc_ref[...] = part if c == 0 else acc_ref[...] + part
    out_ref[...] = h_ref[...] + _rms(acc_ref[...], gpost_ref[...])


def _ffn(h, gpre, wgu, wd, gpost):
    b, s, d = h.shape
    tm = ROW_TILE
    dff = wd.shape[0]
    assert dff % FF_CHUNK == 0
    rows = b * s
    h2 = h.reshape(rows, d)
    const = lambda i: (0, 0)
    resident = dict(pipeline_mode=pl.Buffered(1))
    out = pl.pallas_call(
        _ffn_kernel,
        grid=(rows // tm,),
        in_specs=[
            pl.BlockSpec((tm, d), lambda i: (i, 0)),
            pl.BlockSpec((1, d), const),
            pl.BlockSpec((d, 2 * dff), const, **resident),
            pl.BlockSpec((dff, d), const, **resident),
            pl.BlockSpec((1, d), const),
        ],
        out_specs=pl.BlockSpec((tm, d), lambda i: (i, 0)),
        out_shape=jax.ShapeDtypeStruct((rows, d), F32),
        scratch_shapes=[pltpu.VMEM((tm, d), F32)],
        compiler_params=_params("parallel"),
        name="swiglu_ffn",
    )(h2, gpre, wgu, wd, gpost)
    return out.reshape(b, s, d)


def _rope_tables(positions):
    inv_freq = ROPE_THETA ** (-jnp.arange(0, 2 * ROT_HALF, 2, dtype=F32) / (2 * ROT_HALF))
    ang = positions.astype(F32)[..., None] * inv_freq
    cos, sin = jnp.cos(ang), jnp.sin(ang)
    cosT, sinT = cos.transpose(0, 2, 1), sin.transpose(0, 2, 1)
    zeros = jnp.zeros_like(cos)
    pad = HEAD_DIM - 2 * ROT_HALF
    ones_tail = jnp.ones(cos.shape[:-1] + (pad,), F32)
    zero_tail = jnp.zeros(cos.shape[:-1] + (pad,), F32)
    reps = LANES // HEAD_DIM
    kc = jnp.tile(jnp.concatenate([cos, cos, ones_tail], axis=-1), reps)
    ka = jnp.tile(jnp.concatenate([-sin, zeros, zero_tail], axis=-1), reps)
    kb = jnp.tile(jnp.concatenate([zeros, sin, zero_tail], axis=-1), reps)
    return cosT, sinT, kc, ka, kb


def _row(g):
    return g.reshape(1, -1)


def kernel(x, mem, positions, mix_pre_g, mix_post_g, mem_pre_g, mem_kv_g, mem_post_g, ffn_pre_g, ffn_post_g,
           a_w_in, a_w_out, b_w_in, b_w_out, b_lam_q1, b_lam_k1, b_lam_q2, b_lam_k2, b_sub_g, c_w_in, c_w_out,
           c_sink, x_wq, x_wkv, x_wo, w_gate_up, w_down):
    depth, d = mix_pre_g.shape
    assert d % (2 * HEAD_DIM) == 0 and x.shape[1] % DENSE_BLOCK == 0 and x.shape[1] % ROW_TILE == 0
    cosT, sinT, kc, ka, kb = _rope_tables(positions)
    q_scale = HEAD_DIM ** -0.5 * LOG2E
    x_scale = (d // X_HEADS) ** -0.5 * LOG2E

    mem_kT, mem_v = _mem_kv(mem, mem_kv_g.reshape(depth, 1, d),
                            x_wkv[:, :, :d].transpose(0, 2, 1).astype(BF16), x_wkv[:, :, d:].astype(BF16))

    h = x
    for i in range(depth):
        kind, j = i % N_MIXERS, i // N_MIXERS
        g_pre = _row(mix_pre_g[i])
        if kind == 0:
            w_in, w_out = a_w_in[j], a_w_out[j]
            q, k, v = _in_proj_tm(h, g_pre, w_in.astype(BF16), kc, ka, kb, nq=d, nk=d, q_scale=q_scale,
                                  out_dtype=F32)
            radii = {window // (2 * dil) for window, dil in A_PATTERNS}
            assert len(radii) == 1
            branches = _window_attention(q, k, v, None, radius=radii.pop(), dils=tuple(dl for _, dl in A_PATTERNS))
            mixer, mode = [o for o, _ in branches] + [lse for _, lse in branches], "branches"
        elif kind == 1:
            w_in, w_out = b_w_in[j], b_w_out[j]
            wq, wk, wv = w_in[:, :d], w_in[:, d:2 * d], w_in[:, 2 * d:]
            qT, k, vT = _in_proj(h, g_pre, wq.T.astype(BF16), wk.astype(BF16), wv.T.astype(BF16),
                                 cosT, sinT, kc, ka, kb, q_scale=q_scale, qblk=DENSE_BLOCK, vblk=DENSE_BLOCK)
            lam_init = 0.8 - 0.6 * math.exp(-0.3 * i)
            lamv = jnp.stack([b_lam_q1[j], b_lam_k1[j], b_lam_q2[j], b_lam_k2[j]]).astype(F32)
            subg = jnp.broadcast_to(b_sub_g[j].astype(F32)[:, None], (2 * HEAD_DIM, DENSE_BLOCK))
            mixer, mode = [_diff_attention(qT, k, vT, lamv, subg, lam_init=lam_init)], "feature_major"
        else:
            w_in, w_out = c_w_in[j], c_w_out[j]
            n_kv = (w_in.shape[1] - d) // (2 * HEAD_DIM)
            grp = (d // HEAD_DIM) // n_kv
            perm = np.arange(d).reshape(n_kv, grp, HEAD_DIM).transpose(1, 0, 2).reshape(-1)
            w_in = jnp.concatenate([w_in[:, :d][:, perm], w_in[:, d:]], axis=1)
            w_out = w_out[perm, :]
            q, k, v = _in_proj_tm(h, g_pre, w_in.astype(BF16), kc, ka, kb, nq=d, nk=n_kv * HEAD_DIM,
                                  q_scale=q_scale, out_dtype=BF16)
            sink = (c_sink[j].astype(F32) * LOG2E)[perm[::HEAD_DIM] // HEAD_DIM]
            mixer, mode = [_window_attention(q, k, v, sink, radius=C_RADIUS, dils=(1,))[0][0]], "token_major"
        h = _mid(h, mixer, mode, w_out.astype(BF16), _row(mix_post_g[i]), _row(mem_pre_g[i]),
                 x_wq[i].astype(BF16), mem_kT[i], mem_v[i], x_wo[i].astype(BF16), _row(mem_post_g[i]),
                 x_scale=x_scale)
        h = _ffn(h, _row(ffn_pre_g[i]), w_gate_up[i].astype(BF16), w_down[i].astype(BF16), _row(ffn_post_g[i]))
    return h
```

```python
import functools
import math

import jax
import jax.numpy as jnp
import numpy as np
from jax import lax
from jax.experimental import pallas as pl
from jax.experimental.pallas import tpu as pltpu

F32 = jnp.float32
BF16 = jnp.bfloat16

HEAD_DIM = 64
ROT_HALF = HEAD_DIM // 8
ROPE_THETA = 500000.0
EPS = 1e-6
NEG_INF = -1e30
LOG2E = 1.4426950408889634
N_MIXERS = 3

A_PATTERNS = ((128, 1), (512, 4), (2048, 16))
C_RADIUS = 128
X_HEADS = 4

LANES = 128
ROW_TILE = 1024
DENSE_BLOCK = 1024
DENSE_KBLOCK = 512
ATTN_CHUNK = 256
ATTN_LEAD = 1
DENSE_PER_TRIP = 4
WIN_Q = 128
WIN_PER_TRIP = 16
REGROUP_STRIDE = 4
WIN_DEPTH = 4
FF_CHUNK = 256
VMEM_LIMIT = 56 * 1024 * 1024

_NT = (((1,), (1,)), ((), ()))
_TN = (((0,), (0,)), ((), ()))


def _params(*sem):
    return pltpu.CompilerParams(dimension_semantics=sem, vmem_limit_bytes=VMEM_LIMIT)


def _rms(x, g):
    ms = jnp.mean(x * x, axis=-1, keepdims=True)
    return x * lax.rsqrt(ms + EPS) * g


def _in_proj_kernel(h_ref, g_ref, wqT_ref, wk_ref, wvT_ref, cosT_ref, sinT_ref, kc_ref, ka_ref, kb_ref,
                    qT_ref, k_ref, vT_ref, *, q_scale, qblk, vblk):
    tm = h_ref.shape[0]
    u = _rms(h_ref[...], g_ref[...]).astype(BF16)

    kf = jnp.dot(u, wk_ref[...], preferred_element_type=F32)
    kc, ka, kb = kc_ref[...], ka_ref[...], kb_ref[...]
    for j in range(kf.shape[1] // LANES):
        x = kf[:, j * LANES:(j + 1) * LANES]
        y = x * kc + pltpu.roll(x, LANES - ROT_HALF, 1) * ka + pltpu.roll(x, ROT_HALF, 1) * kb
        k_ref[:, j * LANES:(j + 1) * LANES] = y.astype(BF16)

    qf = lax.dot_general(wqT_ref[...], u, _NT, preferred_element_type=F32)
    c = cosT_ref[...] * q_scale
    s = sinT_ref[...] * q_scale
    for unit in range(qf.shape[0] // HEAD_DIM):
        r0 = unit * HEAD_DIM
        t1 = qf[r0:r0 + ROT_HALF]
        t2 = qf[r0 + ROT_HALF:r0 + 2 * ROT_HALF]
        rest = qf[r0 + 2 * ROT_HALF:r0 + HEAD_DIM] * q_scale
        blk = jnp.concatenate([t1 * c - t2 * s, t2 * c + t1 * s, rest], axis=0).astype(BF16)
        for jb in range(tm // qblk):
            qT_ref[jb, r0:r0 + HEAD_DIM, :] = blk[:, jb * qblk:(jb + 1) * qblk]

    vf = lax.dot_general(wvT_ref[...], u, _NT, preferred_element_type=F32).astype(BF16)
    for jb in range(tm // vblk):
        vT_ref[jb] = vf[:, jb * vblk:(jb + 1) * vblk]


def _in_proj(h, g, wqT, wk, wvT, cosT, sinT, kc, ka, kb, *, q_scale, qblk, vblk):
    b, s, d = h.shape
    nq, nk, nv = wqT.shape[0], wk.shape[1], wvT.shape[0]
    tm = ROW_TILE
    kern = functools.partial(_in_proj_kernel, q_scale=q_scale, qblk=qblk, vblk=vblk)
    const = lambda bi, i: (0, 0)
    return pl.pallas_call(
        kern,
        grid=(b, s // tm),
        in_specs=[
            pl.BlockSpec((None, tm, d), lambda bi, i: (bi, i, 0)),
            pl.BlockSpec((1, d), const),
            pl.BlockSpec((nq, d), const),
            pl.BlockSpec((d, nk), const),
            pl.BlockSpec((nv, d), const),
            pl.BlockSpec((None, ROT_HALF, tm), lambda bi, i: (bi, 0, i)),
            pl.BlockSpec((None, ROT_HALF, tm), lambda bi, i: (bi, 0, i)),
            pl.BlockSpec((None, tm, LANES), lambda bi, i: (bi, i, 0)),
            pl.BlockSpec((None, tm, LANES), lambda bi, i: (bi, i, 0)),
            pl.BlockSpec((None, tm, LANES), lambda bi, i: (bi, i, 0)),
        ],
        out_specs=[
            pl.BlockSpec((None, tm // qblk, nq, qblk), lambda bi, i: (bi, i, 0, 0)),
            pl.BlockSpec((None, tm, nk), lambda bi, i: (bi, i, 0)),
            pl.BlockSpec((None, tm // vblk, nv, vblk), lambda bi, i: (bi, i, 0, 0)),
        ],
        out_shape=[
            jax.ShapeDtypeStruct((b, s // qblk, nq, qblk), BF16),
            jax.ShapeDtypeStruct((b, s, nk), BF16),
            jax.ShapeDtypeStruct((b, s // vblk, nv, vblk), BF16),
        ],
        compiler_params=_params("parallel", "parallel"),
        name="mixer_in_proj",
    )(h, g, wqT, wk, wvT, cosT, sinT, kc, ka, kb)


def _stage_queries(qT_ref, qz_ref, t):
    q = qT_ref[...]
    row = lax.broadcasted_iota(jnp.int32, q.shape, 0)
    zero = jnp.zeros_like(q)
    qz_ref[:, 0:t] = jnp.where(row < HEAD_DIM, q, zero)
    qz_ref[:, t:2 * t] = jnp.where(row >= HEAD_DIM, q, zero)


def _scores_chunk(k_ref, kb, qz_ref, dst, c, tk, cw):
    s_ref, top_ref = dst
    kblk = k_ref[pl.ds(pl.multiple_of(kb * tk, tk), tk), :]
    cols = slice(c * cw, (c + 1) * cw)
    s = jnp.dot(kblk, qz_ref[:, cols], preferred_element_type=F32)
    s_ref[:, cols] = s
    top_ref[:, cols] = jnp.max(s, axis=0, keepdims=True)


def _block_step(kb, src, kb_next, dst, refs, *, t, tk, cw, lead):
    k_ref, vT_ref, qz_ref, m_ref, l_ref, acc_ref = refs
    s_ref, top_ref = src
    nchunk = 2 * t // cw
    v = vT_ref[kb]
    if kb_next is not None:
        for c in range(lead):
            _scores_chunk(k_ref, kb_next, qz_ref, dst, c, tk, cw)
    for c in range(nchunk):
        if kb_next is not None and c + lead < nchunk:
            _scores_chunk(k_ref, kb_next, qz_ref, dst, c + lead, tk, cw)
        u, cc = divmod(c, t // cw)
        cols = slice(c * cw, (c + 1) * cw)
        acc_at = acc_ref.at[u, :, cc * cw:(cc + 1) * cw]
        m_old = m_ref[:, cols]
        m_new = jnp.maximum(m_old, top_ref[:, cols])
        alpha = jnp.exp2(m_old - m_new)
        p = jnp.exp2(s_ref[:, cols] - m_new)
        l_ref[:, cols] = alpha * l_ref[:, cols] + jnp.sum(p, axis=0, keepdims=True)
        m_ref[:, cols] = m_new
        acc_at[...] = alpha * acc_at[...] + jnp.dot(v, p.astype(BF16), preferred_element_type=F32)


def _diff_attn_kernel(lam_ref, subg_ref, qT_ref, k_ref, vT_ref, oT_ref, qz_ref, m_ref, l_ref, acc_ref,
                      sa_ref, sb_ref, ta_ref, tb_ref, *, t, tk, nkb, cw, lead, per_trip, lam_init):
    _stage_queries(qT_ref, qz_ref, t)
    m_ref[...] = jnp.full(m_ref.shape, NEG_INF, F32)
    l_ref[...] = jnp.zeros(l_ref.shape, F32)
    acc_ref[...] = jnp.zeros(acc_ref.shape, F32)

    bufs = ((sa_ref, ta_ref), (sb_ref, tb_ref))
    step = functools.partial(_block_step, refs=(k_ref, vT_ref, qz_ref, m_ref, l_ref, acc_ref),
                             t=t, tk=tk, cw=cw, lead=lead)

    for c in range(2 * t // cw):
        _scores_chunk(k_ref, 0, qz_ref, bufs[0], c, tk, cw)

    def trip(j, carry):
        for i in range(per_trip):
            step(per_trip * j + i, bufs[i % 2], per_trip * j + i + 1, bufs[(i + 1) % 2])
        return carry

    ntrip = nkb // per_trip - 1
    lax.fori_loop(0, ntrip, trip, 0)
    for i in range(per_trip):
        kb = ntrip * per_trip + i
        step(kb, bufs[i % 2], kb + 1 if i + 1 < per_trip else None, bufs[(i + 1) % 2])

    lv = lam_ref[...]
    e1 = jnp.exp(jnp.sum(lv[0:1] * lv[1:2], axis=-1, keepdims=True))
    e2 = jnp.exp(jnp.sum(lv[2:3] * lv[3:4], axis=-1, keepdims=True))
    lam = e1 - e2 + lam_init
    o = acc_ref[0] / l_ref[:, 0:t] - lam * (acc_ref[1] / l_ref[:, t:2 * t])
    ms = jnp.mean(o * o, axis=0, keepdims=True)
    o = o * lax.rsqrt(ms + EPS) * subg_ref[...] * (1.0 - lam_init)
    oT_ref[...] = o.astype(BF16)


def _diff_attention(qT, k, vT, lamv, subg, *, lam_init):
    b, nqb, nq, t = qT.shape
    s = k.shape[1]
    nkb, tk = vT.shape[1], vT.shape[3]
    heads = nq // (2 * HEAD_DIM)
    assert nkb % DENSE_PER_TRIP == 0 and DENSE_PER_TRIP % 2 == 0
    kern = functools.partial(_diff_attn_kernel, t=t, tk=tk, nkb=nkb, cw=ATTN_CHUNK, lead=ATTN_LEAD,
                             per_trip=DENSE_PER_TRIP, lam_init=lam_init)
    return pl.pallas_call(
        kern,
        grid=(b, heads, nqb),
        in_specs=[
            pl.BlockSpec(lamv.shape, lambda bi, h, i: (0, 0)),
            pl.BlockSpec(subg.shape, lambda bi, h, i: (0, 0)),
            pl.BlockSpec((None, None, 2 * HEAD_DIM, t), lambda bi, h, i: (bi, i, h, 0)),
            pl.BlockSpec((None, s, 2 * HEAD_DIM), lambda bi, h, i: (bi, 0, h)),
            pl.BlockSpec((None, nkb, 2 * HEAD_DIM, tk), lambda bi, h, i: (bi, 0, h, 0)),
        ],
        out_specs=pl.BlockSpec((None, None, 2 * HEAD_DIM, t), lambda bi, h, i: (bi, i, h, 0)),
        out_shape=jax.ShapeDtypeStruct((b, nqb, nq, t), BF16),
        scratch_shapes=[
            pltpu.VMEM((2 * HEAD_DIM, 2 * t), BF16),
            pltpu.VMEM((1, 2 * t), F32),
            pltpu.VMEM((1, 2 * t), F32),
            pltpu.VMEM((2, 2 * HEAD_DIM, t), F32),
            pltpu.VMEM((tk, 2 * t), F32),
            pltpu.VMEM((tk, 2 * t), F32),
            pltpu.VMEM((1, 2 * t), F32),
            pltpu.VMEM((1, 2 * t), F32),
        ],
        compiler_params=_params("parallel", "parallel", "arbitrary"),
        name="diff_attention",
    )(lamv, subg, qT, k, vT)


def _in_proj_tm_kernel(h_ref, g_ref, w_ref, kc_ref, ka_ref, kb_ref, q_ref, k_ref, v_ref, *, q_scale):
    u = _rms(h_ref[...], g_ref[...]).astype(BF16)
    y = jnp.dot(u, w_ref[...], preferred_element_type=F32)
    kc, ka, kb = kc_ref[...], ka_ref[...], kb_ref[...]
    nq, nk = q_ref.shape[1], k_ref.shape[1]

    def rope(j):
        x = y[:, j * LANES:(j + 1) * LANES]
        return x * kc + pltpu.roll(x, LANES - ROT_HALF, 1) * ka + pltpu.roll(x, ROT_HALF, 1) * kb

    for j in range(nq // LANES):
        q_ref[:, j * LANES:(j + 1) * LANES] = (rope(j) * q_scale).astype(q_ref.dtype)
    for j in range(nk // LANES):
        k_ref[:, j * LANES:(j + 1) * LANES] = rope(nq // LANES + j).astype(k_ref.dtype)
    v_ref[...] = y[:, nq + nk:].astype(v_ref.dtype)


def _in_proj_tm(h, g, w, kc, ka, kb, *, nq, nk, q_scale, out_dtype):
    b, s, d = h.shape
    nv = w.shape[1] - nq - nk
    tm = ROW_TILE
    const = lambda bi, i: (0, 0)
    row = lambda bi, i: (bi, i, 0)
    return pl.pallas_call(
        functools.partial(_in_proj_tm_kernel, q_scale=q_scale),
        grid=(b, s // tm),
        in_specs=[
            pl.BlockSpec((None, tm, d), row),
            pl.BlockSpec((1, d), const),
            pl.BlockSpec(w.shape, const),
            pl.BlockSpec((None, tm, LANES), row),
            pl.BlockSpec((None, tm, LANES), row),
            pl.BlockSpec((None, tm, LANES), row),
        ],
        out_specs=[pl.BlockSpec((None, tm, n), row) for n in (nq, nk, nv)],
        out_shape=[jax.ShapeDtypeStruct((b, s, n), out_dtype) for n in (nq, nk, nv)],
        compiler_params=_params("parallel", "parallel"),
        name="mixer_in_proj_tm",
    )(h, g, w, kc, ka, kb)


def _window_attn_kernel(*refs, radius, dils, per_trip, depth, has_sink):
    if has_sink:
        sink_ref, refs = refs[0], refs[1:]
    bias_ref, q_ref, k_ref, v_ref = refs[:4]
    outs, scratch = refs[4:4 + 2 * len(dils)], refs[4 + 2 * len(dils):]
    bufs, scratch = scratch[:depth], scratch[depth:]
    pair = pl.program_id(1)
    total = q_ref.shape[0]
    win = WIN_Q + 2 * radius
    nqb = total // WIN_Q
    lane = lax.broadcasted_iota(jnp.int32, (WIN_Q, 2 * HEAD_DIM), 1)
    if q_ref.dtype == F32:
        qg_ref, kg_ref, vg_ref = scratch[:3]
        onat_ref = scratch[3] if max(dils) > 1 else None
    else:
        qg_ref, kg_ref, vg_ref = q_ref, k_ref, v_ref

    for branch, dil in enumerate(dils):
        o_ref, lse_ref = outs[2 * branch], outs[2 * branch + 1]
        seq = total // dil
        if q_ref.dtype == F32:
            for src, dst in ((q_ref, qg_ref), (k_ref, kg_ref), (v_ref, vg_ref)):
                if dil > REGROUP_STRIDE:
                    assert dil == REGROUP_STRIDE ** 2
                    part = total // REGROUP_STRIDE
                    for r1 in range(REGROUP_STRIDE):
                        onat_ref[r1 * part:(r1 + 1) * part, :] = src[pl.ds(r1, part, stride=REGROUP_STRIDE), :]
                    src, outer = onat_ref, REGROUP_STRIDE
                else:
                    outer = 1
                for r in range(dil):
                    r1, r2 = r % outer, r // outer
                    rows = (pl.ds(r1 * (total // outer) + r2, seq, stride=dil // outer) if dil > 1
                            else slice(None))
                    dst[r * seq:(r + 1) * seq, :] = src[rows, :].astype(BF16)
        _window_branch(bias_ref, sink_ref if has_sink else None, qg_ref, kg_ref, vg_ref, o_ref, lse_ref,
                       onat_ref if dil > 1 else None, bufs, lane, pair, radius=radius, dil=dil, seq=seq,
                       win=win, nqb=nqb, per_trip=per_trip, depth=depth)


def _window_branch(bias_ref, sink_ref, qg_ref, kg_ref, vg_ref, o_ref, lse_ref, onat_ref, bufs, lane, pair, *,
                   radius, dil, seq, win, nqb, per_trip, depth):
    has_sink = sink_ref is not None

    def window(i):
        q0 = i * WIN_Q
        lo = (q0 // seq) * seq
        k0 = jnp.clip(q0 - radius, lo, lo + seq - win)
        return pl.multiple_of(q0, WIN_Q), pl.multiple_of(k0, radius), (q0 - k0) // radius

    def scores(i, dst_ref):
        q0, k0, _ = window(i)
        q = qg_ref[pl.ds(q0, WIN_Q), :]
        zero = jnp.zeros_like(q)
        qz = jnp.concatenate([jnp.where(lane < HEAD_DIM, q, zero), jnp.where(lane >= HEAD_DIM, q, zero)],
                             axis=0)
        dst_ref[...] = lax.dot_general(kg_ref[pl.ds(k0, win), :], qz, _NT, preferred_element_type=F32)

    def finish(i, src_ref):
        q0, k0, bidx = window(i)
        bias = bias_ref[bidx]
        s = src_ref[...] + jnp.concatenate([bias, bias], axis=1)
        m = jnp.max(s, axis=0, keepdims=True)
        if has_sink:
            unit = lax.broadcasted_iota(jnp.int32, m.shape, 1) // WIN_Q
            sk = jnp.where(unit == 0, sink_ref[pair * 2], sink_ref[pair * 2 + 1])
            m = jnp.maximum(m, sk)
        p = jnp.exp2(s - m)
        l = jnp.sum(p, axis=0, keepdims=True)
        if has_sink:
            l = l + jnp.exp2(sk - m)
        oT = lax.dot_general(vg_ref[pl.ds(k0, win), :], p.astype(BF16), _TN, preferred_element_type=F32)
        lse = m + jnp.log2(l)
        halves = []
        for u in range(2):
            cols = slice(u * WIN_Q, (u + 1) * WIN_Q)
            halves.append(oT[u * HEAD_DIM:(u + 1) * HEAD_DIM, cols] / l[:, cols])
            lse_ref[u, pl.ds(i, 1), :] = lse[:, cols]
        o = jnp.concatenate(halves, axis=0).T
        if dil > 1:
            r = q0 // seq
            onat_ref[pl.ds(r + (q0 - r * seq) * dil, WIN_Q, stride=dil), :] = o
        else:
            o_ref[pl.ds(q0, WIN_Q), :] = o.astype(BF16)

    for n in range(depth - 1):
        scores(n, bufs[n])

    def trip(j, carry):
        for n in range(per_trip):
            i = per_trip * j + n
            scores(i + depth - 1, bufs[(n + depth - 1) % depth])
            finish(i, bufs[n % depth])
        return carry

    ntrip = nqb // per_trip - 1
    lax.fori_loop(0, ntrip, trip, 0)
    for n in range(per_trip):
        i = ntrip * per_trip + n
        if n + depth - 1 < per_trip:
            scores(i + depth - 1, bufs[(n + depth - 1) % depth])
        finish(i, bufs[n % depth])
    if dil > 1:
        o_ref[...] = onat_ref[...].astype(BF16)


def _window_bias(radius):
    win = WIN_Q + 2 * radius
    i = np.arange(win)[:, None]
    j = np.arange(WIN_Q)[None, :]
    return jnp.asarray(np.stack([np.where(np.abs(i - j - b * radius) <= radius, 0.0, NEG_INF)
                                 for b in range(3)]), F32)


def _window_attention(q, k, v, sink, *, radius, dils):
    b, s, nq = q.shape
    nqb = s // WIN_Q
    npairs = nq // LANES
    win = WIN_Q + 2 * radius
    shared = k.shape[2] == LANES and npairs > 1
    has_sink = sink is not None
    per_trip = min(WIN_PER_TRIP, nqb)
    depth = min(WIN_DEPTH, per_trip)
    assert all((s // dil) % WIN_Q == 0 and s // dil >= win for dil in dils) and WIN_Q % radius == 0
    assert nqb % per_trip == 0 and (per_trip % depth == 0 or per_trip == nqb)
    assert q.dtype == F32 or max(dils) == 1
    qspec = pl.BlockSpec((None, s, LANES), lambda bi, p: (bi, 0, p))
    kvspec = pl.BlockSpec((None, s, LANES), lambda bi, p: (bi, 0, 0)) if shared else qspec
    lse_spec = pl.BlockSpec((None, None, 2, nqb, WIN_Q), lambda bi, p: (bi, p, 0, 0, 0))
    bias = _window_bias(radius)
    in_specs = [pl.BlockSpec(bias.shape, lambda bi, p: (0, 0, 0)), qspec, kvspec, kvspec]
    args = [bias, q, k, v]
    if has_sink:
        in_specs = [pl.BlockSpec(memory_space=pltpu.SMEM)] + in_specs
        args = [sink] + args
    scratch = [pltpu.VMEM((win, 2 * WIN_Q), F32)] * depth
    if q.dtype == F32:
        scratch += [pltpu.VMEM((s, LANES), BF16)] * 3
    if max(dils) > 1:
        scratch += [pltpu.VMEM((s, LANES), F32)]
    outs = pl.pallas_call(
        functools.partial(_window_attn_kernel, radius=radius, dils=dils, per_trip=per_trip, depth=depth,
                          has_sink=has_sink),
        grid=(b, npairs),
        in_specs=in_specs,
        out_specs=[qspec, lse_spec] * len(dils),
        out_shape=[jax.ShapeDtypeStruct((b, s, nq), BF16),
                   jax.ShapeDtypeStruct((b, npairs, 2, nqb, WIN_Q), F32)] * len(dils),
        scratch_shapes=scratch,
        compiler_params=_params("parallel", "parallel"),
        name="window_attention",
    )(*args)
    results = []
    for n, dil in enumerate(dils):
        o, lse = outs[2 * n], outs[2 * n + 1]
        lse = lse.reshape(b, 2 * npairs, dil, s // dil).transpose(0, 3, 2, 1).reshape(b, s, 2 * npairs)
        results.append((o, lse))
    return results


def _mixer_out(mixer_refs, wout_ref, mode):
    if mode == "feature_major":
        (oT_ref,) = mixer_refs
        ys = [lax.dot_general(oT_ref[j], wout_ref[...], _TN, preferred_element_type=F32)
              for j in range(oT_ref.shape[0])]
        return jnp.concatenate(ys, axis=0) if len(ys) > 1 else ys[0]
    if mode == "token_major":
        (o_ref,) = mixer_refs
        return jnp.dot(o_ref[...], wout_ref[...], preferred_element_type=F32)
    n = (len(mixer_refs) - 1) // 2
    o_refs, lse_refs, expand_ref = mixer_refs[:n], mixer_refs[n:2 * n], mixer_refs[2 * n]
    lses = [r[...] for r in lse_refs]
    top = functools.reduce(jnp.maximum, lses)
    es = [jnp.exp2(x - top) for x in lses]
    z = functools.reduce(jnp.add, es)
    o = None
    for e, o_ref in zip(es, o_refs):
        w = e / z
        hi = w.astype(BF16)
        lo = (w - hi.astype(F32)).astype(BF16)
        wide = jnp.dot(jnp.concatenate([hi, lo], axis=1), expand_ref[...], preferred_element_type=F32)
        term = wide * o_ref[...].astype(F32)
        o = term if o is None else o + term
    return jnp.dot(o.astype(BF16), wout_ref[...], preferred_element_type=F32)


def _mid_kernel(*refs, mode, n_mixer, x_scale):
    h_ref = refs[0]
    mixer_refs = refs[1:1 + n_mixer]
    wout_ref, gmix_ref, gpre_ref, wq_ref, kT_ref, v_ref, wo_ref, gpost_ref, out_ref = refs[1 + n_mixer:]
    y = _mixer_out(mixer_refs, wout_ref, mode)
    h1 = h_ref[...] + _rms(y, gmix_ref[...])

    u = _rms(h1, gpre_ref[...]).astype(BF16)
    q = (jnp.dot(u, wq_ref[...], preferred_element_type=F32) * x_scale).astype(BF16)
    xd = q.shape[1] // X_HEADS
    outs = []
    for hd in range(X_HEADS):
        s = jnp.dot(q[:, hd * xd:(hd + 1) * xd], kT_ref[hd * xd:(hd + 1) * xd, :],
                    preferred_element_type=F32)
        p = jnp.exp2(s - jnp.max(s, axis=-1, keepdims=True))
        l = jnp.sum(p, axis=-1, keepdims=True)
        o = jnp.dot(p.astype(BF16), v_ref[:, hd * xd:(hd + 1) * xd], preferred_element_type=F32)
        outs.append((o / l).astype(BF16))
    y2 = jnp.dot(jnp.concatenate(outs, axis=1), wo_ref[...], preferred_element_type=F32)
    out_ref[...] = h1 + _rms(y2, gpost_ref[...])


def _mid(h, mixer, mode, wout, gmix, gpre, wq, kT, v, wo, gpost, *, x_scale):
    b, s, d = h.shape
    tm = ROW_TILE
    n_mem = v.shape[1]
    const = lambda bi, i: (0, 0)
    row = lambda bi, i: (bi, i, 0)
    if mode == "feature_major":
        oblk = mixer[0].shape[3]
        mixer_specs = [pl.BlockSpec((None, tm // oblk, d, oblk), lambda bi, i: (bi, i, 0, 0))]
    else:
        mixer_specs = [pl.BlockSpec((None, tm, a.shape[2]), row) for a in mixer]
    if mode == "branches":
        heads = mixer[-1].shape[2]
        expand = jnp.asarray(np.tile(np.repeat(np.eye(heads), d // heads, axis=1), (2, 1)), BF16)
        mixer = list(mixer) + [expand]
        mixer_specs.append(pl.BlockSpec(expand.shape, const))
    kern = functools.partial(_mid_kernel, mode=mode, n_mixer=len(mixer), x_scale=x_scale)
    return pl.pallas_call(
        kern,
        grid=(b, s // tm),
        in_specs=[
            pl.BlockSpec((None, tm, d), row),
            *mixer_specs,
            pl.BlockSpec((d, d), const),
            pl.BlockSpec((1, d), const),
            pl.BlockSpec((1, d), const),
            pl.BlockSpec((d, d), const),
            pl.BlockSpec((None, d, n_mem), lambda bi, i: (bi, 0, 0)),
            pl.BlockSpec((None, n_mem, d), lambda bi, i: (bi, 0, 0)),
            pl.BlockSpec((d, d), const),
            pl.BlockSpec((1, d), const),
        ],
        out_specs=pl.BlockSpec((None, tm, d), lambda bi, i: (bi, i, 0)),
        out_shape=jax.ShapeDtypeStruct((b, s, d), F32),
        compiler_params=_params("parallel", "parallel"),
        name="out_proj_cross_attention",
    )(h, *mixer, wout, gmix, gpre, wq, kT, v, wo, gpost)


def _mem_kv_kernel(mem_ref, g_ref, wkT_ref, wv_ref, kT_ref, v_ref):
    mn = _rms(mem_ref[...], g_ref[...]).astype(BF16)
    kT_ref[...] = lax.dot_general(wkT_ref[...], mn, _NT, preferred_element_type=F32).astype(BF16)
    v_ref[...] = jnp.dot(mn, wv_ref[...], preferred_element_type=F32).astype(BF16)


def _mem_kv(mem, g, wkT, wv):
    depth, d = g.shape[0], g.shape[2]
    b, n_mem, _ = mem.shape
    return pl.pallas_call(
        _mem_kv_kernel,
        grid=(depth, b),
        in_specs=[
            pl.BlockSpec((None, n_mem, d), lambda li, bi: (bi, 0, 0)),
            pl.BlockSpec((None, 1, d), lambda li, bi: (li, 0, 0)),
            pl.BlockSpec((None, d, d), lambda li, bi: (li, 0, 0)),
            pl.BlockSpec((None, d, d), lambda li, bi: (li, 0, 0)),
        ],
        out_specs=[
            pl.BlockSpec((None, None, d, n_mem), lambda li, bi: (li, bi, 0, 0)),
            pl.BlockSpec((None, None, n_mem, d), lambda li, bi: (li, bi, 0, 0)),
        ],
        out_shape=[
            jax.ShapeDtypeStruct((depth, b, d, n_mem), BF16),
            jax.ShapeDtypeStruct((depth, b, n_mem, d), BF16),
        ],
        compiler_params=_params("parallel", "parallel"),
        name="memory_kv",
    )(mem, g, wkT, wv)


def _ffn_kernel(h_ref, gpre_ref, wgu_ref, wd_ref, gpost_ref, out_ref, acc_ref):
    dff = wd_ref.shape[0]
    u = _rms(h_ref[...], gpre_ref[...]).astype(BF16)
    for c in range(dff // FF_CHUNK):
        cols = slice(c * FF_CHUNK, (c + 1) * FF_CHUNK)
        g = jnp.dot(u, wgu_ref[:, cols], preferred_element_type=F32)
        up = jnp.dot(u, wgu_ref[:, dff + c * FF_CHUNK:dff + (c + 1) * FF_CHUNK], preferred_element_type=F32)
        a = (g / (1.0 + jnp.exp(-g)) * up).astype(BF16)
        part = jnp.dot(a, wd_ref[cols, :], preferred_element_type=F32)
        acc_ref[...] = part if c == 0 else acc_ref[...] + part
    out_ref[...] = h_ref[...] + _rms(acc_ref[...], gpost_ref[...])


def _ffn(h, gpre, wgu, wd, gpost):
    b, s, d = h.shape
    tm = ROW_TILE
    dff = wd.shape[0]
    assert dff % FF_CHUNK == 0
    rows = b * s
    h2 = h.reshape(rows, d)
    const = lambda i: (0, 0)
    resident = dict(pipeline_mode=pl.Buffered(1))
    out = pl.pallas_call(
        _ffn_kernel,
        grid=(rows // tm,),
        in_specs=[
            pl.BlockSpec((tm, d), lambda i: (i, 0)),
            pl.BlockSpec((1, d), const),
            pl.BlockSpec((d, 2 * dff), const, **resident),
            pl.BlockSpec((dff, d), const, **resident),
            pl.BlockSpec((1, d), const),
        ],
        out_specs=pl.BlockSpec((tm, d), lambda i: (i, 0)),
        out_shape=jax.ShapeDtypeStruct((rows, d), F32),
        scratch_shapes=[pltpu.VMEM((tm, d), F32)],
        compiler_params=_params("parallel"),
        name="swiglu_ffn",
    )(h2, gpre, wgu, wd, gpost)
    return out.reshape(b, s, d)


def _rope_tables(positions):
    inv_freq = ROPE_THETA ** (-jnp.arange(0, 2 * ROT_HALF, 2, dtype=F32) / (2 * ROT_HALF))
    ang = positions.astype(F32)[..., None] * inv_freq
    cos, sin = jnp.cos(ang), jnp.sin(ang)
    cosT, sinT = cos.transpose(0, 2, 1), sin.transpose(0, 2, 1)
    zeros = jnp.zeros_like(cos)
    pad = HEAD_DIM - 2 * ROT_HALF
    ones_tail = jnp.ones(cos.shape[:-1] + (pad,), F32)
    zero_tail = jnp.zeros(cos.shape[:-1] + (pad,), F32)
    reps = LANES // HEAD_DIM
    kc = jnp.tile(jnp.concatenate([cos, cos, ones_tail], axis=-1), reps)
    ka = jnp.tile(jnp.concatenate([-sin, zeros, zero_tail], axis=-1), reps)
    kb = jnp.tile(jnp.concatenate([zeros, sin, zero_tail], axis=-1), reps)
    return cosT, sinT, kc, ka, kb


def _row(g):
    return g.reshape(1, -1)


def kernel(x, mem, positions, mix_pre_g, mix_post_g, mem_pre_g, mem_kv_g, mem_post_g, ffn_pre_g, ffn_post_g,
           a_w_in, a_w_out, b_w_in, b_w_out, b_lam_q1, b_lam_k1, b_lam_q2, b_lam_k2, b_sub_g, c_w_in, c_w_out,
           c_sink, x_wq, x_wkv, x_wo, w_gate_up, w_down):
    depth, d = mix_pre_g.shape
    assert d % (2 * HEAD_DIM) == 0 and x.shape[1] % DENSE_BLOCK == 0 and x.shape[1] % ROW_TILE == 0
    cosT, sinT, kc, ka, kb = _rope_tables(positions)
    q_scale = HEAD_DIM ** -0.5 * LOG2E
    x_scale = (d // X_HEADS) ** -0.5 * LOG2E

    mem_kT, mem_v = _mem_kv(mem, mem_kv_g.reshape(depth, 1, d),
                            x_wkv[:, :, :d].transpose(0, 2, 1).astype(BF16), x_wkv[:, :, d:].astype(BF16))

    h = x
    for i in range(depth):
        kind, j = i % N_MIXERS, i // N_MIXERS
        g_pre = _row(mix_pre_g[i])
        if kind == 0:
            w_in, w_out = a_w_in[j], a_w_out[j]
            q, k, v = _in_proj_tm(h, g_pre, w_in.astype(BF16), kc, ka, kb, nq=d, nk=d, q_scale=q_scale,
                                  out_dtype=F32)
            radii = {window // (2 * dil) for window, dil in A_PATTERNS}
            assert len(radii) == 1
            branches = _window_attention(q, k, v, None, radius=radii.pop(), dils=tuple(dl for _, dl in A_PATTERNS))
            mixer, mode = [o for o, _ in branches] + [lse for _, lse in branches], "branches"
        elif kind == 1:
            w_in, w_out = b_w_in[j], b_w_out[j]
            wq, wk, wv = w_in[:, :d], w_in[:, d:2 * d], w_in[:, 2 * d:]
            qT, k, vT = _in_proj(h, g_pre, wq.T.astype(BF16), wk.astype(BF16), wv.T.astype(BF16),
                                 cosT, sinT, kc, ka, kb, q_scale=q_scale, qblk=DENSE_BLOCK, vblk=DENSE_KBLOCK)
            lam_init = 0.8 - 0.6 * math.exp(-0.3 * i)
            lamv = jnp.stack([b_lam_q1[j], b_lam_k1[j], b_lam_q2[j], b_lam_k2[j]]).astype(F32)
            subg = jnp.broadcast_to(b_sub_g[j].astype(F32)[:, None], (2 * HEAD_DIM, DENSE_BLOCK))
            mixer, mode = [_diff_attention(qT, k, vT, lamv, subg, lam_init=lam_init)], "feature_major"
        else:
            w_in, w_out = c_w_in[j], c_w_out[j]
            n_kv = (w_in.shape[1] - d) // (2 * HEAD_DIM)
            grp = (d // HEAD_DIM) // n_kv
            perm = np.arange(d).reshape(n_kv, grp, HEAD_DIM).transpose(1, 0, 2).reshape(-1)
            w_in = jnp.concatenate([w_in[:, :d][:, perm], w_in[:, d:]], axis=1)
            w_out = w_out[perm, :]
            q, k, v = _in_proj_tm(h, g_pre, w_in.astype(BF16), kc, ka, kb, nq=d, nk=n_kv * HEAD_DIM,
                                  q_scale=q_scale, out_dtype=BF16)
            sink = (c_sink[j].astype(F32) * LOG2E)[perm[::HEAD_DIM] // HEAD_DIM]
            mixer, mode = [_window_attention(q, k, v, sink, radius=C_RADIUS, dils=(1,))[0][0]], "token_major"
        h = _mid(h, mixer, mode, w_out.astype(BF16), _row(mix_post_g[i]), _row(mem_pre_g[i]),
                 x_wq[i].astype(BF16), mem_kT[i], mem_v[i], x_wo[i].astype(BF16), _row(mem_post_g[i]),
                 x_scale=x_scale)
        h = _ffn(h, _row(ffn_pre_g[i]), w_gate_up[i].astype(BF16), w_down[i].astype(BF16), _row(ffn_post_g[i]))
    return h
```

```python
import functools
import math

import jax
import jax.numpy as jnp
import numpy as np
from jax import lax
from jax.experimental import pallas as pl
from jax.experimental.pallas import tpu as pltpu

F32 = jnp.float32
BF16 = jnp.bfloat16

HEAD_DIM = 64
ROT_HALF = HEAD_DIM // 8
ROPE_THETA = 500000.0
EPS = 1e-6
NEG_INF = -1e30
LOG2E = 1.4426950408889634
N_MIXERS = 3

A_PATTERNS = ((128, 1), (512, 4), (2048, 16))
C_RADIUS = 128
X_HEADS = 4

LANES = 128
ROW_TILE = 1024
DENSE_BLOCK = 1024
DENSE_KBLOCK = 512
ATTN_CHUNK = 256
ATTN_LEAD = 1
DENSE_PER_TRIP = 4
WIN_Q = 128
WIN_PER_TRIP = 16
REGROUP_STRIDE = 4
WIN_DEPTH = 4
FF_CHUNK = 256
VMEM_LIMIT = 56 * 1024 * 1024

_NT = (((1,), (1,)), ((), ()))
_TN = (((0,), (0,)), ((), ()))


def _params(*sem):
    return pltpu.CompilerParams(dimension_semantics=sem, vmem_limit_bytes=VMEM_LIMIT)


def _rms(x, g):
    ms = jnp.mean(x * x, axis=-1, keepdims=True)
    return x * lax.rsqrt(ms + EPS) * g


def _in_proj_kernel(h_ref, g_ref, wqT_ref, wk_ref, wvT_ref, cosT_ref, sinT_ref, kc_ref, ka_ref, kb_ref,
                    qT_ref, k_ref, vT_ref, *, q_scale, qblk, vblk):
    tm = h_ref.shape[0]
    u = _rms(h_ref[...], g_ref[...]).astype(BF16)

    kf = jnp.dot(u, wk_ref[...], preferred_element_type=F32)
    kc, ka, kb = kc_ref[...], ka_ref[...], kb_ref[...]
    for j in range(kf.shape[1] // LANES):
        x = kf[:, j * LANES:(j + 1) * LANES]
        y = x * kc + pltpu.roll(x, LANES - ROT_HALF, 1) * ka + pltpu.roll(x, ROT_HALF, 1) * kb
        k_ref[:, j * LANES:(j + 1) * LANES] = y.astype(BF16)

    qf = lax.dot_general(wqT_ref[...], u, _NT, preferred_element_type=F32)
    c = cosT_ref[...] * q_scale
    s = sinT_ref[...] * q_scale
    for unit in range(qf.shape[0] // HEAD_DIM):
        r0 = unit * HEAD_DIM
        t1 = qf[r0:r0 + ROT_HALF]
        t2 = qf[r0 + ROT_HALF:r0 + 2 * ROT_HALF]
        rest = qf[r0 + 2 * ROT_HALF:r0 + HEAD_DIM] * q_scale
        blk = jnp.concatenate([t1 * c - t2 * s, t2 * c + t1 * s, rest], axis=0).astype(BF16)
        for jb in range(tm // qblk):
            qT_ref[jb, r0:r0 + HEAD_DIM, :] = blk[:, jb * qblk:(jb + 1) * qblk]

    vf = lax.dot_general(wvT_ref[...], u, _NT, preferred_element_type=F32).astype(BF16)
    for jb in range(tm // vblk):
        vT_ref[jb] = vf[:, jb * vblk:(jb + 1) * vblk]


def _in_proj(h, g, wqT, wk, wvT, cosT, sinT, kc, ka, kb, *, q_scale, qblk, vblk):
    b, s, d = h.shape
    nq, nk, nv = wqT.shape[0], wk.shape[1], wvT.shape[0]
    tm = ROW_TILE
    kern = functools.partial(_in_proj_kernel, q_scale=q_scale, qblk=qblk, vblk=vblk)
    const = lambda bi, i: (0, 0)
    return pl.pallas_call(
        kern,
        grid=(b, s // tm),
        in_specs=[
            pl.BlockSpec((None, tm, d), lambda bi, i: (bi, i, 0)),
            pl.BlockSpec((1, d), const),
            pl.BlockSpec((nq, d), const),
            pl.BlockSpec((d, nk), const),
            pl.BlockSpec((nv, d), const),
            pl.BlockSpec((None, ROT_HALF, tm), lambda bi, i: (bi, 0, i)),
            pl.BlockSpec((None, ROT_HALF, tm), lambda bi, i: (bi, 0, i)),
            pl.BlockSpec((None, tm, LANES), lambda bi, i: (bi, i, 0)),
            pl.BlockSpec((None, tm, LANES), lambda bi, i: (bi, i, 0)),
            pl.BlockSpec((None, tm, LANES), lambda bi, i: (bi, i, 0)),
        ],
        out_specs=[
            pl.BlockSpec((None, tm // qblk, nq, qblk), lambda bi, i: (bi, i, 0, 0)),
            pl.BlockSpec((None, tm, nk), lambda bi, i: (bi, i, 0)),
            pl.BlockSpec((None, tm // vblk, nv, vblk), lambda bi, i: (bi, i, 0, 0)),
        ],
        out_shape=[
            jax.ShapeDtypeStruct((b, s // qblk, nq, qblk), BF16),
            jax.ShapeDtypeStruct((b, s, nk), BF16),
            jax.ShapeDtypeStruct((b, s // vblk, nv, vblk), BF16),
        ],
        compiler_params=_params("parallel", "parallel"),
        name="mixer_in_proj",
    )(h, g, wqT, wk, wvT, cosT, sinT, kc, ka, kb)


def _stage_queries(qT_ref, qz_ref, t):
    q = qT_ref[...]
    row = lax.broadcasted_iota(jnp.int32, q.shape, 0)
    zero = jnp.zeros_like(q)
    qz_ref[:, 0:t] = jnp.where(row < HEAD_DIM, q, zero)
    qz_ref[:, t:2 * t] = jnp.where(row >= HEAD_DIM, q, zero)


def _scores_chunk(k_ref, kb, qz_ref, dst, c, tk, cw):
    s_ref, top_ref = dst
    kblk = k_ref[pl.ds(pl.multiple_of(kb * tk, tk), tk), :]
    cols = slice(c * cw, (c + 1) * cw)
    s = jnp.dot(kblk, qz_ref[:, cols], preferred_element_type=F32)
    s_ref[:, cols] = s
    top_ref[:, cols] = jnp.max(s, axis=0, keepdims=True)


def _block_step(kb, src, kb_next, dst, refs, *, t, tk, cw, lead):
    k_ref, vT_ref, qz_ref, m_ref, l_ref, acc_ref = refs
    s_ref, top_ref = src
    nchunk = 2 * t // cw
    v = vT_ref[kb]
    v1 = jnp.concatenate([v, jnp.ones((16, v.shape[1]), BF16)], axis=0)
    if kb_next is not None:
        for c in range(lead):
            _scores_chunk(k_ref, kb_next, qz_ref, dst, c, tk, cw)
    for c in range(nchunk):
        if kb_next is not None and c + lead < nchunk:
            _scores_chunk(k_ref, kb_next, qz_ref, dst, c + lead, tk, cw)
        u, cc = divmod(c, t // cw)
        cols = slice(c * cw, (c + 1) * cw)
        acc_at = acc_ref.at[u, :, cc * cw:(cc + 1) * cw]
        m_old = m_ref[:, cols]
        m_new = jnp.maximum(m_old, top_ref[:, cols])
        alpha = jnp.exp2(m_old - m_new)
        p = jnp.exp2(s_ref[:, cols] - m_new).astype(BF16)
        pv = jnp.dot(v1, p, preferred_element_type=F32)
        l_ref[:, cols] = alpha * l_ref[:, cols] + pv[2 * HEAD_DIM:2 * HEAD_DIM + 1]
        m_ref[:, cols] = m_new
        acc_at[...] = alpha * acc_at[...] + pv[:2 * HEAD_DIM]


def _diff_attn_kernel(lam_ref, subg_ref, qT_ref, k_ref, vT_ref, oT_ref, qz_ref, m_ref, l_ref, acc_ref,
                      sa_ref, sb_ref, ta_ref, tb_ref, *, t, tk, nkb, cw, lead, per_trip, lam_init):
    _stage_queries(qT_ref, qz_ref, t)
    m_ref[...] = jnp.full(m_ref.shape, NEG_INF, F32)
    l_ref[...] = jnp.zeros(l_ref.shape, F32)
    acc_ref[...] = jnp.zeros(acc_ref.shape, F32)

    bufs = ((sa_ref, ta_ref), (sb_ref, tb_ref))
    step = functools.partial(_block_step, refs=(k_ref, vT_ref, qz_ref, m_ref, l_ref, acc_ref),
                             t=t, tk=tk, cw=cw, lead=lead)

    for c in range(2 * t // cw):
        _scores_chunk(k_ref, 0, qz_ref, bufs[0], c, tk, cw)

    def trip(j, carry):
        for i in range(per_trip):
            step(per_trip * j + i, bufs[i % 2], per_trip * j + i + 1, bufs[(i + 1) % 2])
        return carry

    ntrip = nkb // per_trip - 1
    lax.fori_loop(0, ntrip, trip, 0)
    for i in range(per_trip):
        kb = ntrip * per_trip + i
        step(kb, bufs[i % 2], kb + 1 if i + 1 < per_trip else None, bufs[(i + 1) % 2])

    lv = lam_ref[...]
    e1 = jnp.exp(jnp.sum(lv[0:1] * lv[1:2], axis=-1, keepdims=True))
    e2 = jnp.exp(jnp.sum(lv[2:3] * lv[3:4], axis=-1, keepdims=True))
    lam = e1 - e2 + lam_init
    o = acc_ref[0] / l_ref[:, 0:t] - lam * (acc_ref[1] / l_ref[:, t:2 * t])
    ms = jnp.mean(o * o, axis=0, keepdims=True)
    o = o * lax.rsqrt(ms + EPS) * subg_ref[...] * (1.0 - lam_init)
    oT_ref[...] = o.astype(BF16)


def _diff_attention(qT, k, vT, lamv, subg, *, lam_init):
    b, nqb, nq, t = qT.shape
    s = k.shape[1]
    nkb, tk = vT.shape[1], vT.shape[3]
    heads = nq // (2 * HEAD_DIM)
    assert nkb % DENSE_PER_TRIP == 0 and DENSE_PER_TRIP % 2 == 0
    kern = functools.partial(_diff_attn_kernel, t=t, tk=tk, nkb=nkb, cw=ATTN_CHUNK, lead=ATTN_LEAD,
                             per_trip=DENSE_PER_TRIP, lam_init=lam_init)
    return pl.pallas_call(
        kern,
        grid=(b, heads, nqb),
        in_specs=[
            pl.BlockSpec(lamv.shape, lambda bi, h, i: (0, 0)),
            pl.BlockSpec(subg.shape, lambda bi, h, i: (0, 0)),
            pl.BlockSpec((None, None, 2 * HEAD_DIM, t), lambda bi, h, i: (bi, i, h, 0)),
            pl.BlockSpec((None, s, 2 * HEAD_DIM), lambda bi, h, i: (bi, 0, h)),
            pl.BlockSpec((None, nkb, 2 * HEAD_DIM, tk), lambda bi, h, i: (bi, 0, h, 0)),
        ],
        out_specs=pl.BlockSpec((None, None, 2 * HEAD_DIM, t), lambda bi, h, i: (bi, i, h, 0)),
        out_shape=jax.ShapeDtypeStruct((b, nqb, nq, t), BF16),
        scratch_shapes=[
            pltpu.VMEM((2 * HEAD_DIM, 2 * t), BF16),
            pltpu.VMEM((1, 2 * t), F32),
            pltpu.VMEM((1, 2 * t), F32),
            pltpu.VMEM((2, 2 * HEAD_DIM, t), F32),
            pltpu.VMEM((tk, 2 * t), F32),
            pltpu.VMEM((tk, 2 * t), F32),
            pltpu.VMEM((1, 2 * t), F32),
            pltpu.VMEM((1, 2 * t), F32),
        ],
        compiler_params=_params("parallel", "parallel", "arbitrary"),
        name="diff_attention",
    )(lamv, subg, qT, k, vT)


def _in_proj_tm_kernel(h_ref, g_ref, w_ref, kc_ref, ka_ref, kb_ref, q_ref, k_ref, v_ref, *, q_scale):
    u = _rms(h_ref[...], g_ref[...]).astype(BF16)
    y = jnp.dot(u, w_ref[...], preferred_element_type=F32)
    kc, ka, kb = kc_ref[...], ka_ref[...], kb_ref[...]
    nq, nk = q_ref.shape[1], k_ref.shape[1]

    def rope(j):
        x = y[:, j * LANES:(j + 1) * LANES]
        return x * kc + pltpu.roll(x, LANES - ROT_HALF, 1) * ka + pltpu.roll(x, ROT_HALF, 1) * kb

    for j in range(nq // LANES):
        q_ref[:, j * LANES:(j + 1) * LANES] = (rope(j) * q_scale).astype(q_ref.dtype)
    for j in range(nk // LANES):
        k_ref[:, j * LANES:(j + 1) * LANES] = rope(nq // LANES + j).astype(k_ref.dtype)
    v_ref[...] = y[:, nq + nk:].astype(v_ref.dtype)


def _in_proj_tm(h, g, w, kc, ka, kb, *, nq, nk, q_scale, out_dtype):
    b, s, d = h.shape
    nv = w.shape[1] - nq - nk
    tm = ROW_TILE
    const = lambda bi, i: (0, 0)
    row = lambda bi, i: (bi, i, 0)
    return pl.pallas_call(
        functools.partial(_in_proj_tm_kernel, q_scale=q_scale),
        grid=(b, s // tm),
        in_specs=[
            pl.BlockSpec((None, tm, d), row),
            pl.BlockSpec((1, d), const),
            pl.BlockSpec(w.shape, const),
            pl.BlockSpec((None, tm, LANES), row),
            pl.BlockSpec((None, tm, LANES), row),
            pl.BlockSpec((None, tm, LANES), row),
        ],
        out_specs=[pl.BlockSpec((None, tm, n), row) for n in (nq, nk, nv)],
        out_shape=[jax.ShapeDtypeStruct((b, s, n), out_dtype) for n in (nq, nk, nv)],
        compiler_params=_params("parallel", "parallel"),
        name="mixer_in_proj_tm",
    )(h, g, w, kc, ka, kb)


def _window_attn_kernel(*refs, radius, dils, per_trip, depth, has_sink):
    if has_sink:
        sink_ref, refs = refs[0], refs[1:]
    bias_ref, q_ref, k_ref, v_ref = refs[:4]
    outs, scratch = refs[4:4 + 2 * len(dils)], refs[4 + 2 * len(dils):]
    bufs, scratch = scratch[:depth], scratch[depth:]
    pair = pl.program_id(1)
    total = q_ref.shape[0]
    win = WIN_Q + 2 * radius
    nqb = total // WIN_Q
    lane = lax.broadcasted_iota(jnp.int32, (WIN_Q, 2 * HEAD_DIM), 1)
    if q_ref.dtype == F32:
        qg_ref, kg_ref, vg_ref = scratch[:3]
        onat_ref = scratch[3] if max(dils) > 1 else None
    else:
        qg_ref, kg_ref, vg_ref = q_ref, k_ref, v_ref

    for branch, dil in enumerate(dils):
        o_ref, lse_ref = outs[2 * branch], outs[2 * branch + 1]
        seq = total // dil
        if q_ref.dtype == F32:
            for src, dst in ((q_ref, qg_ref), (k_ref, kg_ref), (v_ref, vg_ref)):
                if dil > REGROUP_STRIDE:
                    assert dil == REGROUP_STRIDE ** 2
                    part = total // REGROUP_STRIDE
                    for r1 in range(REGROUP_STRIDE):
                        onat_ref[r1 * part:(r1 + 1) * part, :] = src[pl.ds(r1, part, stride=REGROUP_STRIDE), :]
                    src, outer = onat_ref, REGROUP_STRIDE
                else:
                    outer = 1
                for r in range(dil):
                    r1, r2 = r % outer, r // outer
                    rows = (pl.ds(r1 * (total // outer) + r2, seq, stride=dil // outer) if dil > 1
                            else slice(None))
                    dst[r * seq:(r + 1) * seq, :] = src[rows, :].astype(BF16)
        _window_branch(bias_ref, sink_ref if has_sink else None, qg_ref, kg_ref, vg_ref, o_ref, lse_ref,
                       onat_ref if dil > 1 else None, bufs, lane, pair, radius=radius, dil=dil, seq=seq,
                       win=win, nqb=nqb, per_trip=per_trip, depth=depth)


def _window_branch(bias_ref, sink_ref, qg_ref, kg_ref, vg_ref, o_ref, lse_ref, onat_ref, bufs, lane, pair, *,
                   radius, dil, seq, win, nqb, per_trip, depth):
    has_sink = sink_ref is not None

    def window(i):
        q0 = i * WIN_Q
        lo = (q0 // seq) * seq
        k0 = jnp.clip(q0 - radius, lo, lo + seq - win)
        return pl.multiple_of(q0, WIN_Q), pl.multiple_of(k0, radius), (q0 - k0) // radius

    def scores(i, dst_ref):
        q0, k0, _ = window(i)
        q = qg_ref[pl.ds(q0, WIN_Q), :]
        zero = jnp.zeros_like(q)
        qz = jnp.concatenate([jnp.where(lane < HEAD_DIM, q, zero), jnp.where(lane >= HEAD_DIM, q, zero)],
                             axis=0)
        dst_ref[...] = lax.dot_general(kg_ref[pl.ds(k0, win), :], qz, _NT, preferred_element_type=F32)

    def finish(i, src_ref):
        q0, k0, bidx = window(i)
        bias = bias_ref[bidx]
        s = src_ref[...] + jnp.concatenate([bias, bias], axis=1)
        m = jnp.max(s, axis=0, keepdims=True)
        if has_sink:
            unit = lax.broadcasted_iota(jnp.int32, m.shape, 1) // WIN_Q
            sk = jnp.where(unit == 0, sink_ref[pair * 2], sink_ref[pair * 2 + 1])
            m = jnp.maximum(m, sk)
        p = jnp.exp2(s - m)
        l = jnp.sum(p, axis=0, keepdims=True)
        if has_sink:
            l = l + jnp.exp2(sk - m)
        oT = lax.dot_general(vg_ref[pl.ds(k0, win), :], p.astype(BF16), _TN, preferred_element_type=F32)
        lse = m + jnp.log2(l)
        halves = []
        for u in range(2):
            cols = slice(u * WIN_Q, (u + 1) * WIN_Q)
            halves.append(oT[u * HEAD_DIM:(u + 1) * HEAD_DIM, cols] / l[:, cols])
            lse_ref[u, pl.ds(i, 1), :] = lse[:, cols]
        o = jnp.concatenate(halves, axis=0).T
        if dil > 1:
            r = q0 // seq
            onat_ref[pl.ds(r + (q0 - r * seq) * dil, WIN_Q, stride=dil), :] = o
        else:
            o_ref[pl.ds(q0, WIN_Q), :] = o.astype(BF16)

    for n in range(depth - 1):
        scores(n, bufs[n])

    def trip(j, carry):
        for n in range(per_trip):
            i = per_trip * j + n
            scores(i + depth - 1, bufs[(n + depth - 1) % depth])
            finish(i, bufs[n % depth])
        return carry

    ntrip = nqb // per_trip - 1
    lax.fori_loop(0, ntrip, trip, 0)
    for n in range(per_trip):
        i = ntrip * per_trip + n
        if n + depth - 1 < per_trip:
            scores(i + depth - 1, bufs[(n + depth - 1) % depth])
        finish(i, bufs[n % depth])
    if dil > 1:
        o_ref[...] = onat_ref[...].astype(BF16)


def _window_bias(radius):
    win = WIN_Q + 2 * radius
    i = np.arange(win)[:, None]
    j = np.arange(WIN_Q)[None, :]
    return jnp.asarray(np.stack([np.where(np.abs(i - j - b * radius) <= radius, 0.0, NEG_INF)
                                 for b in range(3)]), F32)


def _window_attention(q, k, v, sink, *, radius, dils):
    b, s, nq = q.shape
    nqb = s // WIN_Q
    npairs = nq // LANES
    win = WIN_Q + 2 * radius
    shared = k.shape[2] == LANES and npairs > 1
    has_sink = sink is not None
    per_trip = min(WIN_PER_TRIP, nqb)
    depth = min(WIN_DEPTH, per_trip)
    assert all((s // dil) % WIN_Q == 0 and s // dil >= win for dil in dils) and WIN_Q % radius == 0
    assert nqb % per_trip == 0 and (per_trip % depth == 0 or per_trip == nqb)
    assert q.dtype == F32 or max(dils) == 1
    qspec = pl.BlockSpec((None, s, LANES), lambda bi, p: (bi, 0, p))
    kvspec = pl.BlockSpec((None, s, LANES), lambda bi, p: (bi, 0, 0)) if shared else qspec
    lse_spec = pl.BlockSpec((None, None, 2, nqb, WIN_Q), lambda bi, p: (bi, p, 0, 0, 0))
    bias = _window_bias(radius)
    in_specs = [pl.BlockSpec(bias.shape, lambda bi, p: (0, 0, 0)), qspec, kvspec, kvspec]
    args = [bias, q, k, v]
    if has_sink:
        in_specs = [pl.BlockSpec(memory_space=pltpu.SMEM)] + in_specs
        args = [sink] + args
    scratch = [pltpu.VMEM((win, 2 * WIN_Q), F32)] * depth
    if q.dtype == F32:
        scratch += [pltpu.VMEM((s, LANES), BF16)] * 3
    if max(dils) > 1:
        scratch += [pltpu.VMEM((s, LANES), F32)]
    outs = pl.pallas_call(
        functools.partial(_window_attn_kernel, radius=radius, dils=dils, per_trip=per_trip, depth=depth,
                          has_sink=has_sink),
        grid=(b, npairs),
        in_specs=in_specs,
        out_specs=[qspec, lse_spec] * len(dils),
        out_shape=[jax.ShapeDtypeStruct((b, s, nq), BF16),
                   jax.ShapeDtypeStruct((b, npairs, 2, nqb, WIN_Q), F32)] * len(dils),
        scratch_shapes=scratch,
        compiler_params=_params("parallel", "parallel"),
        name="window_attention",
    )(*args)
    results = []
    for n, dil in enumerate(dils):
        o, lse = outs[2 * n], outs[2 * n + 1]
        lse = lse.reshape(b, 2 * npairs, dil, s // dil).transpose(0, 3, 2, 1).reshape(b, s, 2 * npairs)
        results.append((o, lse))
    return results


def _mixer_out(mixer_refs, wout_ref, mode):
    if mode == "feature_major":
        (oT_ref,) = mixer_refs
        ys = [lax.dot_general(oT_ref[j], wout_ref[...], _TN, preferred_element_type=F32)
              for j in range(oT_ref.shape[0])]
        return jnp.concatenate(ys, axis=0) if len(ys) > 1 else ys[0]
    if mode == "token_major":
        (o_ref,) = mixer_refs
        return jnp.dot(o_ref[...], wout_ref[...], preferred_element_type=F32)
    n = (len(mixer_refs) - 1) // 2
    o_refs, lse_refs, expand_ref = mixer_refs[:n], mixer_refs[n:2 * n], mixer_refs[2 * n]
    lses = [r[...] for r in lse_refs]
    top = functools.reduce(jnp.maximum, lses)
    es = [jnp.exp2(x - top) for x in lses]
    z = functools.reduce(jnp.add, es)
    o = None
    for e, o_ref in zip(es, o_refs):
        w = e / z
        hi = w.astype(BF16)
        lo = (w - hi.astype(F32)).astype(BF16)
        wide = jnp.dot(jnp.concatenate([hi, lo], axis=1), expand_ref[...], preferred_element_type=F32)
        term = wide * o_ref[...].astype(F32)
        o = term if o is None else o + term
    return jnp.dot(o.astype(BF16), wout_ref[...], preferred_element_type=F32)


def _mid_kernel(*refs, mode, n_mixer, x_scale):
    h_ref = refs[0]
    mixer_refs = refs[1:1 + n_mixer]
    wout_ref, gmix_ref, gpre_ref, wq_ref, kT_ref, v_ref, wo_ref, gpost_ref, out_ref = refs[1 + n_mixer:]
    y = _mixer_out(mixer_refs, wout_ref, mode)
    h1 = h_ref[...] + _rms(y, gmix_ref[...])

    u = _rms(h1, gpre_ref[...]).astype(BF16)
    q = (jnp.dot(u, wq_ref[...], preferred_element_type=F32) * x_scale).astype(BF16)
    xd = q.shape[1] // X_HEADS
    outs = []
    for hd in range(X_HEADS):
        s = jnp.dot(q[:, hd * xd:(hd + 1) * xd], kT_ref[hd * xd:(hd + 1) * xd, :],
                    preferred_element_type=F32)
        p = jnp.exp2(s - jnp.max(s, axis=-1, keepdims=True))
        l = jnp.sum(p, axis=-1, keepdims=True)
        o = jnp.dot(p.astype(BF16), v_ref[:, hd * xd:(hd + 1) * xd], preferred_element_type=F32)
        outs.append((o / l).astype(BF16))
    y2 = jnp.dot(jnp.concatenate(outs, axis=1), wo_ref[...], preferred_element_type=F32)
    out_ref[...] = h1 + _rms(y2, gpost_ref[...])


def _mid(h, mixer, mode, wout, gmix, gpre, wq, kT, v, wo, gpost, *, x_scale):
    b, s, d = h.shape
    tm = ROW_TILE
    n_mem = v.shape[1]
    const = lambda bi, i: (0, 0)
    row = lambda bi, i: (bi, i, 0)
    if mode == "feature_major":
        oblk = mixer[0].shape[3]
        mixer_specs = [pl.BlockSpec((None, tm // oblk, d, oblk), lambda bi, i: (bi, i, 0, 0))]
    else:
        mixer_specs = [pl.BlockSpec((None, tm, a.shape[2]), row) for a in mixer]
    if mode == "branches":
        heads = mixer[-1].shape[2]
        expand = jnp.asarray(np.tile(np.repeat(np.eye(heads), d // heads, axis=1), (2, 1)), BF16)
        mixer = list(mixer) + [expand]
        mixer_specs.append(pl.BlockSpec(expand.shape, const))
    kern = functools.partial(_mid_kernel, mode=mode, n_mixer=len(mixer), x_scale=x_scale)
    return pl.pallas_call(
        kern,
        grid=(b, s // tm),
        in_specs=[
            pl.BlockSpec((None, tm, d), row),
            *mixer_specs,
            pl.BlockSpec((d, d), const),
            pl.BlockSpec((1, d), const),
            pl.BlockSpec((1, d), const),
            pl.BlockSpec((d, d), const),
            pl.BlockSpec((None, d, n_mem), lambda bi, i: (bi, 0, 0)),
            pl.BlockSpec((None, n_mem, d), lambda bi, i: (bi, 0, 0)),
            pl.BlockSpec((d, d), const),
            pl.BlockSpec((1, d), const),
        ],
        out_specs=pl.BlockSpec((None, tm, d), lambda bi, i: (bi, i, 0)),
        out_shape=jax.ShapeDtypeStruct((b, s, d), F32),
        compiler_params=_params("parallel", "parallel"),
        name="out_proj_cross_attention",
    )(h, *mixer, wout, gmix, gpre, wq, kT, v, wo, gpost)


def _mem_kv_kernel(mem_ref, g_ref, wkT_ref, wv_ref, kT_ref, v_ref):
    mn = _rms(mem_ref[...], g_ref[...]).astype(BF16)
    kT_ref[...] = lax.dot_general(wkT_ref[...], mn, _NT, preferred_element_type=F32).astype(BF16)
    v_ref[...] = jnp.dot(mn, wv_ref[...], preferred_element_type=F32).astype(BF16)


def _mem_kv(mem, g, wkT, wv):
    depth, d = g.shape[0], g.shape[2]
    b, n_mem, _ = mem.shape
    return pl.pallas_call(
        _mem_kv_kernel,
        grid=(depth, b),
        in_specs=[
            pl.BlockSpec((None, n_mem, d), lambda li, bi: (bi, 0, 0)),
            pl.BlockSpec((None, 1, d), lambda li, bi: (li, 0, 0)),
            pl.BlockSpec((None, d, d), lambda li, bi: (li, 0, 0)),
            pl.BlockSpec((None, d, d), lambda li, bi: (li, 0, 0)),
        ],
        out_specs=[
            pl.BlockSpec((None, None, d, n_mem), lambda li, bi: (li, bi, 0, 0)),
            pl.BlockSpec((None, None, n_mem, d), lambda li, bi: (li, bi, 0, 0)),
        ],
        out_shape=[
            jax.ShapeDtypeStruct((depth, b, d, n_mem), BF16),
            jax.ShapeDtypeStruct((depth, b, n_mem, d), BF16),
        ],
        compiler_params=_params("parallel", "parallel"),
        name="memory_kv",
    )(mem, g, wkT, wv)


def _ffn_kernel(h_ref, gpre_ref, wgu_ref, wd_ref, gpost_ref, out_ref, acc_ref):
    dff = wd_ref.shape[0]
    u = _rms(h_ref[...], gpre_ref[...]).astype(BF16)
    for c in range(dff // FF_CHUNK):
        cols = slice(c * FF_CHUNK, (c + 1) * FF_CHUNK)
        g = jnp.dot(u, wgu_ref[:, cols], preferred_element_type=F32)
        up = jnp.dot(u, wgu_ref[:, dff + c * FF_CHUNK:dff + (c + 1) * FF_CHUNK], preferred_element_type=F32)
        a = (g / (1.0 + jnp.exp(-g)) * up).astype(BF16)
        part = jnp.dot(a, wd_ref[cols, :], preferred_element_type=F32)
        acc_ref[...] = part if c == 0 else acc_ref[...] + part
    out_ref[...] = h_ref[...] + _rms(acc_ref[...], gpost_ref[...])


def _ffn(h, gpre, wgu, wd, gpost):
    b, s, d = h.shape
    tm = ROW_TILE
    dff = wd.shape[0]
    assert dff % FF_CHUNK == 0
    rows = b * s
    h2 = h.reshape(rows, d)
    const = lambda i: (0, 0)
    resident = dict(pipeline_mode=pl.Buffered(1))
    out = pl.pallas_call(
        _ffn_kernel,
        grid=(rows // tm,),
        in_specs=[
            pl.BlockSpec((tm, d), lambda i: (i, 0)),
            pl.BlockSpec((1, d), const),
            pl.BlockSpec((d, 2 * dff), const, **resident),
            pl.BlockSpec((dff, d), const, **resident),
            pl.BlockSpec((1, d), const),
        ],
        out_specs=pl.BlockSpec((tm, d), lambda i: (i, 0)),
        out_shape=jax.ShapeDtypeStruct((rows, d), F32),
        scratch_shapes=[pltpu.VMEM((tm, d), F32)],
        compiler_params=_params("parallel"),
        name="swiglu_ffn",
    )(h2, gpre, wgu, wd, gpost)
    return out.reshape(b, s, d)


def _rope_tables(positions):
    inv_freq = ROPE_THETA ** (-jnp.arange(0, 2 * ROT_HALF, 2, dtype=F32) / (2 * ROT_HALF))
    ang = positions.astype(F32)[..., None] * inv_freq
    cos, sin = jnp.cos(ang), jnp.sin(ang)
    cosT, sinT = cos.transpose(0, 2, 1), sin.transpose(0, 2, 1)
    zeros = jnp.zeros_like(cos)
    pad = HEAD_DIM - 2 * ROT_HALF
    ones_tail = jnp.ones(cos.shape[:-1] + (pad,), F32)
    zero_tail = jnp.zeros(cos.shape[:-1] + (pad,), F32)
    reps = LANES // HEAD_DIM
    kc = jnp.tile(jnp.concatenate([cos, cos, ones_tail], axis=-1), reps)
    ka = jnp.tile(jnp.concatenate([-sin, zeros, zero_tail], axis=-1), reps)
    kb = jnp.tile(jnp.concatenate([zeros, sin, zero_tail], axis=-1), reps)
    return cosT, sinT, kc, ka, kb


def _row(g):
    return g.reshape(1, -1)


def kernel(x, mem, positions, mix_pre_g, mix_post_g, mem_pre_g, mem_kv_g, mem_post_g, ffn_pre_g, ffn_post_g,
           a_w_in, a_w_out, b_w_in, b_w_out, b_lam_q1, b_lam_k1, b_lam_q2, b_lam_k2, b_sub_g, c_w_in, c_w_out,
           c_sink, x_wq, x_wkv, x_wo, w_gate_up, w_down):
    depth, d = mix_pre_g.shape
    assert d % (2 * HEAD_DIM) == 0 and x.shape[1] % DENSE_BLOCK == 0 and x.shape[1] % ROW_TILE == 0
    cosT, sinT, kc, ka, kb = _rope_tables(positions)
    q_scale = HEAD_DIM ** -0.5 * LOG2E
    x_scale = (d // X_HEADS) ** -0.5 * LOG2E

    mem_kT, mem_v = _mem_kv(mem, mem_kv_g.reshape(depth, 1, d),
                            x_wkv[:, :, :d].transpose(0, 2, 1).astype(BF16), x_wkv[:, :, d:].astype(BF16))

    h = x
    for i in range(depth):
        kind, j = i % N_MIXERS, i // N_MIXERS
        g_pre = _row(mix_pre_g[i])
        if kind == 0:
            w_in, w_out = a_w_in[j], a_w_out[j]
            q, k, v = _in_proj_tm(h, g_pre, w_in.astype(BF16), kc, ka, kb, nq=d, nk=d, q_scale=q_scale,
                                  out_dtype=F32)
            radii = {window // (2 * dil) for window, dil in A_PATTERNS}
            assert len(radii) == 1
            branches = _window_attention(q, k, v, None, radius=radii.pop(), dils=tuple(dl for _, dl in A_PATTERNS))
            mixer, mode = [o for o, _ in branches] + [lse for _, lse in branches], "branches"
        elif kind == 1:
            w_in, w_out = b_w_in[j], b_w_out[j]
            wq, wk, wv = w_in[:, :d], w_in[:, d:2 * d], w_in[:, 2 * d:]
            qT, k, vT = _in_proj(h, g_pre, wq.T.astype(BF16), wk.astype(BF16), wv.T.astype(BF16),
                                 cosT, sinT, kc, ka, kb, q_scale=q_scale, qblk=DENSE_BLOCK, vblk=DENSE_KBLOCK)
            lam_init = 0.8 - 0.6 * math.exp(-0.3 * i)
            lamv = jnp.stack([b_lam_q1[j], b_lam_k1[j], b_lam_q2[j], b_lam_k2[j]]).astype(F32)
            subg = jnp.broadcast_to(b_sub_g[j].astype(F32)[:, None], (2 * HEAD_DIM, DENSE_BLOCK))
            mixer, mode = [_diff_attention(qT, k, vT, lamv, subg, lam_init=lam_init)], "feature_major"
        else:
            w_in, w_out = c_w_in[j], c_w_out[j]
            n_kv = (w_in.shape[1] - d) // (2 * HEAD_DIM)
            grp = (d // HEAD_DIM) // n_kv
            perm = np.arange(d).reshape(n_kv, grp, HEAD_DIM).transpose(1, 0, 2).reshape(-1)
            w_in = jnp.concatenate([w_in[:, :d][:, perm], w_in[:, d:]], axis=1)
            w_out = w_out[perm, :]
            q, k, v = _in_proj_tm(h, g_pre, w_in.astype(BF16), kc, ka, kb, nq=d, nk=n_kv * HEAD_DIM,
                                  q_scale=q_scale, out_dtype=BF16)
            sink = (c_sink[j].astype(F32) * LOG2E)[perm[::HEAD_DIM] // HEAD_DIM]
            mixer, mode = [_window_attention(q, k, v, sink, radius=C_RADIUS, dils=(1,))[0][0]], "token_major"
        h = _mid(h, mixer, mode, w_out.astype(BF16), _row(mix_post_g[i]), _row(mem_pre_g[i]),
                 x_wq[i].astype(BF16), mem_kT[i], mem_v[i], x_wo[i].astype(BF16), _row(mem_post_g[i]),
                 x_scale=x_scale)
        h = _ffn(h, _row(ffn_pre_g[i]), w_gate_up[i].astype(BF16), w_down[i].astype(BF16), _row(ffn_post_g[i]))
    return h
```

```python
import functools
import math

import jax
import jax.numpy as jnp
import numpy as np
from jax import lax
from jax.experimental import pallas as pl
from jax.experimental.pallas import tpu as pltpu

F32 = jnp.float32
BF16 = jnp.bfloat16

HEAD_DIM = 64
ROT_HALF = HEAD_DIM // 8
ROPE_THETA = 500000.0
EPS = 1e-6
NEG_INF = -1e30
LOG2E = 1.4426950408889634
N_MIXERS = 3

A_PATTERNS = ((128, 1), (512, 4), (2048, 16))
C_RADIUS = 128
X_HEADS = 4

LANES = 128
BF16_ROWS = 16
ROW_TILE = 1024
DENSE_BLOCK = 1024
DENSE_KBLOCK = 512
ATTN_CHUNK = 256
ATTN_LEAD = 1
DENSE_PER_TRIP = 4
WIN_Q = 128
WIN_PER_TRIP = 16
REGROUP_STRIDE = 4
WIN_DEPTH = 4
FF_CHUNK = 256
VMEM_LIMIT = 56 * 1024 * 1024

_NT = (((1,), (1,)), ((), ()))
_TN = (((0,), (0,)), ((), ()))


def _params(*sem):
    return pltpu.CompilerParams(dimension_semantics=sem, vmem_limit_bytes=VMEM_LIMIT)


def _rms(x, g):
    ms = jnp.mean(x * x, axis=-1, keepdims=True)
    return x * lax.rsqrt(ms + EPS) * g


def _in_proj_kernel(h_ref, g_ref, wqT_ref, wk_ref, wvT_ref, cosT_ref, sinT_ref, kc_ref, ka_ref, kb_ref,
                    qT_ref, k_ref, vT_ref, *, q_scale, qblk, vblk):
    tm = h_ref.shape[0]
    u = _rms(h_ref[...], g_ref[...]).astype(BF16)

    kf = jnp.dot(u, wk_ref[...], preferred_element_type=F32)
    kc, ka, kb = kc_ref[...], ka_ref[...], kb_ref[...]
    for j in range(kf.shape[1] // LANES):
        x = kf[:, j * LANES:(j + 1) * LANES]
        y = x * kc + pltpu.roll(x, LANES - ROT_HALF, 1) * ka + pltpu.roll(x, ROT_HALF, 1) * kb
        k_ref[:, j * LANES:(j + 1) * LANES] = y.astype(BF16)

    qf = lax.dot_general(wqT_ref[...], u, _NT, preferred_element_type=F32)
    c = cosT_ref[...] * q_scale
    s = sinT_ref[...] * q_scale
    for unit in range(qf.shape[0] // HEAD_DIM):
        r0 = unit * HEAD_DIM
        t1 = qf[r0:r0 + ROT_HALF]
        t2 = qf[r0 + ROT_HALF:r0 + 2 * ROT_HALF]
        rest = qf[r0 + 2 * ROT_HALF:r0 + HEAD_DIM] * q_scale
        blk = jnp.concatenate([t1 * c - t2 * s, t2 * c + t1 * s, rest], axis=0).astype(BF16)
        for jb in range(tm // qblk):
            qT_ref[jb, r0:r0 + HEAD_DIM, :] = blk[:, jb * qblk:(jb + 1) * qblk]

    vf = lax.dot_general(wvT_ref[...], u, _NT, preferred_element_type=F32).astype(BF16)
    for jb in range(tm // vblk):
        vT_ref[jb] = vf[:, jb * vblk:(jb + 1) * vblk]


def _in_proj(h, g, wqT, wk, wvT, cosT, sinT, kc, ka, kb, *, q_scale, qblk, vblk):
    b, s, d = h.shape
    nq, nk, nv = wqT.shape[0], wk.shape[1], wvT.shape[0]
    tm = ROW_TILE
    kern = functools.partial(_in_proj_kernel, q_scale=q_scale, qblk=qblk, vblk=vblk)
    const = lambda bi, i: (0, 0)
    return pl.pallas_call(
        kern,
        grid=(b, s // tm),
        in_specs=[
            pl.BlockSpec((None, tm, d), lambda bi, i: (bi, i, 0)),
            pl.BlockSpec((1, d), const),
            pl.BlockSpec((nq, d), const),
            pl.BlockSpec((d, nk), const),
            pl.BlockSpec((nv, d), const),
            pl.BlockSpec((None, ROT_HALF, tm), lambda bi, i: (bi, 0, i)),
            pl.BlockSpec((None, ROT_HALF, tm), lambda bi, i: (bi, 0, i)),
            pl.BlockSpec((None, tm, LANES), lambda bi, i: (bi, i, 0)),
            pl.BlockSpec((None, tm, LANES), lambda bi, i: (bi, i, 0)),
            pl.BlockSpec((None, tm, LANES), lambda bi, i: (bi, i, 0)),
        ],
        out_specs=[
            pl.BlockSpec((None, tm // qblk, nq, qblk), lambda bi, i: (bi, i, 0, 0)),
            pl.BlockSpec((None, tm, nk), lambda bi, i: (bi, i, 0)),
            pl.BlockSpec((None, tm // vblk, nv, vblk), lambda bi, i: (bi, i, 0, 0)),
        ],
        out_shape=[
            jax.ShapeDtypeStruct((b, s // qblk, nq, qblk), BF16),
            jax.ShapeDtypeStruct((b, s, nk), BF16),
            jax.ShapeDtypeStruct((b, s // vblk, nv, vblk), BF16),
        ],
        compiler_params=_params("parallel", "parallel"),
        name="mixer_in_proj",
    )(h, g, wqT, wk, wvT, cosT, sinT, kc, ka, kb)


def _stage_queries(qT_ref, qz_ref, t):
    q = qT_ref[...]
    row = lax.broadcasted_iota(jnp.int32, q.shape, 0)
    zero = jnp.zeros_like(q)
    qz_ref[:, 0:t] = jnp.where(row < HEAD_DIM, q, zero)
    qz_ref[:, t:2 * t] = jnp.where(row >= HEAD_DIM, q, zero)


def _scores_chunk(k_ref, kb, qz_ref, dst, c, tk, cw):
    s_ref, top_ref = dst
    kblk = k_ref[pl.ds(pl.multiple_of(kb * tk, tk), tk), :]
    cols = slice(c * cw, (c + 1) * cw)
    s = jnp.dot(kblk, qz_ref[:, cols], preferred_element_type=F32)
    s_ref[:, cols] = s
    top_ref[:, cols] = jnp.max(s, axis=0, keepdims=True)


def _block_step(kb, src, kb_next, dst, refs, *, t, tk, cw, lead):
    k_ref, vT_ref, qz_ref, m_ref, l_ref, acc_ref = refs
    s_ref, top_ref = src
    nchunk = 2 * t // cw
    v = vT_ref[kb]
    v1 = jnp.concatenate([v, jnp.ones((BF16_ROWS, v.shape[1]), BF16)], axis=0)
    if kb_next is not None:
        for c in range(lead):
            _scores_chunk(k_ref, kb_next, qz_ref, dst, c, tk, cw)
    for c in range(nchunk):
        if kb_next is not None and c + lead < nchunk:
            _scores_chunk(k_ref, kb_next, qz_ref, dst, c + lead, tk, cw)
        u, cc = divmod(c, t // cw)
        cols = slice(c * cw, (c + 1) * cw)
        acc_at = acc_ref.at[u, :, cc * cw:(cc + 1) * cw]
        m_old = m_ref[:, cols]
        m_new = jnp.maximum(m_old, top_ref[:, cols])
        alpha = jnp.exp2(m_old - m_new)
        p = jnp.exp2(s_ref[:, cols] - m_new).astype(BF16)
        pv = jnp.dot(v1, p, preferred_element_type=F32)
        l_ref[:, cols] = alpha * l_ref[:, cols] + pv[2 * HEAD_DIM:2 * HEAD_DIM + 1]
        m_ref[:, cols] = m_new
        acc_at[...] = alpha * acc_at[...] + pv[:2 * HEAD_DIM]


def _diff_attn_kernel(lam_ref, subg_ref, qT_ref, k_ref, vT_ref, oT_ref, qz_ref, m_ref, l_ref, acc_ref,
                      sa_ref, sb_ref, ta_ref, tb_ref, *, t, tk, nkb, cw, lead, per_trip, lam_init):
    _stage_queries(qT_ref, qz_ref, t)
    m_ref[...] = jnp.full(m_ref.shape, NEG_INF, F32)
    l_ref[...] = jnp.zeros(l_ref.shape, F32)
    acc_ref[...] = jnp.zeros(acc_ref.shape, F32)

    bufs = ((sa_ref, ta_ref), (sb_ref, tb_ref))
    step = functools.partial(_block_step, refs=(k_ref, vT_ref, qz_ref, m_ref, l_ref, acc_ref),
                             t=t, tk=tk, cw=cw, lead=lead)

    for c in range(2 * t // cw):
        _scores_chunk(k_ref, 0, qz_ref, bufs[0], c, tk, cw)

    def trip(j, carry):
        for i in range(per_trip):
            step(per_trip * j + i, bufs[i % 2], per_trip * j + i + 1, bufs[(i + 1) % 2])
        return carry

    ntrip = nkb // per_trip - 1
    lax.fori_loop(0, ntrip, trip, 0)
    for i in range(per_trip):
        kb = ntrip * per_trip + i
        step(kb, bufs[i % 2], kb + 1 if i + 1 < per_trip else None, bufs[(i + 1) % 2])

    lv = lam_ref[...]
    e1 = jnp.exp(jnp.sum(lv[0:1] * lv[1:2], axis=-1, keepdims=True))
    e2 = jnp.exp(jnp.sum(lv[2:3] * lv[3:4], axis=-1, keepdims=True))
    lam = e1 - e2 + lam_init
    o = acc_ref[0] / l_ref[:, 0:t] - lam * (acc_ref[1] / l_ref[:, t:2 * t])
    ms = jnp.mean(o * o, axis=0, keepdims=True)
    o = o * lax.rsqrt(ms + EPS) * subg_ref[...] * (1.0 - lam_init)
    oT_ref[...] = o.astype(BF16)


def _diff_attention(qT, k, vT, lamv, subg, *, lam_init):
    b, nqb, nq, t = qT.shape
    s = k.shape[1]
    nkb, tk = vT.shape[1], vT.shape[3]
    heads = nq // (2 * HEAD_DIM)
    assert nkb % DENSE_PER_TRIP == 0 and DENSE_PER_TRIP % 2 == 0
    kern = functools.partial(_diff_attn_kernel, t=t, tk=tk, nkb=nkb, cw=ATTN_CHUNK, lead=ATTN_LEAD,
                             per_trip=DENSE_PER_TRIP, lam_init=lam_init)
    return pl.pallas_call(
        kern,
        grid=(b, heads, nqb),
        in_specs=[
            pl.BlockSpec(lamv.shape, lambda bi, h, i: (0, 0)),
            pl.BlockSpec(subg.shape, lambda bi, h, i: (0, 0)),
            pl.BlockSpec((None, None, 2 * HEAD_DIM, t), lambda bi, h, i: (bi, i, h, 0)),
            pl.BlockSpec((None, s, 2 * HEAD_DIM), lambda bi, h, i: (bi, 0, h)),
            pl.BlockSpec((None, nkb, 2 * HEAD_DIM, tk), lambda bi, h, i: (bi, 0, h, 0)),
        ],
        out_specs=pl.BlockSpec((None, None, 2 * HEAD_DIM, t), lambda bi, h, i: (bi, i, h, 0)),
        out_shape=jax.ShapeDtypeStruct((b, nqb, nq, t), BF16),
        scratch_shapes=[
            pltpu.VMEM((2 * HEAD_DIM, 2 * t), BF16),
            pltpu.VMEM((1, 2 * t), F32),
            pltpu.VMEM((1, 2 * t), F32),
            pltpu.VMEM((2, 2 * HEAD_DIM, t), F32),
            pltpu.VMEM((tk, 2 * t), F32),
            pltpu.VMEM((tk, 2 * t), F32),
            pltpu.VMEM((1, 2 * t), F32),
            pltpu.VMEM((1, 2 * t), F32),
        ],
        compiler_params=_params("parallel", "parallel", "arbitrary"),
        name="diff_attention",
    )(lamv, subg, qT, k, vT)


def _in_proj_tm_kernel(h_ref, g_ref, w_ref, kc_ref, ka_ref, kb_ref, q_ref, k_ref, v_ref, *, q_scale):
    u = _rms(h_ref[...], g_ref[...]).astype(BF16)
    y = jnp.dot(u, w_ref[...], preferred_element_type=F32)
    kc, ka, kb = kc_ref[...], ka_ref[...], kb_ref[...]
    nq, nk = q_ref.shape[1], k_ref.shape[1]

    def rope(j):
        x = y[:, j * LANES:(j + 1) * LANES]
        return x * kc + pltpu.roll(x, LANES - ROT_HALF, 1) * ka + pltpu.roll(x, ROT_HALF, 1) * kb

    for j in range(nq // LANES):
        q_ref[:, j * LANES:(j + 1) * LANES] = (rope(j) * q_scale).astype(q_ref.dtype)
    for j in range(nk // LANES):
        k_ref[:, j * LANES:(j + 1) * LANES] = rope(nq // LANES + j).astype(k_ref.dtype)
    v_ref[...] = y[:, nq + nk:].astype(v_ref.dtype)


def _in_proj_tm(h, g, w, kc, ka, kb, *, nq, nk, q_scale, out_dtype):
    b, s, d = h.shape
    nv = w.shape[1] - nq - nk
    tm = ROW_TILE
    const = lambda bi, i: (0, 0)
    row = lambda bi, i: (bi, i, 0)
    return pl.pallas_call(
        functools.partial(_in_proj_tm_kernel, q_scale=q_scale),
        grid=(b, s // tm),
        in_specs=[
            pl.BlockSpec((None, tm, d), row),
            pl.BlockSpec((1, d), const),
            pl.BlockSpec(w.shape, const),
            pl.BlockSpec((None, tm, LANES), row),
            pl.BlockSpec((None, tm, LANES), row),
            pl.BlockSpec((None, tm, LANES), row),
        ],
        out_specs=[pl.BlockSpec((None, tm, n), row) for n in (nq, nk, nv)],
        out_shape=[jax.ShapeDtypeStruct((b, s, n), out_dtype) for n in (nq, nk, nv)],
        compiler_params=_params("parallel", "parallel"),
        name="mixer_in_proj_tm",
    )(h, g, w, kc, ka, kb)


def _window_attn_kernel(*refs, radius, dils, per_trip, depth, has_sink):
    if has_sink:
        sink_ref, refs = refs[0], refs[1:]
    bias_ref, q_ref, k_ref, v_ref = refs[:4]
    outs, scratch = refs[4:4 + 2 * len(dils)], refs[4 + 2 * len(dils):]
    bufs, scratch = scratch[:depth], scratch[depth:]
    pair = pl.program_id(1)
    total = q_ref.shape[0]
    win = WIN_Q + 2 * radius
    nqb = total // WIN_Q
    lane = lax.broadcasted_iota(jnp.int32, (WIN_Q, 2 * HEAD_DIM), 1)
    if q_ref.dtype == F32:
        qg_ref, kg_ref, vg_ref = scratch[:3]
        onat_ref = scratch[3] if max(dils) > 1 else None
    else:
        qg_ref, kg_ref, vg_ref = q_ref, k_ref, v_ref

    for branch, dil in enumerate(dils):
        o_ref, lse_ref = outs[2 * branch], outs[2 * branch + 1]
        seq = total // dil
        if q_ref.dtype == F32:
            for src, dst in ((q_ref, qg_ref), (k_ref, kg_ref), (v_ref, vg_ref)):
                if dil > REGROUP_STRIDE:
                    assert dil == REGROUP_STRIDE ** 2
                    part = total // REGROUP_STRIDE
                    for r1 in range(REGROUP_STRIDE):
                        onat_ref[r1 * part:(r1 + 1) * part, :] = src[pl.ds(r1, part, stride=REGROUP_STRIDE), :]
                    src, outer = onat_ref, REGROUP_STRIDE
                else:
                    outer = 1
                for r in range(dil):
                    r1, r2 = r % outer, r // outer
                    rows = (pl.ds(r1 * (total // outer) + r2, seq, stride=dil // outer) if dil > 1
                            else slice(None))
                    dst[r * seq:(r + 1) * seq, :] = src[rows, :].astype(BF16)
        _window_branch(bias_ref, sink_ref if has_sink else None, qg_ref, kg_ref, vg_ref, o_ref, lse_ref,
                       onat_ref if dil > 1 else None, bufs, lane, pair, radius=radius, dil=dil, seq=seq,
                       win=win, nqb=nqb, per_trip=per_trip, depth=depth)


def _window_branch(bias_ref, sink_ref, qg_ref, kg_ref, vg_ref, o_ref, lse_ref, onat_ref, bufs, lane, pair, *,
                   radius, dil, seq, win, nqb, per_trip, depth):
    has_sink = sink_ref is not None

    def window(i):
        q0 = i * WIN_Q
        lo = (q0 // seq) * seq
        k0 = jnp.clip(q0 - radius, lo, lo + seq - win)
        return pl.multiple_of(q0, WIN_Q), pl.multiple_of(k0, radius), (q0 - k0) // radius

    def scores(i, dst_ref):
        q0, k0, _ = window(i)
        q = qg_ref[pl.ds(q0, WIN_Q), :]
        zero = jnp.zeros_like(q)
        qz = jnp.concatenate([jnp.where(lane < HEAD_DIM, q, zero), jnp.where(lane >= HEAD_DIM, q, zero)],
                             axis=0)
        dst_ref[...] = lax.dot_general(kg_ref[pl.ds(k0, win), :], qz, _NT, preferred_element_type=F32)

    def finish(i, src_ref):
        q0, k0, bidx = window(i)
        bias = bias_ref[bidx]
        s = src_ref[...] + jnp.concatenate([bias, bias], axis=1)
        m = jnp.max(s, axis=0, keepdims=True)
        if has_sink:
            unit = lax.broadcasted_iota(jnp.int32, m.shape, 1) // WIN_Q
            sk = jnp.where(unit == 0, sink_ref[pair * 2], sink_ref[pair * 2 + 1])
            m = jnp.maximum(m, sk)
        p = jnp.exp2(s - m)
        l = jnp.sum(p, axis=0, keepdims=True)
        if has_sink:
            l = l + jnp.exp2(sk - m)
        oT = lax.dot_general(vg_ref[pl.ds(k0, win), :], p.astype(BF16), _TN, preferred_element_type=F32)
        lse = m + jnp.log2(l)
        halves = []
        for u in range(2):
            cols = slice(u * WIN_Q, (u + 1) * WIN_Q)
            halves.append(oT[u * HEAD_DIM:(u + 1) * HEAD_DIM, cols] / l[:, cols])
            lse_ref[u, pl.ds(i, 1), :] = lse[:, cols]
        o = jnp.concatenate(halves, axis=0).T
        if dil > 1:
            r = q0 // seq
            onat_ref[pl.ds(r + (q0 - r * seq) * dil, WIN_Q, stride=dil), :] = o
        else:
            o_ref[pl.ds(q0, WIN_Q), :] = o.astype(BF16)

    for n in range(depth - 1):
        scores(n, bufs[n])

    def trip(j, carry):
        for n in range(per_trip):
            i = per_trip * j + n
            scores(i + depth - 1, bufs[(n + depth - 1) % depth])
            finish(i, bufs[n % depth])
        return carry

    ntrip = nqb // per_trip - 1
    lax.fori_loop(0, ntrip, trip, 0)
    for n in range(per_trip):
        i = ntrip * per_trip + n
        if n + depth - 1 < per_trip:
            scores(i + depth - 1, bufs[(n + depth - 1) % depth])
        finish(i, bufs[n % depth])
    if dil > 1:
        o_ref[...] = onat_ref[...].astype(BF16)


def _window_bias(radius):
    win = WIN_Q + 2 * radius
    i = np.arange(win)[:, None]
    j = np.arange(WIN_Q)[None, :]
    return jnp.asarray(np.stack([np.where(np.abs(i - j - b * radius) <= radius, 0.0, NEG_INF)
                                 for b in range(3)]), F32)


def _window_attention(q, k, v, sink, *, radius, dils):
    b, s, nq = q.shape
    nqb = s // WIN_Q
    npairs = nq // LANES
    win = WIN_Q + 2 * radius
    shared = k.shape[2] == LANES and npairs > 1
    has_sink = sink is not None
    per_trip = min(WIN_PER_TRIP, nqb)
    depth = min(WIN_DEPTH, per_trip)
    assert all((s // dil) % WIN_Q == 0 and s // dil >= win for dil in dils) and WIN_Q % radius == 0
    assert nqb % per_trip == 0 and (per_trip % depth == 0 or per_trip == nqb)
    assert q.dtype == F32 or max(dils) == 1
    qspec = pl.BlockSpec((None, s, LANES), lambda bi, p: (bi, 0, p))
    kvspec = pl.BlockSpec((None, s, LANES), lambda bi, p: (bi, 0, 0)) if shared else qspec
    lse_spec = pl.BlockSpec((None, None, 2, nqb, WIN_Q), lambda bi, p: (bi, p, 0, 0, 0))
    bias = _window_bias(radius)
    in_specs = [pl.BlockSpec(bias.shape, lambda bi, p: (0, 0, 0)), qspec, kvspec, kvspec]
    args = [bias, q, k, v]
    if has_sink:
        in_specs = [pl.BlockSpec(memory_space=pltpu.SMEM)] + in_specs
        args = [sink] + args
    scratch = [pltpu.VMEM((win, 2 * WIN_Q), F32)] * depth
    if q.dtype == F32:
        scratch += [pltpu.VMEM((s, LANES), BF16)] * 3
    if max(dils) > 1:
        scratch += [pltpu.VMEM((s, LANES), F32)]
    outs = pl.pallas_call(
        functools.partial(_window_attn_kernel, radius=radius, dils=dils, per_trip=per_trip, depth=depth,
                          has_sink=has_sink),
        grid=(b, npairs),
        in_specs=in_specs,
        out_specs=[qspec, lse_spec] * len(dils),
        out_shape=[jax.ShapeDtypeStruct((b, s, nq), BF16),
                   jax.ShapeDtypeStruct((b, npairs, 2, nqb, WIN_Q), F32)] * len(dils),
        scratch_shapes=scratch,
        compiler_params=_params("parallel", "parallel"),
        name="window_attention",
    )(*args)
    results = []
    for n, dil in enumerate(dils):
        o, lse = outs[2 * n], outs[2 * n + 1]
        lse = lse.reshape(b, 2 * npairs, dil, s // dil).transpose(0, 3, 2, 1).reshape(b, s, 2 * npairs)
        results.append((o, lse))
    return results


def _mixer_out(mixer_refs, wout_ref, mode):
    if mode == "feature_major":
        (oT_ref,) = mixer_refs
        ys = [lax.dot_general(oT_ref[j], wout_ref[...], _TN, preferred_element_type=F32)
              for j in range(oT_ref.shape[0])]
        return jnp.concatenate(ys, axis=0) if len(ys) > 1 else ys[0]
    if mode == "token_major":
        (o_ref,) = mixer_refs
        return jnp.dot(o_ref[...], wout_ref[...], preferred_element_type=F32)
    n = (len(mixer_refs) - 1) // 2
    o_refs, lse_refs, expand_ref = mixer_refs[:n], mixer_refs[n:2 * n], mixer_refs[2 * n]
    lses = [r[...] for r in lse_refs]
    top = functools.reduce(jnp.maximum, lses)
    es = [jnp.exp2(x - top) for x in lses]
    z = functools.reduce(jnp.add, es)
    o = None
    for e, o_ref in zip(es, o_refs):
        w = e / z
        hi = w.astype(BF16)
        lo = (w - hi.astype(F32)).astype(BF16)
        wide = jnp.dot(jnp.concatenate([hi, lo], axis=1), expand_ref[...], preferred_element_type=F32)
        term = wide * o_ref[...].astype(F32)
        o = term if o is None else o + term
    return jnp.dot(o.astype(BF16), wout_ref[...], preferred_element_type=F32)


def _mid_kernel(*refs, mode, n_mixer, x_scale):
    h_ref = refs[0]
    mixer_refs = refs[1:1 + n_mixer]
    wout_ref, gmix_ref, gpre_ref, wq_ref, kT_ref, v_ref, wo_ref, gpost_ref, out_ref = refs[1 + n_mixer:]
    y = _mixer_out(mixer_refs, wout_ref, mode)
    h1 = h_ref[...] + _rms(y, gmix_ref[...])

    u = _rms(h1, gpre_ref[...]).astype(BF16)
    q = (jnp.dot(u, wq_ref[...], preferred_element_type=F32) * x_scale).astype(BF16)
    xd = q.shape[1] // X_HEADS
    outs = []
    for hd in range(X_HEADS):
        s = jnp.dot(q[:, hd * xd:(hd + 1) * xd], kT_ref[hd * xd:(hd + 1) * xd, :],
                    preferred_element_type=F32)
        p = jnp.exp2(s - jnp.max(s, axis=-1, keepdims=True))
        l = jnp.sum(p, axis=-1, keepdims=True)
        o = jnp.dot(p.astype(BF16), v_ref[:, hd * xd:(hd + 1) * xd], preferred_element_type=F32)
        outs.append((o / l).astype(BF16))
    y2 = jnp.dot(jnp.concatenate(outs, axis=1), wo_ref[...], preferred_element_type=F32)
    out_ref[...] = h1 + _rms(y2, gpost_ref[...])


def _mid(h, mixer, mode, wout, gmix, gpre, wq, kT, v, wo, gpost, *, x_scale):
    b, s, d = h.shape
    tm = ROW_TILE
    n_mem = v.shape[1]
    const = lambda bi, i: (0, 0)
    row = lambda bi, i: (bi, i, 0)
    if mode == "feature_major":
        oblk = mixer[0].shape[3]
        mixer_specs = [pl.BlockSpec((None, tm // oblk, d, oblk), lambda bi, i: (bi, i, 0, 0))]
    else:
        mixer_specs = [pl.BlockSpec((None, tm, a.shape[2]), row) for a in mixer]
    if mode == "branches":
        heads = mixer[-1].shape[2]
        expand = jnp.asarray(np.tile(np.repeat(np.eye(heads), d // heads, axis=1), (2, 1)), BF16)
        mixer = list(mixer) + [expand]
        mixer_specs.append(pl.BlockSpec(expand.shape, const))
    kern = functools.partial(_mid_kernel, mode=mode, n_mixer=len(mixer), x_scale=x_scale)
    return pl.pallas_call(
        kern,
        grid=(b, s // tm),
        in_specs=[
            pl.BlockSpec((None, tm, d), row),
            *mixer_specs,
            pl.BlockSpec((d, d), const),
            pl.BlockSpec((1, d), const),
            pl.BlockSpec((1, d), const),
            pl.BlockSpec((d, d), const),
            pl.BlockSpec((None, d, n_mem), lambda bi, i: (bi, 0, 0)),
            pl.BlockSpec((None, n_mem, d), lambda bi, i: (bi, 0, 0)),
            pl.BlockSpec((d, d), const),
            pl.BlockSpec((1, d), const),
        ],
        out_specs=pl.BlockSpec((None, tm, d), lambda bi, i: (bi, i, 0)),
        out_shape=jax.ShapeDtypeStruct((b, s, d), F32),
        compiler_params=_params("parallel", "parallel"),
        name="out_proj_cross_attention",
    )(h, *mixer, wout, gmix, gpre, wq, kT, v, wo, gpost)


def _mem_kv_kernel(mem_ref, g_ref, wkT_ref, wv_ref, kT_ref, v_ref):
    mn = _rms(mem_ref[...], g_ref[...]).astype(BF16)
    kT_ref[...] = lax.dot_general(wkT_ref[...], mn, _NT, preferred_element_type=F32).astype(BF16)
    v_ref[...] = jnp.dot(mn, wv_ref[...], preferred_element_type=F32).astype(BF16)


def _mem_kv(mem, g, wkT, wv):
    depth, d = g.shape[0], g.shape[2]
    b, n_mem, _ = mem.shape
    return pl.pallas_call(
        _mem_kv_kernel,
        grid=(depth, b),
        in_specs=[
            pl.BlockSpec((None, n_mem, d), lambda li, bi: (bi, 0, 0)),
            pl.BlockSpec((None, 1, d), lambda li, bi: (li, 0, 0)),
            pl.BlockSpec((None, d, d), lambda li, bi: (li, 0, 0)),
            pl.BlockSpec((None, d, d), lambda li, bi: (li, 0, 0)),
        ],
        out_specs=[
            pl.BlockSpec((None, None, d, n_mem), lambda li, bi: (li, bi, 0, 0)),
            pl.BlockSpec((None, None, n_mem, d), lambda li, bi: (li, bi, 0, 0)),
        ],
        out_shape=[
            jax.ShapeDtypeStruct((depth, b, d, n_mem), BF16),
            jax.ShapeDtypeStruct((depth, b, n_mem, d), BF16),
        ],
        compiler_params=_params("parallel", "parallel"),
        name="memory_kv",
    )(mem, g, wkT, wv)


def _ffn_kernel(h_ref, gpre_ref, wgu_ref, wd_ref, gpost_ref, out_ref, acc_ref):
    dff = wd_ref.shape[0]
    u = _rms(h_ref[...], gpre_ref[...]).astype(BF16)
    for c in range(dff // FF_CHUNK):
        cols = slice(c * FF_CHUNK, (c + 1) * FF_CHUNK)
        g = jnp.dot(u, wgu_ref[:, cols], preferred_element_type=F32)
        up = jnp.dot(u, wgu_ref[:, dff + c * FF_CHUNK:dff + (c + 1) * FF_CHUNK], preferred_element_type=F32)
        a = (g / (1.0 + jnp.exp(-g)) * up).astype(BF16)
        part = jnp.dot(a, wd_ref[cols, :], preferred_element_type=F32)
        acc_ref[...] = part if c == 0 else acc_ref[...] + part
    out_ref[...] = h_ref[...] + _rms(acc_ref[...], gpost_ref[...])


def _ffn(h, gpre, wgu, wd, gpost):
    b, s, d = h.shape
    tm = ROW_TILE
    dff = wd.shape[0]
    assert dff % FF_CHUNK == 0
    rows = b * s
    h2 = h.reshape(rows, d)
    const = lambda i: (0, 0)
    resident = dict(pipeline_mode=pl.Buffered(1))
    out = pl.pallas_call(
        _ffn_kernel,
        grid=(rows // tm,),
        in_specs=[
            pl.BlockSpec((tm, d), lambda i: (i, 0)),
            pl.BlockSpec((1, d), const),
            pl.BlockSpec((d, 2 * dff), const, **resident),
            pl.BlockSpec((dff, d), const, **resident),
            pl.BlockSpec((1, d), const),
        ],
        out_specs=pl.BlockSpec((tm, d), lambda i: (i, 0)),
        out_shape=jax.ShapeDtypeStruct((rows, d), F32),
        scratch_shapes=[pltpu.VMEM((tm, d), F32)],
        compiler_params=_params("parallel"),
        name="swiglu_ffn",
    )(h2, gpre, wgu, wd, gpost)
    return out.reshape(b, s, d)


def _rope_tables(positions):
    inv_freq = ROPE_THETA ** (-jnp.arange(0, 2 * ROT_HALF, 2, dtype=F32) / (2 * ROT_HALF))
    ang = positions.astype(F32)[..., None] * inv_freq
    cos, sin = jnp.cos(ang), jnp.sin(ang)
    cosT, sinT = cos.transpose(0, 2, 1), sin.transpose(0, 2, 1)
    zeros = jnp.zeros_like(cos)
    pad = HEAD_DIM - 2 * ROT_HALF
    ones_tail = jnp.ones(cos.shape[:-1] + (pad,), F32)
    zero_tail = jnp.zeros(cos.shape[:-1] + (pad,), F32)
    reps = LANES // HEAD_DIM
    kc = jnp.tile(jnp.concatenate([cos, cos, ones_tail], axis=-1), reps)
    ka = jnp.tile(jnp.concatenate([-sin, zeros, zero_tail], axis=-1), reps)
    kb = jnp.tile(jnp.concatenate([zeros, sin, zero_tail], axis=-1), reps)
    return cosT, sinT, kc, ka, kb


def _row(g):
    return g.reshape(1, -1)


def kernel(x, mem, positions, mix_pre_g, mix_post_g, mem_pre_g, mem_kv_g, mem_post_g, ffn_pre_g, ffn_post_g,
           a_w_in, a_w_out, b_w_in, b_w_out, b_lam_q1, b_lam_k1, b_lam_q2, b_lam_k2, b_sub_g, c_w_in, c_w_out,
           c_sink, x_wq, x_wkv, x_wo, w_gate_up, w_down):
    depth, d = mix_pre_g.shape
    assert d % (2 * HEAD_DIM) == 0 and x.shape[1] % DENSE_BLOCK == 0 and x.shape[1] % ROW_TILE == 0
    cosT, sinT, kc, ka, kb = _rope_tables(positions)
    q_scale = HEAD_DIM ** -0.5 * LOG2E
    x_scale = (d // X_HEADS) ** -0.5 * LOG2E

    mem_kT, mem_v = _mem_kv(mem, mem_kv_g.reshape(depth, 1, d),
                            x_wkv[:, :, :d].transpose(0, 2, 1).astype(BF16), x_wkv[:, :, d:].astype(BF16))

    h = x
    for i in range(depth):
        kind, j = i % N_MIXERS, i // N_MIXERS
        g_pre = _row(mix_pre_g[i])
        if kind == 0:
            w_in, w_out = a_w_in[j], a_w_out[j]
            q, k, v = _in_proj_tm(h, g_pre, w_in.astype(BF16), kc, ka, kb, nq=d, nk=d, q_scale=q_scale,
                                  out_dtype=F32)
            radii = {window // (2 * dil) for window, dil in A_PATTERNS}
            assert len(radii) == 1
            branches = _window_attention(q, k, v, None, radius=radii.pop(), dils=tuple(dl for _, dl in A_PATTERNS))
            mixer, mode = [o for o, _ in branches] + [lse for _, lse in branches], "branches"
        elif kind == 1:
            w_in, w_out = b_w_in[j], b_w_out[j]
            wq, wk, wv = w_in[:, :d], w_in[:, d:2 * d], w_in[:, 2 * d:]
            qT, k, vT = _in_proj(h, g_pre, wq.T.astype(BF16), wk.astype(BF16), wv.T.astype(BF16),
                                 cosT, sinT, kc, ka, kb, q_scale=q_scale, qblk=DENSE_BLOCK, vblk=DENSE_KBLOCK)
            lam_init = 0.8 - 0.6 * math.exp(-0.3 * i)
            lamv = jnp.stack([b_lam_q1[j], b_lam_k1[j], b_lam_q2[j], b_lam_k2[j]]).astype(F32)
            subg = jnp.broadcast_to(b_sub_g[j].astype(F32)[:, None], (2 * HEAD_DIM, DENSE_BLOCK))
            mixer, mode = [_diff_attention(qT, k, vT, lamv, subg, lam_init=lam_init)], "feature_major"
        else:
            w_in, w_out = c_w_in[j], c_w_out[j]
            n_kv = (w_in.shape[1] - d) // (2 * HEAD_DIM)
            grp = (d // HEAD_DIM) // n_kv
            perm = np.arange(d).reshape(n_kv, grp, HEAD_DIM).transpose(1, 0, 2).reshape(-1)
            w_in = jnp.concatenate([w_in[:, :d][:, perm], w_in[:, d:]], axis=1)
            w_out = w_out[perm, :]
            q, k, v = _in_proj_tm(h, g_pre, w_in.astype(BF16), kc, ka, kb, nq=d, nk=n_kv * HEAD_DIM,
                                  q_scale=q_scale, out_dtype=BF16)
            sink = (c_sink[j].astype(F32) * LOG2E)[perm[::HEAD_DIM] // HEAD_DIM]
            mixer, mode = [_window_attention(q, k, v, sink, radius=C_RADIUS, dils=(1,))[0][0]], "token_major"
        h = _mid(h, mixer, mode, w_out.astype(BF16), _row(mix_post_g[i]), _row(mem_pre_g[i]),
                 x_wq[i].astype(BF16), mem_kT[i], mem_v[i], x_wo[i].astype(BF16), _row(mem_post_g[i]),
                 x_scale=x_scale)
        h = _ffn(h, _row(ffn_pre_g[i]), w_gate_up[i].astype(BF16), w_down[i].astype(BF16), _row(ffn_post_g[i]))
    return h
```

```python
import functools
import math

import jax
import jax.numpy as jnp
import numpy as np
from jax import lax
from jax.experimental import pallas as pl
from jax.experimental.pallas import tpu as pltpu

F32 = jnp.float32
BF16 = jnp.bfloat16

HEAD_DIM = 64
ROT_HALF = HEAD_DIM // 8
ROPE_THETA = 500000.0
EPS = 1e-6
NEG_INF = -1e30
LOG2E = 1.4426950408889634
N_MIXERS = 3

A_PATTERNS = ((128, 1), (512, 4), (2048, 16))
C_RADIUS = 128
X_HEADS = 4

LANES = 128
BF16_ROWS = 16
ROW_TILE = 1024
DENSE_BLOCK = 1024
DENSE_KBLOCK = 512
ATTN_CHUNK = 256
ATTN_LEAD = 1
DENSE_PER_TRIP = 4
WIN_Q = 128
WIN_PER_TRIP = 16
REGROUP_STRIDE = 4
WIN_DEPTH = 4
MID_SUBTILES = 2
FF_CHUNK = 256
VMEM_LIMIT = 56 * 1024 * 1024

_NT = (((1,), (1,)), ((), ()))
_TN = (((0,), (0,)), ((), ()))


def _params(*sem):
    return pltpu.CompilerParams(dimension_semantics=sem, vmem_limit_bytes=VMEM_LIMIT)


def _rms(x, g):
    ms = jnp.mean(x * x, axis=-1, keepdims=True)
    return x * lax.rsqrt(ms + EPS) * g


def _in_proj_kernel(h_ref, g_ref, wqT_ref, wk_ref, wvT_ref, cosT_ref, sinT_ref, kc_ref, ka_ref, kb_ref,
                    qT_ref, k_ref, vT_ref, *, q_scale, qblk, vblk):
    tm = h_ref.shape[0]
    u = _rms(h_ref[...], g_ref[...]).astype(BF16)

    kf = jnp.dot(u, wk_ref[...], preferred_element_type=F32)
    kc, ka, kb = kc_ref[...], ka_ref[...], kb_ref[...]
    for j in range(kf.shape[1] // LANES):
        x = kf[:, j * LANES:(j + 1) * LANES]
        y = x * kc + pltpu.roll(x, LANES - ROT_HALF, 1) * ka + pltpu.roll(x, ROT_HALF, 1) * kb
        k_ref[:, j * LANES:(j + 1) * LANES] = y.astype(BF16)

    qf = lax.dot_general(wqT_ref[...], u, _NT, preferred_element_type=F32)
    c = cosT_ref[...] * q_scale
    s = sinT_ref[...] * q_scale
    for unit in range(qf.shape[0] // HEAD_DIM):
        r0 = unit * HEAD_DIM
        t1 = qf[r0:r0 + ROT_HALF]
        t2 = qf[r0 + ROT_HALF:r0 + 2 * ROT_HALF]
        rest = qf[r0 + 2 * ROT_HALF:r0 + HEAD_DIM] * q_scale
        blk = jnp.concatenate([t1 * c - t2 * s, t2 * c + t1 * s, rest], axis=0).astype(BF16)
        for jb in range(tm // qblk):
            qT_ref[jb, r0:r0 + HEAD_DIM, :] = blk[:, jb * qblk:(jb + 1) * qblk]

    vf = lax.dot_general(wvT_ref[...], u, _NT, preferred_element_type=F32).astype(BF16)
    for jb in range(tm // vblk):
        vT_ref[jb] = vf[:, jb * vblk:(jb + 1) * vblk]


def _in_proj(h, g, wqT, wk, wvT, cosT, sinT, kc, ka, kb, *, q_scale, qblk, vblk):
    b, s, d = h.shape
    nq, nk, nv = wqT.shape[0], wk.shape[1], wvT.shape[0]
    tm = ROW_TILE
    kern = functools.partial(_in_proj_kernel, q_scale=q_scale, qblk=qblk, vblk=vblk)
    const = lambda bi, i: (0, 0)
    return pl.pallas_call(
        kern,
        grid=(b, s // tm),
        in_specs=[
            pl.BlockSpec((None, tm, d), lambda bi, i: (bi, i, 0)),
            pl.BlockSpec((1, d), const),
            pl.BlockSpec((nq, d), const),
            pl.BlockSpec((d, nk), const),
            pl.BlockSpec((nv, d), const),
            pl.BlockSpec((None, ROT_HALF, tm), lambda bi, i: (bi, 0, i)),
            pl.BlockSpec((None, ROT_HALF, tm), lambda bi, i: (bi, 0, i)),
            pl.BlockSpec((None, tm, LANES), lambda bi, i: (bi, i, 0)),
            pl.BlockSpec((None, tm, LANES), lambda bi, i: (bi, i, 0)),
            pl.BlockSpec((None, tm, LANES), lambda bi, i: (bi, i, 0)),
        ],
        out_specs=[
            pl.BlockSpec((None, tm // qblk, nq, qblk), lambda bi, i: (bi, i, 0, 0)),
            pl.BlockSpec((None, tm, nk), lambda bi, i: (bi, i, 0)),
            pl.BlockSpec((None, tm // vblk, nv, vblk), lambda bi, i: (bi, i, 0, 0)),
        ],
        out_shape=[
            jax.ShapeDtypeStruct((b, s // qblk, nq, qblk), BF16),
            jax.ShapeDtypeStruct((b, s, nk), BF16),
            jax.ShapeDtypeStruct((b, s // vblk, nv, vblk), BF16),
        ],
        compiler_params=_params("parallel", "parallel"),
        name="mixer_in_proj",
    )(h, g, wqT, wk, wvT, cosT, sinT, kc, ka, kb)


def _stage_queries(qT_ref, qz_ref, t):
    q = qT_ref[...]
    row = lax.broadcasted_iota(jnp.int32, q.shape, 0)
    zero = jnp.zeros_like(q)
    qz_ref[:, 0:t] = jnp.where(row < HEAD_DIM, q, zero)
    qz_ref[:, t:2 * t] = jnp.where(row >= HEAD_DIM, q, zero)


def _scores_chunk(k_ref, kb, qz_ref, dst, c, tk, cw):
    s_ref, top_ref = dst
    kblk = k_ref[pl.ds(pl.multiple_of(kb * tk, tk), tk), :]
    cols = slice(c * cw, (c + 1) * cw)
    s = jnp.dot(kblk, qz_ref[:, cols], preferred_element_type=F32)
    s_ref[:, cols] = s
    top_ref[:, cols] = jnp.max(s, axis=0, keepdims=True)


def _block_step(kb, src, kb_next, dst, refs, *, t, tk, cw, lead):
    k_ref, vT_ref, qz_ref, m_ref, l_ref, acc_ref = refs
    s_ref, top_ref = src
    nchunk = 2 * t // cw
    v = vT_ref[kb]
    v1 = jnp.concatenate([v, jnp.ones((BF16_ROWS, v.shape[1]), BF16)], axis=0)
    if kb_next is not None:
        for c in range(lead):
            _scores_chunk(k_ref, kb_next, qz_ref, dst, c, tk, cw)
    for c in range(nchunk):
        if kb_next is not None and c + lead < nchunk:
            _scores_chunk(k_ref, kb_next, qz_ref, dst, c + lead, tk, cw)
        u, cc = divmod(c, t // cw)
        cols = slice(c * cw, (c + 1) * cw)
        acc_at = acc_ref.at[u, :, cc * cw:(cc + 1) * cw]
        m_old = m_ref[:, cols]
        m_new = jnp.maximum(m_old, top_ref[:, cols])
        alpha = jnp.exp2(m_old - m_new)
        p = jnp.exp2(s_ref[:, cols] - m_new).astype(BF16)
        pv = jnp.dot(v1, p, preferred_element_type=F32)
        l_ref[:, cols] = alpha * l_ref[:, cols] + pv[2 * HEAD_DIM:2 * HEAD_DIM + 1]
        m_ref[:, cols] = m_new
        acc_at[...] = alpha * acc_at[...] + pv[:2 * HEAD_DIM]


def _diff_attn_kernel(lam_ref, subg_ref, qT_ref, k_ref, vT_ref, oT_ref, qz_ref, m_ref, l_ref, acc_ref,
                      sa_ref, sb_ref, ta_ref, tb_ref, *, t, tk, nkb, cw, lead, per_trip, lam_init):
    _stage_queries(qT_ref, qz_ref, t)
    m_ref[...] = jnp.full(m_ref.shape, NEG_INF, F32)
    l_ref[...] = jnp.zeros(l_ref.shape, F32)
    acc_ref[...] = jnp.zeros(acc_ref.shape, F32)

    bufs = ((sa_ref, ta_ref), (sb_ref, tb_ref))
    step = functools.partial(_block_step, refs=(k_ref, vT_ref, qz_ref, m_ref, l_ref, acc_ref),
                             t=t, tk=tk, cw=cw, lead=lead)

    for c in range(2 * t // cw):
        _scores_chunk(k_ref, 0, qz_ref, bufs[0], c, tk, cw)

    def trip(j, carry):
        for i in range(per_trip):
            step(per_trip * j + i, bufs[i % 2], per_trip * j + i + 1, bufs[(i + 1) % 2])
        return carry

    ntrip = nkb // per_trip - 1
    lax.fori_loop(0, ntrip, trip, 0)
    for i in range(per_trip):
        kb = ntrip * per_trip + i
        step(kb, bufs[i % 2], kb + 1 if i + 1 < per_trip else None, bufs[(i + 1) % 2])

    lv = lam_ref[...]
    e1 = jnp.exp(jnp.sum(lv[0:1] * lv[1:2], axis=-1, keepdims=True))
    e2 = jnp.exp(jnp.sum(lv[2:3] * lv[3:4], axis=-1, keepdims=True))
    lam = e1 - e2 + lam_init
    o = acc_ref[0] / l_ref[:, 0:t] - lam * (acc_ref[1] / l_ref[:, t:2 * t])
    ms = jnp.mean(o * o, axis=0, keepdims=True)
    o = o * lax.rsqrt(ms + EPS) * subg_ref[...] * (1.0 - lam_init)
    oT_ref[...] = o.astype(BF16)


def _diff_attention(qT, k, vT, lamv, subg, *, lam_init):
    b, nqb, nq, t = qT.shape
    s = k.shape[1]
    nkb, tk = vT.shape[1], vT.shape[3]
    heads = nq // (2 * HEAD_DIM)
    assert nkb % DENSE_PER_TRIP == 0 and DENSE_PER_TRIP % 2 == 0
    kern = functools.partial(_diff_attn_kernel, t=t, tk=tk, nkb=nkb, cw=ATTN_CHUNK, lead=ATTN_LEAD,
                             per_trip=DENSE_PER_TRIP, lam_init=lam_init)
    return pl.pallas_call(
        kern,
        grid=(b, heads, nqb),
        in_specs=[
            pl.BlockSpec(lamv.shape, lambda bi, h, i: (0, 0)),
            pl.BlockSpec(subg.shape, lambda bi, h, i: (0, 0)),
            pl.BlockSpec((None, None, 2 * HEAD_DIM, t), lambda bi, h, i: (bi, i, h, 0)),
            pl.BlockSpec((None, s, 2 * HEAD_DIM), lambda bi, h, i: (bi, 0, h)),
            pl.BlockSpec((None, nkb, 2 * HEAD_DIM, tk), lambda bi, h, i: (bi, 0, h, 0)),
        ],
        out_specs=pl.BlockSpec((None, None, 2 * HEAD_DIM, t), lambda bi, h, i: (bi, i, h, 0)),
        out_shape=jax.ShapeDtypeStruct((b, nqb, nq, t), BF16),
        scratch_shapes=[
            pltpu.VMEM((2 * HEAD_DIM, 2 * t), BF16),
            pltpu.VMEM((1, 2 * t), F32),
            pltpu.VMEM((1, 2 * t), F32),
            pltpu.VMEM((2, 2 * HEAD_DIM, t), F32),
            pltpu.VMEM((tk, 2 * t), F32),
            pltpu.VMEM((tk, 2 * t), F32),
            pltpu.VMEM((1, 2 * t), F32),
            pltpu.VMEM((1, 2 * t), F32),
        ],
        compiler_params=_params("parallel", "parallel", "arbitrary"),
        name="diff_attention",
    )(lamv, subg, qT, k, vT)


def _in_proj_tm_kernel(h_ref, g_ref, w_ref, kc_ref, ka_ref, kb_ref, q_ref, k_ref, v_ref, *, q_scale):
    u = _rms(h_ref[...], g_ref[...]).astype(BF16)
    y = jnp.dot(u, w_ref[...], preferred_element_type=F32)
    kc, ka, kb = kc_ref[...], ka_ref[...], kb_ref[...]
    nq, nk = q_ref.shape[1], k_ref.shape[1]

    def rope(j):
        x = y[:, j * LANES:(j + 1) * LANES]
        return x * kc + pltpu.roll(x, LANES - ROT_HALF, 1) * ka + pltpu.roll(x, ROT_HALF, 1) * kb

    for j in range(nq // LANES):
        q_ref[:, j * LANES:(j + 1) * LANES] = (rope(j) * q_scale).astype(q_ref.dtype)
    for j in range(nk // LANES):
        k_ref[:, j * LANES:(j + 1) * LANES] = rope(nq // LANES + j).astype(k_ref.dtype)
    v_ref[...] = y[:, nq + nk:].astype(v_ref.dtype)


def _in_proj_tm(h, g, w, kc, ka, kb, *, nq, nk, q_scale, out_dtype):
    b, s, d = h.shape
    nv = w.shape[1] - nq - nk
    tm = ROW_TILE
    const = lambda bi, i: (0, 0)
    row = lambda bi, i: (bi, i, 0)
    return pl.pallas_call(
        functools.partial(_in_proj_tm_kernel, q_scale=q_scale),
        grid=(b, s // tm),
        in_specs=[
            pl.BlockSpec((None, tm, d), row),
            pl.BlockSpec((1, d), const),
            pl.BlockSpec(w.shape, const),
            pl.BlockSpec((None, tm, LANES), row),
            pl.BlockSpec((None, tm, LANES), row),
            pl.BlockSpec((None, tm, LANES), row),
        ],
        out_specs=[pl.BlockSpec((None, tm, n), row) for n in (nq, nk, nv)],
        out_shape=[jax.ShapeDtypeStruct((b, s, n), out_dtype) for n in (nq, nk, nv)],
        compiler_params=_params("parallel", "parallel"),
        name="mixer_in_proj_tm",
    )(h, g, w, kc, ka, kb)


def _window_attn_kernel(*refs, radius, dils, per_trip, depth, has_sink):
    if has_sink:
        sink_ref, refs = refs[0], refs[1:]
    bias_ref, q_ref, k_ref, v_ref = refs[:4]
    outs, scratch = refs[4:4 + 2 * len(dils)], refs[4 + 2 * len(dils):]
    bufs, scratch = scratch[:depth], scratch[depth:]
    pair = pl.program_id(1)
    total = q_ref.shape[0]
    win = WIN_Q + 2 * radius
    nqb = total // WIN_Q
    lane = lax.broadcasted_iota(jnp.int32, (WIN_Q, 2 * HEAD_DIM), 1)
    if q_ref.dtype == F32:
        qg_ref, kg_ref, vg_ref = scratch[:3]
        onat_ref = scratch[3] if max(dils) > 1 else None
    else:
        qg_ref, kg_ref, vg_ref = q_ref, k_ref, v_ref

    for branch, dil in enumerate(dils):
        o_ref, lse_ref = outs[2 * branch], outs[2 * branch + 1]
        seq = total // dil
        if q_ref.dtype == F32:
            for src, dst in ((q_ref, qg_ref), (k_ref, kg_ref), (v_ref, vg_ref)):
                if dil > REGROUP_STRIDE:
                    assert dil == REGROUP_STRIDE ** 2
                    part = total // REGROUP_STRIDE
                    for r1 in range(REGROUP_STRIDE):
                        onat_ref[r1 * part:(r1 + 1) * part, :] = src[pl.ds(r1, part, stride=REGROUP_STRIDE), :]
                    src, outer = onat_ref, REGROUP_STRIDE
                else:
                    outer = 1
                for r in range(dil):
                    r1, r2 = r % outer, r // outer
                    rows = (pl.ds(r1 * (total // outer) + r2, seq, stride=dil // outer) if dil > 1
                            else slice(None))
                    dst[r * seq:(r + 1) * seq, :] = src[rows, :].astype(BF16)
        _window_branch(bias_ref, sink_ref if has_sink else None, qg_ref, kg_ref, vg_ref, o_ref, lse_ref,
                       onat_ref if dil > 1 else None, bufs, lane, pair, radius=radius, dil=dil, seq=seq,
                       win=win, nqb=nqb, per_trip=per_trip, depth=depth)


def _window_branch(bias_ref, sink_ref, qg_ref, kg_ref, vg_ref, o_ref, lse_ref, onat_ref, bufs, lane, pair, *,
                   radius, dil, seq, win, nqb, per_trip, depth):
    has_sink = sink_ref is not None

    def window(i):
        q0 = i * WIN_Q
        lo = (q0 // seq) * seq
        k0 = jnp.clip(q0 - radius, lo, lo + seq - win)
        return pl.multiple_of(q0, WIN_Q), pl.multiple_of(k0, radius), (q0 - k0) // radius

    def scores(i, dst_ref):
        q0, k0, _ = window(i)
        q = qg_ref[pl.ds(q0, WIN_Q), :]
        zero = jnp.zeros_like(q)
        qz = jnp.concatenate([jnp.where(lane < HEAD_DIM, q, zero), jnp.where(lane >= HEAD_DIM, q, zero)],
                             axis=0)
        dst_ref[...] = lax.dot_general(kg_ref[pl.ds(k0, win), :], qz, _NT, preferred_element_type=F32)

    def finish(i, src_ref):
        q0, k0, bidx = window(i)
        bias = bias_ref[bidx]
        s = src_ref[...] + jnp.concatenate([bias, bias], axis=1)
        m = jnp.max(s, axis=0, keepdims=True)
        if has_sink:
            unit = lax.broadcasted_iota(jnp.int32, m.shape, 1) // WIN_Q
            sk = jnp.where(unit == 0, sink_ref[pair * 2], sink_ref[pair * 2 + 1])
            m = jnp.maximum(m, sk)
        p = jnp.exp2(s - m)
        l = jnp.sum(p, axis=0, keepdims=True)
        if has_sink:
            l = l + jnp.exp2(sk - m)
        oT = lax.dot_general(vg_ref[pl.ds(k0, win), :], p.astype(BF16), _TN, preferred_element_type=F32)
        lse = m + jnp.log2(l)
        halves = []
        for u in range(2):
            cols = slice(u * WIN_Q, (u + 1) * WIN_Q)
            halves.append(oT[u * HEAD_DIM:(u + 1) * HEAD_DIM, cols] / l[:, cols])
            lse_ref[u, pl.ds(i, 1), :] = lse[:, cols]
        o = jnp.concatenate(halves, axis=0).T
        if dil > 1:
            r = q0 // seq
            onat_ref[pl.ds(r + (q0 - r * seq) * dil, WIN_Q, stride=dil), :] = o
        else:
            o_ref[pl.ds(q0, WIN_Q), :] = o.astype(BF16)

    for n in range(depth - 1):
        scores(n, bufs[n])

    def trip(j, carry):
        for n in range(per_trip):
            i = per_trip * j + n
            scores(i + depth - 1, bufs[(n + depth - 1) % depth])
            finish(i, bufs[n % depth])
        return carry

    ntrip = nqb // per_trip - 1
    lax.fori_loop(0, ntrip, trip, 0)
    for n in range(per_trip):
        i = ntrip * per_trip + n
        if n + depth - 1 < per_trip:
            scores(i + depth - 1, bufs[(n + depth - 1) % depth])
        finish(i, bufs[n % depth])
    if dil > 1:
        o_ref[...] = onat_ref[...].astype(BF16)


def _window_bias(radius):
    win = WIN_Q + 2 * radius
    i = np.arange(win)[:, None]
    j = np.arange(WIN_Q)[None, :]
    return jnp.asarray(np.stack([np.where(np.abs(i - j - b * radius) <= radius, 0.0, NEG_INF)
                                 for b in range(3)]), F32)


def _window_attention(q, k, v, sink, *, radius, dils):
    b, s, nq = q.shape
    nqb = s // WIN_Q
    npairs = nq // LANES
    win = WIN_Q + 2 * radius
    shared = k.shape[2] == LANES and npairs > 1
    has_sink = sink is not None
    per_trip = min(WIN_PER_TRIP, nqb)
    depth = min(WIN_DEPTH, per_trip)
    assert all((s // dil) % WIN_Q == 0 and s // dil >= win for dil in dils) and WIN_Q % radius == 0
    assert nqb % per_trip == 0 and (per_trip % depth == 0 or per_trip == nqb)
    assert q.dtype == F32 or max(dils) == 1
    qspec = pl.BlockSpec((None, s, LANES), lambda bi, p: (bi, 0, p))
    kvspec = pl.BlockSpec((None, s, LANES), lambda bi, p: (bi, 0, 0)) if shared else qspec
    lse_spec = pl.BlockSpec((None, None, 2, nqb, WIN_Q), lambda bi, p: (bi, p, 0, 0, 0))
    bias = _window_bias(radius)
    in_specs = [pl.BlockSpec(bias.shape, lambda bi, p: (0, 0, 0)), qspec, kvspec, kvspec]
    args = [bias, q, k, v]
    if has_sink:
        in_specs = [pl.BlockSpec(memory_space=pltpu.SMEM)] + in_specs
        args = [sink] + args
    scratch = [pltpu.VMEM((win, 2 * WIN_Q), F32)] * depth
    if q.dtype == F32:
        scratch += [pltpu.VMEM((s, LANES), BF16)] * 3
    if max(dils) > 1:
        scratch += [pltpu.VMEM((s, LANES), F32)]
    outs = pl.pallas_call(
        functools.partial(_window_attn_kernel, radius=radius, dils=dils, per_trip=per_trip, depth=depth,
                          has_sink=has_sink),
        grid=(b, npairs),
        in_specs=in_specs,
        out_specs=[qspec, lse_spec] * len(dils),
        out_shape=[jax.ShapeDtypeStruct((b, s, nq), BF16),
                   jax.ShapeDtypeStruct((b, npairs, 2, nqb, WIN_Q), F32)] * len(dils),
        scratch_shapes=scratch,
        compiler_params=_params("parallel", "parallel"),
        name="window_attention",
    )(*args)
    results = []
    for n, dil in enumerate(dils):
        o, lse = outs[2 * n], outs[2 * n + 1]
        lse = lse.reshape(b, 2 * npairs, dil, s // dil).transpose(0, 3, 2, 1).reshape(b, s, 2 * npairs)
        results.append((o, lse))
    return results


def _mixer_out(mixer_refs, wout_ref, mode, lo, hi):
    if mode == "feature_major":
        (oT_ref,) = mixer_refs
        blk = oT_ref.shape[2]
        ys = []
        for j in range(lo // blk, -(-hi // blk)):
            a, b = max(lo, j * blk) - j * blk, min(hi, (j + 1) * blk) - j * blk
            ys.append(lax.dot_general(oT_ref[j, :, a:b], wout_ref[...], _TN, preferred_element_type=F32))
        return jnp.concatenate(ys, axis=0) if len(ys) > 1 else ys[0]
    if mode == "token_major":
        (o_ref,) = mixer_refs
        return jnp.dot(o_ref[lo:hi, :], wout_ref[...], preferred_element_type=F32)
    n = (len(mixer_refs) - 1) // 2
    o_refs, lse_refs, expand_ref = mixer_refs[:n], mixer_refs[n:2 * n], mixer_refs[2 * n]
    lses = [r[lo:hi, :] for r in lse_refs]
    top = functools.reduce(jnp.maximum, lses)
    es = [jnp.exp2(x - top) for x in lses]
    z = functools.reduce(jnp.add, es)
    o = None
    for e, o_ref in zip(es, o_refs):
        w = e / z
        w_hi = w.astype(BF16)
        w_lo = (w - w_hi.astype(F32)).astype(BF16)
        wide = jnp.dot(jnp.concatenate([w_hi, w_lo], axis=1), expand_ref[...], preferred_element_type=F32)
        term = wide * o_ref[lo:hi, :].astype(F32)
        o = term if o is None else o + term
    return jnp.dot(o.astype(BF16), wout_ref[...], preferred_element_type=F32)


def _mid_kernel(*refs, mode, n_mixer, x_scale):
    h_ref = refs[0]
    mixer_refs = refs[1:1 + n_mixer]
    wout_ref, gmix_ref, gpre_ref, wq_ref, kT_ref, v_ref, wo_ref, gpost_ref, out_ref = refs[1 + n_mixer:]
    tm = h_ref.shape[0]
    rows = tm // MID_SUBTILES
    subs = [(i * rows, (i + 1) * rows) for i in range(MID_SUBTILES)]
    ys = [_mixer_out(mixer_refs, wout_ref, mode, lo, hi) for lo, hi in subs]
    h1s = [h_ref[lo:hi, :] + _rms(y, gmix_ref[...]) for (lo, hi), y in zip(subs, ys)]
    us = [_rms(h1, gpre_ref[...]).astype(BF16) for h1 in h1s]
    qs = [(jnp.dot(u, wq_ref[...], preferred_element_type=F32) * x_scale).astype(BF16) for u in us]
    xd = wq_ref.shape[1] // X_HEADS
    outs = [[] for _ in subs]
    for hd in range(X_HEADS):
        cols = slice(hd * xd, (hd + 1) * xd)
        ss = [jnp.dot(q[:, cols], kT_ref[cols, :], preferred_element_type=F32) for q in qs]
        ps = [jnp.exp2(s - jnp.max(s, axis=-1, keepdims=True)) for s in ss]
        for i, p in enumerate(ps):
            o = jnp.dot(p.astype(BF16), v_ref[:, cols], preferred_element_type=F32)
            outs[i].append((o / jnp.sum(p, axis=-1, keepdims=True)).astype(BF16))
    y2s = [jnp.dot(jnp.concatenate(o, axis=1), wo_ref[...], preferred_element_type=F32) for o in outs]
    for (lo, hi), h1, y2 in zip(subs, h1s, y2s):
        out_ref[lo:hi, :] = h1 + _rms(y2, gpost_ref[...])


def _mid(h, mixer, mode, wout, gmix, gpre, wq, kT, v, wo, gpost, *, x_scale):
    b, s, d = h.shape
    tm = ROW_TILE
    n_mem = v.shape[1]
    const = lambda bi, i: (0, 0)
    row = lambda bi, i: (bi, i, 0)
    if mode == "feature_major":
        oblk = mixer[0].shape[3]
        mixer_specs = [pl.BlockSpec((None, tm // oblk, d, oblk), lambda bi, i: (bi, i, 0, 0))]
    else:
        mixer_specs = [pl.BlockSpec((None, tm, a.shape[2]), row) for a in mixer]
    if mode == "branches":
        heads = mixer[-1].shape[2]
        expand = jnp.asarray(np.tile(np.repeat(np.eye(heads), d // heads, axis=1), (2, 1)), BF16)
        mixer = list(mixer) + [expand]
        mixer_specs.append(pl.BlockSpec(expand.shape, const))
    kern = functools.partial(_mid_kernel, mode=mode, n_mixer=len(mixer), x_scale=x_scale)
    return pl.pallas_call(
        kern,
        grid=(b, s // tm),
        in_specs=[
            pl.BlockSpec((None, tm, d), row),
            *mixer_specs,
            pl.BlockSpec((d, d), const),
            pl.BlockSpec((1, d), const),
            pl.BlockSpec((1, d), const),
            pl.BlockSpec((d, d), const),
            pl.BlockSpec((None, d, n_mem), lambda bi, i: (bi, 0, 0)),
            pl.BlockSpec((None, n_mem, d), lambda bi, i: (bi, 0, 0)),
            pl.BlockSpec((d, d), const),
            pl.BlockSpec((1, d), const),
        ],
        out_specs=pl.BlockSpec((None, tm, d), lambda bi, i: (bi, i, 0)),
        out_shape=jax.ShapeDtypeStruct((b, s, d), F32),
        compiler_params=_params("parallel", "parallel"),
        name="out_proj_cross_attention",
    )(h, *mixer, wout, gmix, gpre, wq, kT, v, wo, gpost)


def _mem_kv_kernel(mem_ref, g_ref, wkT_ref, wv_ref, kT_ref, v_ref):
    mn = _rms(mem_ref[...], g_ref[...]).astype(BF16)
    kT_ref[...] = lax.dot_general(wkT_ref[...], mn, _NT, preferred_element_type=F32).astype(BF16)
    v_ref[...] = jnp.dot(mn, wv_ref[...], preferred_element_type=F32).astype(BF16)


def _mem_kv(mem, g, wkT, wv):
    depth, d = g.shape[0], g.shape[2]
    b, n_mem, _ = mem.shape
    return pl.pallas_call(
        _mem_kv_kernel,
        grid=(depth, b),
        in_specs=[
            pl.BlockSpec((None, n_mem, d), lambda li, bi: (bi, 0, 0)),
            pl.BlockSpec((None, 1, d), lambda li, bi: (li, 0, 0)),
            pl.BlockSpec((None, d, d), lambda li, bi: (li, 0, 0)),
            pl.BlockSpec((None, d, d), lambda li, bi: (li, 0, 0)),
        ],
        out_specs=[
            pl.BlockSpec((None, None, d, n_mem), lambda li, bi: (li, bi, 0, 0)),
            pl.BlockSpec((None, None, n_mem, d), lambda li, bi: (li, bi, 0, 0)),
        ],
        out_shape=[
            jax.ShapeDtypeStruct((depth, b, d, n_mem), BF16),
            jax.ShapeDtypeStruct((depth, b, n_mem, d), BF16),
        ],
        compiler_params=_params("parallel", "parallel"),
        name="memory_kv",
    )(mem, g, wkT, wv)


def _ffn_kernel(h_ref, gpre_ref, wgu_ref, wd_ref, gpost_ref, out_ref, acc_ref):
    dff = wd_ref.shape[0]
    u = _rms(h_ref[...], gpre_ref[...]).astype(BF16)
    for c in range(dff // FF_CHUNK):
        cols = slice(c * FF_CHUNK, (c + 1) * FF_CHUNK)
        g = jnp.dot(u, wgu_ref[:, cols], preferred_element_type=F32)
        up = jnp.dot(u, wgu_ref[:, dff + c * FF_CHUNK:dff + (c + 1) * FF_CHUNK], preferred_element_type=F32)
        a = (g / (1.0 + jnp.exp(-g)) * up).astype(BF16)
        part = jnp.dot(a, wd_ref[cols, :], preferred_element_type=F32)
        acc_ref[...] = part if c == 0 else acc_ref[...] + part
    out_ref[...] = h_ref[...] + _rms(acc_ref[...], gpost_ref[...])


def _ffn(h, gpre, wgu, wd, gpost):
    b, s, d = h.shape
    tm = ROW_TILE
    dff = wd.shape[0]
    assert dff % FF_CHUNK == 0
    rows = b * s
    h2 = h.reshape(rows, d)
    const = lambda i: (0, 0)
    resident = dict(pipeline_mode=pl.Buffered(1))
    out = pl.pallas_call(
        _ffn_kernel,
        grid=(rows // tm,),
        in_specs=[
            pl.BlockSpec((tm, d), lambda i: (i, 0)),
            pl.BlockSpec((1, d), const),
            pl.BlockSpec((d, 2 * dff), const, **resident),
            pl.BlockSpec((dff, d), const, **resident),
            pl.BlockSpec((1, d), const),
        ],
        out_specs=pl.BlockSpec((tm, d), lambda i: (i, 0)),
        out_shape=jax.ShapeDtypeStruct((rows, d), F32),
        scratch_shapes=[pltpu.VMEM((tm, d), F32)],
        compiler_params=_params("parallel"),
        name="swiglu_ffn",
    )(h2, gpre, wgu, wd, gpost)
    return out.reshape(b, s, d)


def _rope_tables(positions):
    inv_freq = ROPE_THETA ** (-jnp.arange(0, 2 * ROT_HALF, 2, dtype=F32) / (2 * ROT_HALF))
    ang = positions.astype(F32)[..., None] * inv_freq
    cos, sin = jnp.cos(ang), jnp.sin(ang)
    cosT, sinT = cos.transpose(0, 2, 1), sin.transpose(0, 2, 1)
    zeros = jnp.zeros_like(cos)
    pad = HEAD_DIM - 2 * ROT_HALF
    ones_tail = jnp.ones(cos.shape[:-1] + (pad,), F32)
    zero_tail = jnp.zeros(cos.shape[:-1] + (pad,), F32)
    reps = LANES // HEAD_DIM
    kc = jnp.tile(jnp.concatenate([cos, cos, ones_tail], axis=-1), reps)
    ka = jnp.tile(jnp.concatenate([-sin, zeros, zero_tail], axis=-1), reps)
    kb = jnp.tile(jnp.concatenate([zeros, sin, zero_tail], axis=-1), reps)
    return cosT, sinT, kc, ka, kb


def _row(g):
    return g.reshape(1, -1)


def kernel(x, mem, positions, mix_pre_g, mix_post_g, mem_pre_g, mem_kv_g, mem_post_g, ffn_pre_g, ffn_post_g,
           a_w_in, a_w_out, b_w_in, b_w_out, b_lam_q1, b_lam_k1, b_lam_q2, b_lam_k2, b_sub_g, c_w_in, c_w_out,
           c_sink, x_wq, x_wkv, x_wo, w_gate_up, w_down):
    depth, d = mix_pre_g.shape
    assert d % (2 * HEAD_DIM) == 0 and x.shape[1] % DENSE_BLOCK == 0 and x.shape[1] % ROW_TILE == 0
    cosT, sinT, kc, ka, kb = _rope_tables(positions)
    q_scale = HEAD_DIM ** -0.5 * LOG2E
    x_scale = (d // X_HEADS) ** -0.5 * LOG2E

    mem_kT, mem_v = _mem_kv(mem, mem_kv_g.reshape(depth, 1, d),
                            x_wkv[:, :, :d].transpose(0, 2, 1).astype(BF16), x_wkv[:, :, d:].astype(BF16))

    h = x
    for i in range(depth):
        kind, j = i % N_MIXERS, i // N_MIXERS
        g_pre = _row(mix_pre_g[i])
        if kind == 0:
            w_in, w_out = a_w_in[j], a_w_out[j]
            q, k, v = _in_proj_tm(h, g_pre, w_in.astype(BF16), kc, ka, kb, nq=d, nk=d, q_scale=q_scale,
                                  out_dtype=F32)
            radii = {window // (2 * dil) for window, dil in A_PATTERNS}
            assert len(radii) == 1
            branches = _window_attention(q, k, v, None, radius=radii.pop(), dils=tuple(dl for _, dl in A_PATTERNS))
            mixer, mode = [o for o, _ in branches] + [lse for _, lse in branches], "branches"
        elif kind == 1:
            w_in, w_out = b_w_in[j], b_w_out[j]
            wq, wk, wv = w_in[:, :d], w_in[:, d:2 * d], w_in[:, 2 * d:]
            qT, k, vT = _in_proj(h, g_pre, wq.T.astype(BF16), wk.astype(BF16), wv.T.astype(BF16),
                                 cosT, sinT, kc, ka, kb, q_scale=q_scale, qblk=DENSE_BLOCK, vblk=DENSE_KBLOCK)
            lam_init = 0.8 - 0.6 * math.exp(-0.3 * i)
            lamv = jnp.stack([b_lam_q1[j], b_lam_k1[j], b_lam_q2[j], b_lam_k2[j]]).astype(F32)
            subg = jnp.broadcast_to(b_sub_g[j].astype(F32)[:, None], (2 * HEAD_DIM, DENSE_BLOCK))
            mixer, mode = [_diff_attention(qT, k, vT, lamv, subg, lam_init=lam_init)], "feature_major"
        else:
            w_in, w_out = c_w_in[j], c_w_out[j]
            n_kv = (w_in.shape[1] - d) // (2 * HEAD_DIM)
            grp = (d // HEAD_DIM) // n_kv
            perm = np.arange(d).reshape(n_kv, grp, HEAD_DIM).transpose(1, 0, 2).reshape(-1)
            w_in = jnp.concatenate([w_in[:, :d][:, perm], w_in[:, d:]], axis=1)
            w_out = w_out[perm, :]
            q, k, v = _in_proj_tm(h, g_pre, w_in.astype(BF16), kc, ka, kb, nq=d, nk=n_kv * HEAD_DIM,
                                  q_scale=q_scale, out_dtype=BF16)
            sink = (c_sink[j].astype(F32) * LOG2E)[perm[::HEAD_DIM] // HEAD_DIM]
            mixer, mode = [_window_attention(q, k, v, sink, radius=C_RADIUS, dils=(1,))[0][0]], "token_major"
        h = _mid(h, mixer, mode, w_out.astype(BF16), _row(mix_post_g[i]), _row(mem_pre_g[i]),
                 x_wq[i].astype(BF16), mem_kT[i], mem_v[i], x_wo[i].astype(BF16), _row(mem_post_g[i]),
                 x_scale=x_scale)
        h = _ffn(h, _row(ffn_pre_g[i]), w_gate_up[i].astype(BF16), w_down[i].astype(BF16), _row(ffn_post_g[i]))
    return h
```
